```python
import math
import jax, jax.numpy as jnp
from jax import lax
import numpy as np

D_MODEL = 2048
BATCH = 4
SEQ = 8192
DEPTH = 4

N_MIXERS = 3
BLOCK = 128
RMS_EPS = 1e-6
SB_HEADS = 16
SB_HEAD_DIM = 128
SWA_HEADS = 32
SWA_KV_HEADS = 4
SWA_HEAD_DIM = 64
SWA_WINDOW = 128
MLA_HEADS = 16
MLA_Q_RANK = 512
MLA_KV_RANK = 512
MLA_NOPE_DIM = 128
MLA_ROPE_DIM = 64
MLA_V_DIM = 128
ROPE_THETA = 10000.0
PEER_HEADS = 8
PEER_N_KEYS = 64
PEER_N_EXPERTS = PEER_N_KEYS * PEER_N_KEYS
PEER_KEY_DIM = 256
PEER_TOPK = 16
PLE_DIM = 256

N_SB_LAYERS = (DEPTH + 2) // 3
N_SWA_LAYERS = (DEPTH + 1) // 3
N_MLA_LAYERS = DEPTH // 3

kernel_name = 'hybrid_sb_swa_mla_peer_trunk'


def rms_norm(x, g):
    xf = x.astype(jnp.float32)
    y = xf * lax.rsqrt(jnp.mean(xf * xf, axis=-1, keepdims=True) + RMS_EPS)
    return (y * g.astype(jnp.float32)).astype(x.dtype)


def query_blocks(q):
    b, s, h, d = q.shape
    return q.reshape(b, s // BLOCK, BLOCK, h, d).transpose(1, 0, 3, 2, 4)


def merge_blocks(o):
    n, b, h, blk, d = o.shape
    return o.transpose(1, 0, 3, 2, 4).reshape(b, n * blk, h * d)


def stick_breaking_attention(xn, w_qkv, w_o):
    b, s, _ = xn.shape
    qkv = (xn @ w_qkv).reshape(b, s, 3, SB_HEADS, SB_HEAD_DIM)
    qb_all = query_blocks(qkv[:, :, 0])
    kb_all = query_blocks(qkv[:, :, 1])
    vb_all = query_blocks(qkv[:, :, 2])
    pos = jnp.arange(BLOCK)
    strict_diag = pos[None, :] < pos[:, None]
    scale = SB_HEAD_DIM ** -0.5

    def q_block(args):
        n, qb = args

        def body(i, carry):
            acc, log_run = carry
            m = n - i
            kb = kb_all[m]
            vb = vb_all[m]
            z = jnp.einsum('bhqd,bhkd->bhqk', qb, kb).astype(jnp.float32) * scale
            mask = (m < n) | strict_diag
            log_keep = jnp.where(mask, jax.nn.log_sigmoid(-z), 0.0)
            incl = lax.cumsum(log_keep, axis=3, reverse=True)
            logw = jnp.where(mask, z + incl + log_run[..., None], -jnp.inf)
            acc = acc + jnp.einsum('bhqk,bhkd->bhqd', jnp.exp(logw).astype(vb.dtype), vb).astype(jnp.float32)
            return acc, log_run + incl[..., 0]

        init = (jnp.zeros(qb.shape, jnp.float32), jnp.zeros(qb.shape[:-1], jnp.float32))
        acc, _ = lax.fori_loop(0, n + 1, body, init)
        return acc.astype(qb.dtype)

    o = lax.map(q_block, (jnp.arange(s // BLOCK), qb_all))
    return merge_blocks(o) @ w_o


def alibi_slopes(n_heads):
    return 2.0 ** (-8.0 * jnp.arange(1, n_heads + 1, dtype=jnp.float32) / n_heads)


def sliding_window_attention(xn, w_qkv, w_o, sinks):
    b, s, _ = xn.shape
    grp = SWA_HEADS // SWA_KV_HEADS
    qd = SWA_HEADS * SWA_HEAD_DIM
    kd = SWA_KV_HEADS * SWA_HEAD_DIM
    qkv = xn @ w_qkv
    q = qkv[..., :qd].reshape(b, s, SWA_HEADS, SWA_HEAD_DIM)
    k = qkv[..., qd:qd + kd].reshape(b, s, SWA_KV_HEADS, SWA_HEAD_DIM)
    v = qkv[..., qd + kd:].reshape(b, s, SWA_KV_HEADS, SWA_HEAD_DIM)
    pad = ((0, 0), (SWA_WINDOW, 0), (0, 0), (0, 0))
    k_pad = jnp.pad(k, pad)
    v_pad = jnp.pad(v, pad)
    slopes = alibi_slopes(SWA_HEADS).reshape(SWA_KV_HEADS, grp)
    sink = sinks.astype(jnp.float32).reshape(SWA_KV_HEADS, grp)
    span = BLOCK + SWA_WINDOW
    scale = SWA_HEAD_DIM ** -0.5

    def block(args):
        n, qb = args
        qb = qb.reshape(b, SWA_KV_HEADS, grp, BLOCK, SWA_HEAD_DIM)
        kb = lax.dynamic_slice_in_dim(k_pad, n * BLOCK, span, axis=1)
        vb = lax.dynamic_slice_in_dim(v_pad, n * BLOCK, span, axis=1)
        q_pos = n * BLOCK + jnp.arange(BLOCK)
        k_pos = n * BLOCK - SWA_WINDOW + jnp.arange(span)
        dist = q_pos[:, None] - k_pos[None, :]
        in_window = (dist >= 0) & (dist < SWA_WINDOW) & (k_pos[None, :] >= 0)
        z = jnp.einsum('bkgqd,bskd->bkgqs', qb, kb).astype(jnp.float32) * scale
        z = z - slopes[:, :, None, None] * dist.astype(jnp.float32)
        z = jnp.where(in_window, z, -jnp.inf)
        sink_col = jnp.broadcast_to(sink[None, :, :, None, None], (b, SWA_KV_HEADS, grp, BLOCK, 1))
        prob = jax.nn.softmax(jnp.concatenate([z, sink_col], axis=-1), axis=-1)[..., :span]
        o = jnp.einsum('bkgqs,bskd->bkgqd', prob.astype(vb.dtype), vb)
        return o.reshape(b, SWA_HEADS, BLOCK, SWA_HEAD_DIM)

    o = lax.map(block, (jnp.arange(s // BLOCK), query_blocks(q)))
    return merge_blocks(o) @ w_o


def apply_rope(x):
    s = x.shape[1]
    half = MLA_ROPE_DIM // 2
    inv_freq = ROPE_THETA ** (-jnp.arange(half, dtype=jnp.float32) / half)
    ang = jnp.arange(s, dtype=jnp.float32)[:, None] * inv_freq[None, :]
    cos = jnp.cos(ang)[None, :, None, :]
    sin = jnp.sin(ang)[None, :, None, :]
    x1 = x[..., :half].astype(jnp.float32)
    x2 = x[..., half:].astype(jnp.float32)
    return jnp.concatenate([x1 * cos - x2 * sin, x1 * sin + x2 * cos], axis=-1).astype(x.dtype)


def latent_attention(xn, w_down, g_q, g_kv, w_uq, w_ukv, w_o):
    b, s, _ = xn.shape
    qk_dim = MLA_NOPE_DIM + MLA_ROPE_DIM
    down = xn @ w_down
    c_q = rms_norm(down[..., :MLA_Q_RANK], g_q)
    c_kv = rms_norm(down[..., MLA_Q_RANK:MLA_Q_RANK + MLA_KV_RANK], g_kv)
    k_rope = apply_rope(down[:, :, None, MLA_Q_RANK + MLA_KV_RANK:])
    q = (c_q @ w_uq).reshape(b, s, MLA_HEADS, qk_dim)
    q = jnp.concatenate([q[..., :MLA_NOPE_DIM], apply_rope(q[..., MLA_NOPE_DIM:])], axis=-1)
    kv = (c_kv @ w_ukv).reshape(b, s, MLA_HEADS, MLA_NOPE_DIM + MLA_V_DIM)
    k = jnp.concatenate([kv[..., :MLA_NOPE_DIM],
                         jnp.broadcast_to(k_rope, (b, s, MLA_HEADS, MLA_ROPE_DIM))], axis=-1)
    qb_all = query_blocks(q)
    kb_all = query_blocks(k)
    vb_all = query_blocks(kv[..., MLA_NOPE_DIM:])
    pos = jnp.arange(BLOCK)
    causal_diag = pos[None, :] <= pos[:, None]
    scale = qk_dim ** -0.5

    def q_block(args):
        n, qb = args

        def body(m, carry):
            acc, mx, den = carry
            kb = kb_all[m]
            vb = vb_all[m]
            z = jnp.einsum('bhqd,bhkd->bhqk', qb, kb).astype(jnp.float32) * scale
            z = jnp.where((m < n) | causal_diag, z, -jnp.inf)
            new_mx = jnp.maximum(mx, jnp.max(z, axis=-1))
            corr = jnp.exp(mx - new_mx)
            pz = jnp.exp(z - new_mx[..., None])
            den = den * corr + jnp.sum(pz, axis=-1)
            acc = acc * corr[..., None] + jnp.einsum('bhqk,bhkd->bhqd', pz.astype(vb.dtype), vb).astype(jnp.float32)
            return acc, new_mx, den

        stat_shape = qb.shape[:-1]
        init = (jnp.zeros(stat_shape + (MLA_V_DIM,), jnp.float32),
                jnp.full(stat_shape, -jnp.inf, jnp.float32),
                jnp.zeros(stat_shape, jnp.float32))
        acc, _, den = lax.fori_loop(0, n + 1, body, init)
        return (acc / den[..., None]).astype(qb.dtype)

    o = lax.map(q_block, (jnp.arange(s // BLOCK), qb_all))
    return merge_blocks(o) @ w_o


def peer_ffn(xn, w_q, sub_keys, u, v):
    b, s, d = xn.shape
    t = b * s
    xt = xn.reshape(t, d)
    qh = (xt @ w_q).reshape(t, PEER_HEADS, 2, PEER_KEY_DIM // 2)
    s1 = jnp.einsum('thc,nc->thn', qh[:, :, 0], sub_keys[0]).astype(jnp.float32)
    s2 = jnp.einsum('thc,nc->thn', qh[:, :, 1], sub_keys[1]).astype(jnp.float32)
    v1, i1 = lax.top_k(s1, PEER_TOPK)
    v2, i2 = lax.top_k(s2, PEER_TOPK)
    cand = (v1[..., :, None] + v2[..., None, :]).reshape(t, PEER_HEADS, PEER_TOPK * PEER_TOPK)
    best, ci = lax.top_k(cand, PEER_TOPK)
    e1 = jnp.take_along_axis(i1, ci // PEER_TOPK, axis=-1)
    e2 = jnp.take_along_axis(i2, ci % PEER_TOPK, axis=-1)
    experts = (e1 * PEER_N_KEYS + e2).reshape(t, PEER_HEADS * PEER_TOPK)
    gates = jax.nn.softmax(best, axis=-1).reshape(t, PEER_HEADS * PEER_TOPK)
    hidden = jax.nn.gelu(xt @ u.T)
    gate_mat = jnp.zeros((t, PEER_N_EXPERTS), jnp.float32).at[jnp.arange(t)[:, None], experts].add(gates)
    y = (gate_mat.astype(hidden.dtype) * hidden) @ v
    return y.reshape(b, s, d)


def per_layer_embedding(h, p_i, w_p, g, w_g):
    gate = jax.nn.sigmoid((rms_norm(h, g) @ w_g).astype(jnp.float32))
    return h + (p_i @ w_p) * gate.astype(h.dtype)


def setup_inputs(seed: int = 0) -> dict:
    key = jax.random.key(seed)
    ks = jax.random.split(key, 24)

    def dense(k, shape, fan_in):
        return jax.random.normal(k, shape, jnp.float32) * (fan_in ** -0.5)

    def gain(k, shape):
        return 1.0 + 0.05 * jax.random.normal(k, shape, jnp.float32)

    sb_in = SB_HEADS * SB_HEAD_DIM
    swa_cols = SWA_HEADS * SWA_HEAD_DIM + 2 * SWA_KV_HEADS * SWA_HEAD_DIM
    mla_down_cols = MLA_Q_RANK + MLA_KV_RANK + MLA_ROPE_DIM
    return {
        'x': jax.random.normal(ks[0], (BATCH, SEQ, D_MODEL), jnp.float32),
        'p': jax.random.normal(ks[1], (DEPTH, BATCH, SEQ, PLE_DIM), jnp.float32),
        'attn_norm': gain(ks[2], (DEPTH, D_MODEL)),
        'ffn_norm': gain(ks[3], (DEPTH, D_MODEL)),
        'ple_norm': gain(ks[4], (DEPTH, D_MODEL)),
        'final_norm': gain(ks[5], (D_MODEL,)),
        'sb_w_qkv': dense(ks[6], (N_SB_LAYERS, D_MODEL, 3 * sb_in), D_MODEL),
        'sb_w_o': dense(ks[7], (N_SB_LAYERS, sb_in, D_MODEL), sb_in),
        'swa_w_qkv': dense(ks[8], (N_SWA_LAYERS, D_MODEL, swa_cols), D_MODEL),
        'swa_w_o': dense(ks[9], (N_SWA_LAYERS, SWA_HEADS * SWA_HEAD_DIM, D_MODEL), SWA_HEADS * SWA_HEAD_DIM),
        'swa_sinks': 0.5 * jax.random.normal(ks[10], (N_SWA_LAYERS, SWA_HEADS), jnp.float32),
        'mla_w_down': dense(ks[11], (N_MLA_LAYERS, D_MODEL, mla_down_cols), D_MODEL),
        'mla_q_norm': gain(ks[12], (N_MLA_LAYERS, MLA_Q_RANK)),
        'mla_kv_norm': gain(ks[13], (N_MLA_LAYERS, MLA_KV_RANK)),
        'mla_w_uq': dense(ks[14], (N_MLA_LAYERS, MLA_Q_RANK, MLA_HEADS * (MLA_NOPE_DIM + MLA_ROPE_DIM)), MLA_Q_RANK),
        'mla_w_ukv': dense(ks[15], (N_MLA_LAYERS, MLA_KV_RANK, MLA_HEADS * (MLA_NOPE_DIM + MLA_V_DIM)), MLA_KV_RANK),
        'mla_w_o': dense(ks[16], (N_MLA_LAYERS, MLA_HEADS * MLA_V_DIM, D_MODEL), MLA_HEADS * MLA_V_DIM),
        'peer_w_q': dense(ks[17], (DEPTH, D_MODEL, PEER_HEADS * PEER_KEY_DIM), D_MODEL),
        'peer_sub_keys': dense(ks[18], (DEPTH, 2, PEER_N_KEYS, PEER_KEY_DIM // 2), PEER_KEY_DIM // 2),
        'peer_u': dense(ks[19], (DEPTH, PEER_N_EXPERTS, D_MODEL), D_MODEL),
        'peer_v': dense(ks[20], (DEPTH, PEER_N_EXPERTS, D_MODEL), PEER_HEADS * PEER_TOPK),
        'ple_w_p': dense(ks[21], (DEPTH, PLE_DIM, D_MODEL), PLE_DIM),
        'ple_w_g': dense(ks[22], (DEPTH, D_MODEL, D_MODEL), D_MODEL),
    }


def reference(x, p, attn_norm, ffn_norm, ple_norm, final_norm,
              sb_w_qkv, sb_w_o, swa_w_qkv, swa_w_o, swa_sinks,
              mla_w_down, mla_q_norm, mla_kv_norm, mla_w_uq, mla_w_ukv, mla_w_o,
              peer_w_q, peer_sub_keys, peer_u, peer_v, ple_w_p, ple_w_g):
    h = x
    for i in range(DEPTH):
        kind, j = i % N_MIXERS, i // N_MIXERS
        hn = rms_norm(h, attn_norm[i])
        if kind == 0:
            mix = stick_breaking_attention(hn, sb_w_qkv[j], sb_w_o[j])
        elif kind == 1:
            mix = sliding_window_attention(hn, swa_w_qkv[j], swa_w_o[j], swa_sinks[j])
        else:
            mix = latent_attention(hn, mla_w_down[j], mla_q_norm[j], mla_kv_norm[j],
                                   mla_w_uq[j], mla_w_ukv[j], mla_w_o[j])
        h = h + mix
        h = h + peer_ffn(rms_norm(h, ffn_norm[i]), peer_w_q[i], peer_sub_keys[i], peer_u[i], peer_v[i])
        h = per_layer_embedding(h, p[i], ple_w_p[i], ple_norm[i], ple_w_g[i])
    return rms_norm(h, final_norm)
```

```python
import functools
import math

import jax
import jax.numpy as jnp
from jax import lax
from jax.experimental import pallas as pl
from jax.experimental.pallas import tpu as pltpu

F32 = jnp.float32
BF16 = jnp.bfloat16

RMS_EPS = 1e-6
N_MIXERS = 3
SB_HEADS = 16
SWA_HEADS = 32
SWA_KV_HEADS = 4
SWA_WINDOW = 128
MLA_HEADS = 16
MLA_Q_RANK = 512
MLA_KV_RANK = 512
MLA_NOPE_DIM = 128
MLA_ROPE_DIM = 64
MLA_V_DIM = 128
ROPE_THETA = 10000.0
PEER_N_KEYS = 64
PEER_TOPK = 16

LANES = 128
V7X_VMEM_BYTES = 64 * 1024 * 1024
VMEM_LIMIT = V7X_VMEM_BYTES * 3 // 4

_NT = (((1,), (1,)), ((), ()))


def _params(*semantics):
    return pltpu.CompilerParams(dimension_semantics=semantics, vmem_limit_bytes=VMEM_LIMIT)


def _tile(n, want):
    if n <= want:
        return n
    t = (want // LANES) * LANES
    while t > LANES and n % t:
        t -= LANES
    assert n % t == 0, (n, want)
    return t


def _rms_scale(x, g):
    ms = jnp.mean(x * x, axis=-1, keepdims=True)
    return x * lax.rsqrt(ms + RMS_EPS) * g


def _norm_matmul_kernel(x_ref, g_ref, w_ref, o_ref, xn_ref):
    @pl.when(pl.program_id(1) == 0)
    def _():
        xn_ref[...] = _rms_scale(x_ref[...], g_ref[...]).astype(BF16)

    o_ref[...] = jnp.dot(xn_ref[...], w_ref[...], preferred_element_type=F32).astype(o_ref.dtype)


def norm_matmul(x, g, w, out_dtype, tm=512, tn=1024):
    m, k = x.shape
    n = w.shape[1]
    tm, tn = _tile(m, tm), _tile(n, tn)
    return pl.pallas_call(
        _norm_matmul_kernel,
        grid=(m // tm, n // tn),
        in_specs=[
            pl.BlockSpec((tm, k), lambda i, j: (i, 0)),
            pl.BlockSpec((1, k), lambda i, j: (0, 0)),
            pl.BlockSpec((k, tn), lambda i, j: (0, j)),
        ],
        out_specs=pl.BlockSpec((tm, tn), lambda i, j: (i, j)),
        out_shape=jax.ShapeDtypeStruct((m, n), out_dtype),
        scratch_shapes=[pltpu.VMEM((tm, k), BF16)],
        compiler_params=_params("parallel", "arbitrary"),
        name="norm_matmul",
    )(x, g.reshape(1, k), w)


def _matmul_residual_kernel(a_ref, w_ref, r_ref, o_ref):
    o_ref[...] = r_ref[...] + jnp.dot(a_ref[...], w_ref[...], preferred_element_type=F32)


def matmul_residual(a, w, res, tm=512, tn=1024):
    m, k = a.shape
    n = w.shape[1]
    tm, tn = _tile(m, tm), _tile(n, tn)
    return pl.pallas_call(
        _matmul_residual_kernel,
        grid=(m // tm, n // tn),
        in_specs=[
            pl.BlockSpec((tm, k), lambda i, j: (i, 0)),
            pl.BlockSpec((k, tn), lambda i, j: (0, j)),
            pl.BlockSpec((tm, tn), lambda i, j: (i, j)),
        ],
        out_specs=pl.BlockSpec((tm, tn), lambda i, j: (i, j)),
        out_shape=jax.ShapeDtypeStruct((m, n), F32),
        compiler_params=_params("parallel", "arbitrary"),
        name="matmul_residual",
    )(a, w, res)


def _final_norm_kernel(x_ref, g_ref, o_ref):
    o_ref[...] = _rms_scale(x_ref[...], g_ref[...])


def final_rms_norm(x, g, tm=512):
    m, k = x.shape
    tm = _tile(m, tm)
    return pl.pallas_call(
        _final_norm_kernel,
        grid=(m // tm,),
        in_specs=[pl.BlockSpec((tm, k), lambda i: (i, 0)), pl.BlockSpec((1, k), lambda i: (0, 0))],
        out_specs=pl.BlockSpec((tm, k), lambda i: (i, 0)),
        out_shape=jax.ShapeDtypeStruct((m, k), F32),
        compiler_params=_params("parallel"),
        name="final_norm",
    )(x, g.reshape(1, k))


def _sb_attn_kernel(q_ref, k_ref, v_ref, o_ref, *, blk, scale):
    i = pl.program_id(2)
    q = q_ref[0]
    d = q.shape[-1]
    row = lax.broadcasted_iota(jnp.int32, (blk, blk), 0)
    col = lax.broadcasted_iota(jnp.int32, (blk, blk), 1)
    suffix_ones = (row >= col).astype(BF16)
    strict = col < row

    def step(m, acc, log_run, diagonal):
        start = pl.multiple_of(m * blk, blk)
        kb = k_ref[0, pl.ds(start, blk), :]
        vb = v_ref[0, pl.ds(start, blk), :]
        z = lax.dot_general(q, kb, _NT, preferred_element_type=F32) * scale
        log_keep = -(jnp.maximum(z, 0.0) + jnp.log(1.0 + jnp.exp(-jnp.abs(z))))
        if diagonal:
            log_keep = jnp.where(strict, log_keep, 0.0)
        incl = jnp.dot(log_keep.astype(BF16), suffix_ones, preferred_element_type=F32)
        logw = z + incl + log_run
        if diagonal:
            logw = jnp.where(strict, logw, -jnp.inf)
        p = jnp.exp(logw)
        acc = acc + jnp.dot(p.astype(BF16), vb, preferred_element_type=F32)
        return acc, log_run + incl[:, 0:1]

    acc, log_run = step(i, jnp.zeros((blk, d), F32), jnp.zeros((blk, 1), F32), True)
    acc, log_run = lax.fori_loop(0, i, lambda t, c: step(i - 1 - t, c[0], c[1], False), (acc, log_run))
    o_ref[0] = acc.astype(o_ref.dtype)


def sb_attention(qkv, n_heads, blk=256):
    b, s, three_hd = qkv.shape
    d = three_hd // (3 * n_heads)
    assert d == LANES
    blk = min(blk, s)
    kern = functools.partial(_sb_attn_kernel, blk=blk, scale=d ** -0.5)
    return pl.pallas_call(
        kern,
        grid=(b, n_heads, s // blk),
        in_specs=[
            pl.BlockSpec((1, blk, d), lambda bi, h, i: (bi, i, h)),
            pl.BlockSpec((1, s, d), lambda bi, h, i: (bi, 0, n_heads + h)),
            pl.BlockSpec((1, s, d), lambda bi, h, i: (bi, 0, 2 * n_heads + h)),
        ],
        out_specs=pl.BlockSpec((1, blk, d), lambda bi, h, i: (bi, i, h)),
        out_shape=jax.ShapeDtypeStruct((b, s, n_heads * d), BF16),
        compiler_params=_params("parallel", "parallel", "arbitrary"),
        name="sb_attention",
    )(qkv, qkv, qkv)


def _swa_kernel(q_ref, kp_ref, kc_ref, vp_ref, vc_ref, slope_ref, sink_ref, o_ref, *, n_kv, grp, d, blk, scale):
    i = pl.program_id(1)
    rows = grp * blk
    qpos = lax.broadcasted_iota(jnp.int32, (rows, 2 * blk), 0) & (blk - 1)
    kcol = lax.broadcasted_iota(jnp.int32, (rows, 2 * blk), 1)
    dist = qpos + blk - kcol
    first_key = jnp.where(i > 0, 0, blk)
    in_window = (dist >= 0) & (dist < SWA_WINDOW) & (kcol >= first_key)
    dist_f = dist.astype(F32)
    outs = []
    for kh in range(n_kv):
        qg = jnp.concatenate(
            [q_ref[0, :, (kh * grp + g) * d:(kh * grp + g + 1) * d] for g in range(grp)], axis=0)
        kk = jnp.concatenate([kp_ref[0, :, kh * d:(kh + 1) * d], kc_ref[0, :, kh * d:(kh + 1) * d]], axis=0)
        vv = jnp.concatenate([vp_ref[0, :, kh * d:(kh + 1) * d], vc_ref[0, :, kh * d:(kh + 1) * d]], axis=0)
        z = lax.dot_general(qg, kk, _NT, preferred_element_type=F32) * scale
        z = z - slope_ref[kh] * dist_f
        z = jnp.where(in_window, z, -jnp.inf)
        sink = sink_ref[kh]
        mx = jnp.maximum(jnp.max(z, axis=-1, keepdims=True), sink)
        p = jnp.exp(z - mx)
        den = jnp.sum(p, axis=-1, keepdims=True) + jnp.exp(sink - mx)
        o = jnp.dot(p.astype(BF16), vv, preferred_element_type=F32) / den
        outs.extend(o[g * blk:(g + 1) * blk] for g in range(grp))
    o_ref[0] = jnp.concatenate(outs, axis=1).astype(o_ref.dtype)


def swa_attention(qkv, sinks, n_heads, n_kv, d):
    b, s, _ = qkv.shape
    blk = SWA_WINDOW
    grp = n_heads // n_kv
    qd, kd = n_heads * d, n_kv * d
    assert qd % kd == 0 and kd % LANES == 0 and s % blk == 0
    slopes = 2.0 ** (-8.0 * jnp.arange(1, n_heads + 1, dtype=F32) / n_heads)
    per_row = lambda a: jnp.repeat(a.astype(F32).reshape(n_kv, grp), blk, axis=1).reshape(n_kv, grp * blk, 1)
    kern = functools.partial(_swa_kernel, n_kv=n_kv, grp=grp, d=d, blk=blk, scale=d ** -0.5)
    prev = lambda bi, i: (bi, jnp.maximum(i - 1, 0), qd // kd)
    cur = lambda bi, i: (bi, i, qd // kd)
    prev_v = lambda bi, i: (bi, jnp.maximum(i - 1, 0), qd // kd + 1)
    cur_v = lambda bi, i: (bi, i, qd // kd + 1)
    const = lambda bi, i: (0, 0, 0)
    return pl.pallas_call(
        kern,
        grid=(b, s // blk),
        in_specs=[
            pl.BlockSpec((1, blk, qd), lambda bi, i: (bi, i, 0)),
            pl.BlockSpec((1, blk, kd), prev),
            pl.BlockSpec((1, blk, kd), cur),
            pl.BlockSpec((1, blk, kd), prev_v),
            pl.BlockSpec((1, blk, kd), cur_v),
            pl.BlockSpec((n_kv, grp * blk, 1), const),
            pl.BlockSpec((n_kv, grp * blk, 1), const),
        ],
        out_specs=pl.BlockSpec((1, blk, qd), lambda bi, i: (bi, i, 0)),
        out_shape=jax.ShapeDtypeStruct((b, s, qd), BF16),
        compiler_params=_params("parallel", "arbitrary"),
        name="swa_attention",
    )(qkv, qkv, qkv, qkv, qkv, per_row(slopes), per_row(sinks))


def _rope_lanes(r, cos_ref, sin_lo_ref, sin_hi_ref):
    half = MLA_ROPE_DIM // 2
    return (r * cos_ref[...]
            + pltpu.roll(r, LANES - half, 1) * sin_lo_ref[...]
            + pltpu.roll(r, half, 1) * sin_hi_ref[...])


def _mla_q_kernel(x_ref, g_ref, w_ref, cos_ref, slo_ref, shi_ref, o_ref, *, n_heads):
    xn = _rms_scale(x_ref[...], g_ref[...]).astype(BF16)
    for h in range(n_heads):
        a = jnp.dot(xn, w_ref[:, 2 * LANES * h:2 * LANES * (h + 1)], preferred_element_type=F32)
        o_ref[:, 2 * LANES * h:2 * LANES * h + LANES] = a[:, :LANES].astype(BF16)
        o_ref[:, 2 * LANES * h + LANES:2 * LANES * (h + 1)] = _rope_lanes(
            a[:, LANES:], cos_ref, slo_ref, shi_ref).astype(BF16)


def _mla_kv_kernel(x_ref, g_ref, xr_ref, w_ref, cos_ref, slo_ref, shi_ref, kv_ref, kr_ref, *, chunk):
    xn = _rms_scale(x_ref[...], g_ref[...]).astype(BF16)
    for c in range(w_ref.shape[1] // chunk):
        kv_ref[:, c * chunk:(c + 1) * chunk] = jnp.dot(
            xn, w_ref[:, c * chunk:(c + 1) * chunk], preferred_element_type=F32).astype(BF16)
    kr_ref[...] = _rope_lanes(xr_ref[...], cos_ref, slo_ref, shi_ref).astype(BF16)


def _rope_tables(s):
    half = MLA_ROPE_DIM // 2
    inv_freq = ROPE_THETA ** (-jnp.arange(half, dtype=F32) / half)
    ang = jnp.arange(s, dtype=F32)[:, None] * inv_freq[None, :]
    cos, sin = jnp.cos(ang), jnp.sin(ang)
    zeros = jnp.zeros((s, LANES - 2 * half), F32)
    z_half = jnp.zeros((s, half), F32)
    cos_t = jnp.concatenate([cos, cos, zeros], axis=1)
    sin_lo = jnp.concatenate([-sin, z_half, zeros], axis=1)
    sin_hi = jnp.concatenate([z_half, sin, zeros], axis=1)
    return cos_t, sin_lo, sin_hi


def mla_projections(down, g_q, g_kv, w_uq_pad, w_ukv, seq, tm=512):
    t = down.shape[0]
    tm = _tile(seq, tm)
    n_heads = w_uq_pad.shape[1] // (2 * LANES)
    tables = _rope_tables(seq)
    n_pos = seq // tm
    tab_spec = pl.BlockSpec((tm, LANES), lambda i: (i % n_pos, 0))
    q = pl.pallas_call(
        functools.partial(_mla_q_kernel, n_heads=n_heads),
        grid=(t // tm,),
        in_specs=[
            pl.BlockSpec((tm, MLA_Q_RANK), lambda i: (i, 0)),
            pl.BlockSpec((1, MLA_Q_RANK), lambda i: (0, 0)),
            pl.BlockSpec(w_uq_pad.shape, lambda i: (0, 0)),
            tab_spec, tab_spec, tab_spec,
        ],
        out_specs=pl.BlockSpec((tm, w_uq_pad.shape[1]), lambda i: (i, 0)),
        out_shape=jax.ShapeDtypeStruct((t, w_uq_pad.shape[1]), BF16),
        compiler_params=_params("parallel"),
        name="mla_q_proj",
    )(down, g_q.reshape(1, -1), w_uq_pad, *tables)
    assert MLA_Q_RANK == MLA_KV_RANK and (MLA_Q_RANK + MLA_KV_RANK) % LANES == 0
    kv, k_rope = pl.pallas_call(
        functools.partial(_mla_kv_kernel, chunk=_tile(w_ukv.shape[1], 1024)),
        grid=(t // tm,),
        in_specs=[
            pl.BlockSpec((tm, MLA_KV_RANK), lambda i: (i, 1)),
            pl.BlockSpec((1, MLA_KV_RANK), lambda i: (0, 0)),
            pl.BlockSpec((tm, LANES), lambda i: (i, (MLA_Q_RANK + MLA_KV_RANK) // LANES)),
            pl.BlockSpec(w_ukv.shape, lambda i: (0, 0)),
            tab_spec, tab_spec, tab_spec,
        ],
        out_specs=[pl.BlockSpec((tm, w_ukv.shape[1]), lambda i: (i, 0)),
                   pl.BlockSpec((tm, LANES), lambda i: (i, 0))],
        out_shape=[jax.ShapeDtypeStruct((t, w_ukv.shape[1]), BF16), jax.ShapeDtypeStruct((t, LANES), BF16)],
        compiler_params=_params("parallel"),
        name="mla_kv_proj",
    )(down, g_kv.reshape(1, -1), down, w_ukv, *tables)
    return q, kv, k_rope


def _mla_attn_kernel(q_ref, kn_ref, kr_ref, v_ref, o_ref, k_ref, *, blk, scale):
    i = pl.program_id(2)

    @pl.when(i == 0)
    def _():
        k_ref[:, :LANES] = kn_ref[0]
        k_ref[:, LANES:] = kr_ref[0]

    q = q_ref[0]
    row = lax.broadcasted_iota(jnp.int32, (blk, blk), 0)
    col = lax.broadcasted_iota(jnp.int32, (blk, blk), 1)
    causal = col <= row

    def step(m, carry, diagonal):
        acc, mx, den = carry
        start = pl.multiple_of(m * blk, blk)
        kb = k_ref[pl.ds(start, blk), :]
        vb = v_ref[0, pl.ds(start, blk), :]
        z = lax.dot_general(q, kb, _NT, preferred_element_type=F32) * scale
        if diagonal:
            z = jnp.where(causal, z, -jnp.inf)
        new_mx = jnp.maximum(mx, jnp.max(z, axis=-1, keepdims=True))
        corr = jnp.exp(mx - new_mx)
        pz = jnp.exp(z - new_mx)
        den = den * corr + jnp.sum(pz, axis=-1, keepdims=True)
        acc = acc * corr + jnp.dot(pz.astype(BF16), vb, preferred_element_type=F32)
        return acc, new_mx, den

    init = (jnp.zeros((blk, v_ref.shape[-1]), F32), jnp.full((blk, 1), -jnp.inf, F32), jnp.zeros((blk, 1), F32))
    carry = lax.fori_loop(0, i, lambda m, c: step(m, c, False), init)
    acc, _, den = step(i, carry, True)
    o_ref[0] = (acc / den).astype(o_ref.dtype)


def mla_attention(q, kv, k_rope, n_heads, blk=256):
    b, s, _ = q.shape
    blk = min(blk, s)
    scale = (MLA_NOPE_DIM + MLA_ROPE_DIM) ** -0.5
    return pl.pallas_call(
        functools.partial(_mla_attn_kernel, blk=blk, scale=scale),
        grid=(b, n_heads, s // blk),
        in_specs=[
            pl.BlockSpec((1, blk, 2 * LANES), lambda bi, h, i: (bi, i, h)),
            pl.BlockSpec((1, s, LANES), lambda bi, h, i: (bi, 0, 2 * h)),
            pl.BlockSpec((1, s, LANES), lambda bi, h, i: (bi, 0, 0)),
            pl.BlockSpec((1, s, LANES), lambda bi, h, i: (bi, 0, 2 * h + 1)),
        ],
        out_specs=pl.BlockSpec((1, blk, LANES), lambda bi, h, i: (bi, i, h)),
        out_shape=jax.ShapeDtypeStruct((b, s, n_heads * MLA_V_DIM), BF16),
        scratch_shapes=[pltpu.VMEM((s, 2 * LANES), BF16)],
        compiler_params=_params("parallel", "parallel", "arbitrary"),
        name="mla_attention",
    )(q, kv, k_rope, kv)


def _extract_top(x, out_ref, k):
    n_rows = x.shape[0]
    rows = lax.broadcasted_iota(jnp.int32, x.shape, 0).astype(F32)

    def body(r, x):
        m = jnp.max(x, axis=0, keepdims=True)
        out_ref[pl.ds(r, 1), :] = m
        first = jnp.min(jnp.where(x == m, rows, float(n_rows)), axis=0, keepdims=True)
        return jnp.where(rows == first, -jnp.inf, x)

    lax.fori_loop(0, k, body, x)


def _peer_select_kernel(qh_ref, keys_ref, st_ref, thr_ref, c_ref, v1_ref, v2_ref, top_ref, *, n_keys, key_half, topk):
    for half, v_ref in ((0, v1_ref), (1, v2_ref)):
        qs = qh_ref[:, half * key_half:(half + 1) * key_half]
        s_t = lax.dot_general(keys_ref[half], qs, _NT, preferred_element_type=F32)
        st_ref[0, half * n_keys:(half + 1) * n_keys, :] = s_t
        _extract_top(s_t, v_ref, topk)
    v1, v2 = v1_ref[...], v2_ref[...]
    assert topk == 16
    slabs = [v1 + v2[0:1]]
    slabs += [v1[0:8] + v2[b:b + 1] for b in range(1, 8)]
    slabs += [v1[0:1] + v2[8:16]]
    _extract_top(jnp.concatenate(slabs, axis=0), top_ref, topk)
    top = top_ref[...]
    mx = top[0:1]
    thr_ref[0] = top[topk - 1:topk]
    c_ref[0] = mx + jnp.log(jnp.sum(jnp.exp(top - mx), axis=0, keepdims=True))


def peer_select(qh, sub_keys, tt=512):
    t = qh.shape[0]
    _, n_keys, key_half = sub_keys.shape
    n_heads = qh.shape[1] // (2 * key_half)
    tt = _tile(t, tt)
    kern = functools.partial(_peer_select_kernel, n_keys=n_keys, key_half=key_half, topk=PEER_TOPK)
    return pl.pallas_call(
        kern,
        grid=(t // tt, n_heads),
        in_specs=[
            pl.BlockSpec((tt, 2 * key_half), lambda i, h: (i, h)),
            pl.BlockSpec(sub_keys.shape, lambda i, h: (0, 0, 0)),
        ],
        out_specs=[
            pl.BlockSpec((1, 2 * n_keys, tt), lambda i, h: (h, 0, i)),
            pl.BlockSpec((1, 1, tt), lambda i, h: (h, 0, i)),
            pl.BlockSpec((1, 1, tt), lambda i, h: (h, 0, i)),
        ],
        out_shape=[
            jax.ShapeDtypeStruct((n_heads, 2 * n_keys, t), F32),
            jax.ShapeDtypeStruct((n_heads, 1, t), F32),
            jax.ShapeDtypeStruct((n_heads, 1, t), F32),
        ],
        scratch_shapes=[pltpu.VMEM((PEER_TOPK, tt), F32)] * 3,
        compiler_params=_params("parallel", "arbitrary"),
        name="peer_select",
    )(qh, sub_keys)


def _gelu_tanh(x):
    return 0.5 * x * (1.0 + jnp.tanh(math.sqrt(2.0 / math.pi) * (x + 0.044715 * (x * x * x))))


def _peer_dense_kernel(h_ref, g_ref, u_ref, v_ref, st_ref, thr_ref, c_ref, o_ref, xn_ref, *, n_heads, n_keys, te):
    e = pl.program_id(1)

    @pl.when(e == 0)
    def _():
        x = h_ref[...]
        xn_ref[...] = _rms_scale(x, g_ref[...]).astype(BF16)
        o_ref[...] = x

    hidden = lax.dot_general(xn_ref[...], u_ref[...], _NT, preferred_element_type=F32)
    rows_per_step = te // n_keys
    blocks = []
    for ii in range(rows_per_step):
        i_row = e * rows_per_step + ii
        gate_t = None
        for h in range(n_heads):
            score = st_ref[h, n_keys:2 * n_keys, :] + st_ref[h, pl.ds(i_row, 1), :]
            g = jnp.where(score >= thr_ref[h], jnp.exp(score - c_ref[h]), 0.0)
            gate_t = g if gate_t is None else gate_t + g
        blocks.append(gate_t)
    gate = jnp.concatenate(blocks, axis=0).T
    gh = (gate * _gelu_tanh(hidden)).astype(BF16)
    o_ref[...] += jnp.dot(gh, v_ref[...], preferred_element_type=F32)


def peer_dense(h, g, u, v, scores_t, thr, c, tt=512, te=512):
    t, d = h.shape
    n_exp = u.shape[0]
    n_heads, two_keys, _ = scores_t.shape
    n_keys = two_keys // 2
    tt, te = _tile(t, tt), _tile(n_exp, te)
    assert te % n_keys == 0 and n_exp == n_keys * n_keys
    kern = functools.partial(_peer_dense_kernel, n_heads=n_heads, n_keys=n_keys, te=te)
    return pl.pallas_call(
        kern,
        grid=(t // tt, n_exp // te),
        in_specs=[
            pl.BlockSpec((tt, d), lambda i, e: (i, 0)),
            pl.BlockSpec((1, d), lambda i, e: (0, 0)),
            pl.BlockSpec((te, d), lambda i, e: (e, 0)),
            pl.BlockSpec((te, d), lambda i, e: (e, 0)),
            pl.BlockSpec((n_heads, two_keys, tt), lambda i, e: (0, 0, i)),
            pl.BlockSpec((n_heads, 1, tt), lambda i, e: (0, 0, i)),
            pl.BlockSpec((n_heads, 1, tt), lambda i, e: (0, 0, i)),
        ],
        out_specs=pl.BlockSpec((tt, d), lambda i, e: (i, 0)),
        out_shape=jax.ShapeDtypeStruct((t, d), F32),
        scratch_shapes=[pltpu.VMEM((tt, d), BF16)],
        compiler_params=_params("parallel", "arbitrary"),
        name="peer_dense",
    )(h, g.reshape(1, d), u, v, scores_t, thr, c)


def _ple_kernel(x_ref, g_ref, wg_ref, p_ref, wp_ref, r_ref, o_ref, xn_ref, pb_ref):
    @pl.when(pl.program_id(1) == 0)
    def _():
        xn_ref[...] = _rms_scale(x_ref[...], g_ref[...]).astype(BF16)
        pb_ref[...] = p_ref[...].astype(BF16)

    gate = 1.0 / (1.0 + jnp.exp(-jnp.dot(xn_ref[...], wg_ref[...], preferred_element_type=F32)))
    o_ref[...] = r_ref[...] + jnp.dot(pb_ref[...], wp_ref[...], preferred_element_type=F32) * gate


def per_layer_embedding(h, p, g, w_g, w_p, tm=512, tn=1024):
    t, d = h.shape
    pd = p.shape[1]
    tm, tn = _tile(t, tm), _tile(d, tn)
    return pl.pallas_call(
        _ple_kernel,
        grid=(t // tm, d // tn),
        in_specs=[
            pl.BlockSpec((tm, d), lambda i, j: (i, 0)),
            pl.BlockSpec((1, d), lambda i, j: (0, 0)),
            pl.BlockSpec((d, tn), lambda i, j: (0, j)),
            pl.BlockSpec((tm, pd), lambda i, j: (i, 0)),
            pl.BlockSpec((pd, tn), lambda i, j: (0, j)),
            pl.BlockSpec((tm, tn), lambda i, j: (i, j)),
        ],
        out_specs=pl.BlockSpec((tm, tn), lambda i, j: (i, j)),
        out_shape=jax.ShapeDtypeStruct((t, d), F32),
        scratch_shapes=[pltpu.VMEM((tm, d), BF16), pltpu.VMEM((tm, pd), BF16)],
        compiler_params=_params("parallel", "arbitrary"),
        name="per_layer_embedding",
    )(h, g.reshape(1, d), w_g, p, w_p, h)


def _pad_cols(w, multiple):
    extra = (-w.shape[-1]) % multiple
    return jnp.pad(w, ((0, 0), (0, extra))) if extra else w


def kernel(x, p, attn_norm, ffn_norm, ple_norm, final_norm, sb_w_qkv, sb_w_o, swa_w_qkv, swa_w_o, swa_sinks,
           mla_w_down, mla_q_norm, mla_kv_norm, mla_w_uq, mla_w_ukv, mla_w_o,
           peer_w_q, peer_sub_keys, peer_u, peer_v, ple_w_p, ple_w_g):
    b, s, d = x.shape
    t = b * s
    depth = p.shape[0]
    h = x.reshape(t, d)
    for i in range(depth):
        kind, j = i % N_MIXERS, i // N_MIXERS
        if kind == 0:
            qkv = norm_matmul(h, attn_norm[i], sb_w_qkv[j].astype(BF16), BF16)
            o = sb_attention(qkv.reshape(b, s, -1), SB_HEADS)
            w_o = sb_w_o[j]
        elif kind == 1:
            qkv = norm_matmul(h, attn_norm[i], swa_w_qkv[j].astype(BF16), BF16, tn=1280)
            head_dim = swa_w_o.shape[1] // SWA_HEADS
            o = swa_attention(qkv.reshape(b, s, -1), swa_sinks[j], SWA_HEADS, SWA_KV_HEADS, head_dim)
            w_o = swa_w_o[j]
        else:
            down = norm_matmul(h, attn_norm[i], _pad_cols(mla_w_down[j], LANES).astype(BF16), F32, tn=2048)
            qk_dim = MLA_NOPE_DIM + MLA_ROPE_DIM
            w_uq_pad = jnp.pad(mla_w_uq[j].reshape(MLA_Q_RANK, MLA_HEADS, qk_dim),
                               ((0, 0), (0, 0), (0, 2 * LANES - qk_dim))).reshape(MLA_Q_RANK, -1)
            q, kv, k_rope = mla_projections(down, mla_q_norm[j], mla_kv_norm[j], w_uq_pad.astype(BF16),
                                            mla_w_ukv[j].astype(BF16), s)
            o = mla_attention(q.reshape(b, s, -1), kv.reshape(b, s, -1), k_rope.reshape(b, s, -1), MLA_HEADS)
            w_o = mla_w_o[j]
        h = matmul_residual(o.reshape(t, -1), w_o.astype(BF16), h)
        qh = norm_matmul(h, ffn_norm[i], peer_w_q[i].astype(BF16), BF16)
        scores_t, thr, c = peer_select(qh, peer_sub_keys[i].astype(BF16))
        h = peer_dense(h, ffn_norm[i], peer_u[i].astype(BF16), peer_v[i].astype(BF16), scores_t, thr, c)
        h = per_layer_embedding(h, p[i].reshape(t, -1), ple_norm[i], ple_w_g[i].astype(BF16), ple_w_p[i].astype(BF16))
    return final_rms_norm(h, final_norm).reshape(b, s, d)
```

```python
import functools
import math

import jax
import jax.numpy as jnp
from jax import lax
from jax.experimental import pallas as pl
from jax.experimental.pallas import tpu as pltpu

F32 = jnp.float32
BF16 = jnp.bfloat16

RMS_EPS = 1e-6
N_MIXERS = 3
SB_HEADS = 16
SWA_HEADS = 32
SWA_KV_HEADS = 4
SWA_WINDOW = 128
MLA_HEADS = 16
MLA_Q_RANK = 512
MLA_KV_RANK = 512
MLA_NOPE_DIM = 128
MLA_ROPE_DIM = 64
MLA_V_DIM = 128
ROPE_THETA = 10000.0
PEER_N_KEYS = 64
PEER_TOPK = 16

LANES = 128
V7X_VMEM_BYTES = 64 * 1024 * 1024
VMEM_LIMIT = V7X_VMEM_BYTES * 3 // 4

_NT = (((1,), (1,)), ((), ()))


def _params(*semantics):
    return pltpu.CompilerParams(dimension_semantics=semantics, vmem_limit_bytes=VMEM_LIMIT)


def _tile(n, want):
    if n <= want:
        return n
    t = (want // LANES) * LANES
    while t > LANES and n % t:
        t -= LANES
    assert n % t == 0, (n, want)
    return t


def _rms_scale(x, g):
    ms = jnp.mean(x * x, axis=-1, keepdims=True)
    return x * lax.rsqrt(ms + RMS_EPS) * g


def _norm_matmul_kernel(x_ref, g_ref, w_ref, o_ref, xn_ref):
    @pl.when(pl.program_id(1) == 0)
    def _():
        xn_ref[...] = _rms_scale(x_ref[...], g_ref[...]).astype(BF16)

    o_ref[...] = jnp.dot(xn_ref[...], w_ref[...], preferred_element_type=F32).astype(o_ref.dtype)


def norm_matmul(x, g, w, out_dtype, tm=512, tn=1024):
    m, k = x.shape
    n = w.shape[1]
    tm, tn = _tile(m, tm), _tile(n, tn)
    return pl.pallas_call(
        _norm_matmul_kernel,
        grid=(m // tm, n // tn),
        in_specs=[
            pl.BlockSpec((tm, k), lambda i, j: (i, 0)),
            pl.BlockSpec((1, k), lambda i, j: (0, 0)),
            pl.BlockSpec((k, tn), lambda i, j: (0, j)),
        ],
        out_specs=pl.BlockSpec((tm, tn), lambda i, j: (i, j)),
        out_shape=jax.ShapeDtypeStruct((m, n), out_dtype),
        scratch_shapes=[pltpu.VMEM((tm, k), BF16)],
        compiler_params=_params("parallel", "arbitrary"),
        name="norm_matmul",
    )(x, g.reshape(1, k), w)


def _matmul_residual_kernel(a_ref, w_ref, r_ref, o_ref):
    o_ref[...] = r_ref[...] + jnp.dot(a_ref[...], w_ref[...], preferred_element_type=F32)


def matmul_residual(a, w, res, tm=512, tn=1024):
    m, k = a.shape
    n = w.shape[1]
    tm, tn = _tile(m, tm), _tile(n, tn)
    return pl.pallas_call(
        _matmul_residual_kernel,
        grid=(m // tm, n // tn),
        in_specs=[
            pl.BlockSpec((tm, k), lambda i, j: (i, 0)),
            pl.BlockSpec((k, tn), lambda i, j: (0, j)),
            pl.BlockSpec((tm, tn), lambda i, j: (i, j)),
        ],
        out_specs=pl.BlockSpec((tm, tn), lambda i, j: (i, j)),
        out_shape=jax.ShapeDtypeStruct((m, n), F32),
        compiler_params=_params("parallel", "arbitrary"),
        name="matmul_residual",
    )(a, w, res)


def _final_norm_kernel(x_ref, g_ref, o_ref):
    o_ref[...] = _rms_scale(x_ref[...], g_ref[...])


def final_rms_norm(x, g, tm=512):
    m, k = x.shape
    tm = _tile(m, tm)
    return pl.pallas_call(
        _final_norm_kernel,
        grid=(m // tm,),
        in_specs=[pl.BlockSpec((tm, k), lambda i: (i, 0)), pl.BlockSpec((1, k), lambda i: (0, 0))],
        out_specs=pl.BlockSpec((tm, k), lambda i: (i, 0)),
        out_shape=jax.ShapeDtypeStruct((m, k), F32),
        compiler_params=_params("parallel"),
        name="final_norm",
    )(x, g.reshape(1, k))


def _sb_attn_kernel(q_ref, k_ref, v_ref, o_ref, *, bq, bk, d, heads, scale):
    i = pl.program_id(2)
    n_diag = bq // bk
    row = lax.broadcasted_iota(jnp.int32, (bk, bk), 0)
    col = lax.broadcasted_iota(jnp.int32, (bk, bk), 1)
    suffix_ones = (row >= col).astype(BF16)
    q_idx = lax.broadcasted_iota(jnp.int32, (bq, bk), 0)
    k_idx = lax.broadcasted_iota(jnp.int32, (bq, bk), 1)

    def head_step(hh, m, acc, log_run, key_offset):
        lanes = slice(hh * d, (hh + 1) * d)
        start = pl.multiple_of(m * bk, bk)
        kb = k_ref[0, pl.ds(start, bk), lanes]
        vb = v_ref[0, pl.ds(start, bk), lanes]
        z = lax.dot_general(q_ref[0, :, lanes], kb, _NT, preferred_element_type=F32) * scale
        log_keep = -(jnp.maximum(z, 0.0) + jnp.log(1.0 + jnp.exp(-jnp.abs(z))))
        if key_offset is not None:
            strict = k_idx + key_offset < q_idx
            log_keep = jnp.where(strict, log_keep, 0.0)
        incl = jnp.dot(log_keep.astype(BF16), suffix_ones, preferred_element_type=F32)
        logw = z + incl + log_run
        if key_offset is not None:
            logw = jnp.where(strict, logw, -jnp.inf)
        p = jnp.exp(logw)
        acc = acc + jnp.dot(p.astype(BF16), vb, preferred_element_type=F32)
        return acc, log_run + incl[:, 0:1]

    def step(m, carry, key_offset):
        return tuple(head_step(hh, m, *carry[hh], key_offset) for hh in range(heads))

    carry = tuple((jnp.zeros((bq, d), F32), jnp.zeros((bq, 1), F32)) for _ in range(heads))
    for dd in reversed(range(n_diag)):
        carry = step(i * n_diag + dd, carry, dd * bk)
    n_before = i * n_diag
    carry = lax.fori_loop(0, n_before, lambda t, c: step(n_before - 1 - t, c, None), carry)
    for hh in range(heads):
        o_ref[0, :, hh * d:(hh + 1) * d] = carry[hh][0].astype(o_ref.dtype)


def sb_attention(qkv, n_heads, bq=1024, bk=256, heads_per_step=2):
    b, s, three_hd = qkv.shape
    d = three_hd // (3 * n_heads)
    assert d == LANES
    bq = min(bq, s)
    bk = min(bk, bq)
    hp = heads_per_step
    assert s % bq == 0 and bq % bk == 0 and n_heads % hp == 0
    n_grp = n_heads // hp
    kern = functools.partial(_sb_attn_kernel, bq=bq, bk=bk, d=d, heads=hp, scale=d ** -0.5)
    return pl.pallas_call(
        kern,
        grid=(b, n_grp, s // bq),
        in_specs=[
            pl.BlockSpec((1, bq, hp * d), lambda bi, h, i: (bi, i, h)),
            pl.BlockSpec((1, s, hp * d), lambda bi, h, i: (bi, 0, n_grp + h)),
            pl.BlockSpec((1, s, hp * d), lambda bi, h, i: (bi, 0, 2 * n_grp + h)),
        ],
        out_specs=pl.BlockSpec((1, bq, hp * d), lambda bi, h, i: (bi, i, h)),
        out_shape=jax.ShapeDtypeStruct((b, s, n_heads * d), BF16),
        compiler_params=_params("parallel", "parallel", "arbitrary"),
        name="sb_attention",
    )(qkv, qkv, qkv)


def _swa_kernel(q_ref, kp_ref, kc_ref, vp_ref, vc_ref, slope_ref, sink_ref, o_ref, *, n_kv, grp, d, blk, scale):
    i = pl.program_id(1)
    rows = grp * blk
    qpos = lax.broadcasted_iota(jnp.int32, (rows, 2 * blk), 0) & (blk - 1)
    kcol = lax.broadcasted_iota(jnp.int32, (rows, 2 * blk), 1)
    dist = qpos + blk - kcol
    first_key = jnp.where(i > 0, 0, blk)
    in_window = (dist >= 0) & (dist < SWA_WINDOW) & (kcol >= first_key)
    dist_f = dist.astype(F32)
    outs = []
    for kh in range(n_kv):
        qg = jnp.concatenate(
            [q_ref[0, :, (kh * grp + g) * d:(kh * grp + g + 1) * d] for g in range(grp)], axis=0)
        kk = jnp.concatenate([kp_ref[0, :, kh * d:(kh + 1) * d], kc_ref[0, :, kh * d:(kh + 1) * d]], axis=0)
        vv = jnp.concatenate([vp_ref[0, :, kh * d:(kh + 1) * d], vc_ref[0, :, kh * d:(kh + 1) * d]], axis=0)
        z = lax.dot_general(qg, kk, _NT, preferred_element_type=F32) * scale
        z = z - slope_ref[kh] * dist_f
        z = jnp.where(in_window, z, -jnp.inf)
        sink = sink_ref[kh]
        mx = jnp.maximum(jnp.max(z, axis=-1, keepdims=True), sink)
        p = jnp.exp(z - mx)
        den = jnp.sum(p, axis=-1, keepdims=True) + jnp.exp(sink - mx)
        o = jnp.dot(p.astype(BF16), vv, preferred_element_type=F32) / den
        outs.extend(o[g * blk:(g + 1) * blk] for g in range(grp))
    o_ref[0] = jnp.concatenate(outs, axis=1).astype(o_ref.dtype)


def swa_attention(qkv, sinks, n_heads, n_kv, d):
    b, s, _ = qkv.shape
    blk = SWA_WINDOW
    grp = n_heads // n_kv
    qd, kd = n_heads * d, n_kv * d
    assert qd % kd == 0 and kd % LANES == 0 and s % blk == 0
    slopes = 2.0 ** (-8.0 * jnp.arange(1, n_heads + 1, dtype=F32) / n_heads)
    per_row = lambda a: jnp.repeat(a.astype(F32).reshape(n_kv, grp), blk, axis=1).reshape(n_kv, grp * blk, 1)
    kern = functools.partial(_swa_kernel, n_kv=n_kv, grp=grp, d=d, blk=blk, scale=d ** -0.5)
    prev = lambda bi, i: (bi, jnp.maximum(i - 1, 0), qd // kd)
    cur = lambda bi, i: (bi, i, qd // kd)
    prev_v = lambda bi, i: (bi, jnp.maximum(i - 1, 0), qd // kd + 1)
    cur_v = lambda bi, i: (bi, i, qd // kd + 1)
    const = lambda bi, i: (0, 0, 0)
    return pl.pallas_call(
        kern,
        grid=(b, s // blk),
        in_specs=[
            pl.BlockSpec((1, blk, qd), lambda bi, i: (bi, i, 0)),
            pl.BlockSpec((1, blk, kd), prev),
            pl.BlockSpec((1, blk, kd), cur),
            pl.BlockSpec((1, blk, kd), prev_v),
            pl.BlockSpec((1, blk, kd), cur_v),
            pl.BlockSpec((n_kv, grp * blk, 1), const),
            pl.BlockSpec((n_kv, grp * blk, 1), const),
        ],
        out_specs=pl.BlockSpec((1, blk, qd), lambda bi, i: (bi, i, 0)),
        out_shape=jax.ShapeDtypeStruct((b, s, qd), BF16),
        compiler_params=_params("parallel", "arbitrary"),
        name="swa_attention",
    )(qkv, qkv, qkv, qkv, qkv, per_row(slopes), per_row(sinks))


def _rope_lanes(r, cos_ref, sin_lo_ref, sin_hi_ref):
    half = MLA_ROPE_DIM // 2
    return (r * cos_ref[...]
            + pltpu.roll(r, LANES - half, 1) * sin_lo_ref[...]
            + pltpu.roll(r, half, 1) * sin_hi_ref[...])


def _mla_q_kernel(x_ref, g_ref, w_ref, cos_ref, slo_ref, shi_ref, o_ref, *, n_heads):
    xn = _rms_scale(x_ref[...], g_ref[...]).astype(BF16)
    for h in range(n_heads):
        a = jnp.dot(xn, w_ref[:, 2 * LANES * h:2 * LANES * (h + 1)], preferred_element_type=F32)
        o_ref[:, 2 * LANES * h:2 * LANES * h + LANES] = a[:, :LANES].astype(BF16)
        o_ref[:, 2 * LANES * h + LANES:2 * LANES * (h + 1)] = _rope_lanes(
            a[:, LANES:], cos_ref, slo_ref, shi_ref).astype(BF16)


def _mla_kv_kernel(x_ref, g_ref, xr_ref, w_ref, cos_ref, slo_ref, shi_ref, kv_ref, kr_ref, *, chunk):
    xn = _rms_scale(x_ref[...], g_ref[...]).astype(BF16)
    for c in range(w_ref.shape[1] // chunk):
        kv_ref[:, c * chunk:(c + 1) * chunk] = jnp.dot(
            xn, w_ref[:, c * chunk:(c + 1) * chunk], preferred_element_type=F32).astype(BF16)
    kr_ref[...] = _rope_lanes(xr_ref[...], cos_ref, slo_ref, shi_ref).astype(BF16)


def _rope_tables(s):
    half = MLA_ROPE_DIM // 2
    inv_freq = ROPE_THETA ** (-jnp.arange(half, dtype=F32) / half)
    ang = jnp.arange(s, dtype=F32)[:, None] * inv_freq[None, :]
    cos, sin = jnp.cos(ang), jnp.sin(ang)
    zeros = jnp.zeros((s, LANES - 2 * half), F32)
    z_half = jnp.zeros((s, half), F32)
    cos_t = jnp.concatenate([cos, cos, zeros], axis=1)
    sin_lo = jnp.concatenate([-sin, z_half, zeros], axis=1)
    sin_hi = jnp.concatenate([z_half, sin, zeros], axis=1)
    return cos_t, sin_lo, sin_hi


def mla_projections(down, g_q, g_kv, w_uq_pad, w_ukv, seq, tm=512):
    t = down.shape[0]
    tm = _tile(seq, tm)
    n_heads = w_uq_pad.shape[1] // (2 * LANES)
    tables = _rope_tables(seq)
    n_pos = seq // tm
    tab_spec = pl.BlockSpec((tm, LANES), lambda i: (i % n_pos, 0))
    q = pl.pallas_call(
        functools.partial(_mla_q_kernel, n_heads=n_heads),
        grid=(t // tm,),
        in_specs=[
            pl.BlockSpec((tm, MLA_Q_RANK), lambda i: (i, 0)),
            pl.BlockSpec((1, MLA_Q_RANK), lambda i: (0, 0)),
            pl.BlockSpec(w_uq_pad.shape, lambda i: (0, 0)),
            tab_spec, tab_spec, tab_spec,
        ],
        out_specs=pl.BlockSpec((tm, w_uq_pad.shape[1]), lambda i: (i, 0)),
        out_shape=jax.ShapeDtypeStruct((t, w_uq_pad.shape[1]), BF16),
        compiler_params=_params("parallel"),
        name="mla_q_proj",
    )(down, g_q.reshape(1, -1), w_uq_pad, *tables)
    assert MLA_Q_RANK == MLA_KV_RANK and (MLA_Q_RANK + MLA_KV_RANK) % LANES == 0
    kv, k_rope = pl.pallas_call(
        functools.partial(_mla_kv_kernel, chunk=_tile(w_ukv.shape[1], 1024)),
        grid=(t // tm,),
        in_specs=[
            pl.BlockSpec((tm, MLA_KV_RANK), lambda i: (i, 1)),
            pl.BlockSpec((1, MLA_KV_RANK), lambda i: (0, 0)),
            pl.BlockSpec((tm, LANES), lambda i: (i, (MLA_Q_RANK + MLA_KV_RANK) // LANES)),
            pl.BlockSpec(w_ukv.shape, lambda i: (0, 0)),
            tab_spec, tab_spec, tab_spec,
        ],
        out_specs=[pl.BlockSpec((tm, w_ukv.shape[1]), lambda i: (i, 0)),
                   pl.BlockSpec((tm, LANES), lambda i: (i, 0))],
        out_shape=[jax.ShapeDtypeStruct((t, w_ukv.shape[1]), BF16), jax.ShapeDtypeStruct((t, LANES), BF16)],
        compiler_params=_params("parallel"),
        name="mla_kv_proj",
    )(down, g_kv.reshape(1, -1), down, w_ukv, *tables)
    return q, kv, k_rope


def _mla_attn_kernel(q_ref, kv_ref, kr_ref, o_ref, k_ref, *, bq, bk, chunk, heads, scale):
    i = pl.program_id(2)
    width = 2 * LANES

    @pl.when(i == 0)
    def _():
        for hh in range(heads):
            k_ref[hh, :, :LANES] = kv_ref[0, :, hh * width:hh * width + LANES]
            k_ref[hh, :, LANES:] = kr_ref[0]

    n_diag = bq // bk
    n_chunks = bq // chunk
    q_idx = lax.broadcasted_iota(jnp.int32, (chunk, bk), 0)
    k_idx = lax.broadcasted_iota(jnp.int32, (chunk, bk), 1)

    def head_step(hh, m, acc, mx, den, key_offset):
        start = pl.multiple_of(m * bk, bk)
        kb = k_ref[hh, pl.ds(start, bk), :]
        vb = kv_ref[0, pl.ds(start, bk), hh * width + LANES:(hh + 1) * width]
        q = q_ref[0, :, hh * width:(hh + 1) * width]
        z_all = lax.dot_general(q, kb, _NT, preferred_element_type=F32)
        p_parts, mx_new, den_new, corr_parts = [], [], [], []
        for c in range(n_chunks):
            first_row = c * chunk
            if key_offset is not None and first_row + chunk - 1 < key_offset:
                p_parts.append(jnp.zeros((chunk, bk), BF16))
                mx_new.append(mx[c])
                den_new.append(den[c])
                corr_parts.append(None)
                continue
            z = z_all[first_row:first_row + chunk] * scale
            if key_offset is not None and first_row < key_offset + bk - 1:
                z = jnp.where(k_idx + (key_offset - first_row) <= q_idx, z, -jnp.inf)
            new_mx = jnp.maximum(mx[c], jnp.max(z, axis=-1, keepdims=True))
            corr = jnp.exp(mx[c] - new_mx)
            pz = jnp.exp(z - new_mx)
            den_new.append(den[c] * corr + jnp.sum(pz, axis=-1, keepdims=True))
            mx_new.append(new_mx)
            corr_parts.append(corr)
            p_parts.append(pz.astype(BF16))
        pv = jnp.dot(jnp.concatenate(p_parts, axis=0), vb, preferred_element_type=F32)
        acc_parts = []
        for c in range(n_chunks):
            rows = slice(c * chunk, (c + 1) * chunk)
            acc_parts.append(acc[rows] + pv[rows] if corr_parts[c] is None else acc[rows] * corr_parts[c] + pv[rows])
        return jnp.concatenate(acc_parts, axis=0), tuple(mx_new), tuple(den_new)

    def step(m, carry, key_offset):
        return tuple(head_step(hh, m, *carry[hh], key_offset) for hh in range(heads))

    init = tuple((jnp.zeros((bq, LANES), F32),
                  tuple(jnp.full((chunk, 1), -jnp.inf, F32) for _ in range(n_chunks)),
                  tuple(jnp.zeros((chunk, 1), F32) for _ in range(n_chunks)))
                 for _ in range(heads))
    carry = lax.fori_loop(0, i * n_diag, lambda m, c: step(m, c, None), init)
    for dd in range(n_diag):
        carry = step(i * n_diag + dd, carry, dd * bk)
    for hh in range(heads):
        acc, _, den = carry[hh]
        for c in range(n_chunks):
            o_ref[0, c * chunk:(c + 1) * chunk, hh * LANES:(hh + 1) * LANES] = (
                acc[c * chunk:(c + 1) * chunk] / den[c]).astype(o_ref.dtype)


def mla_attention(q, kv, k_rope, n_heads, bq=1024, bk=256, chunk=64, heads_per_step=2):
    b, s, _ = q.shape
    bq = min(bq, s)
    bk = min(bk, bq)
    hp = heads_per_step
    assert s % bq == 0 and bq % bk == 0 and n_heads % hp == 0 and MLA_V_DIM == LANES
    scale = (MLA_NOPE_DIM + MLA_ROPE_DIM) ** -0.5
    return pl.pallas_call(
        functools.partial(_mla_attn_kernel, bq=bq, bk=bk, chunk=min(chunk, bq), heads=hp, scale=scale),
        grid=(b, n_heads // hp, s // bq),
        in_specs=[
            pl.BlockSpec((1, bq, hp * 2 * LANES), lambda bi, h, i: (bi, i, h)),
            pl.BlockSpec((1, s, hp * 2 * LANES), lambda bi, h, i: (bi, 0, h)),
            pl.BlockSpec((1, s, LANES), lambda bi, h, i: (bi, 0, 0)),
        ],
        out_specs=pl.BlockSpec((1, bq, hp * LANES), lambda bi, h, i: (bi, i, h)),
        out_shape=jax.ShapeDtypeStruct((b, s, n_heads * MLA_V_DIM), BF16),
        scratch_shapes=[pltpu.VMEM((hp, s, 2 * LANES), BF16)],
        compiler_params=_params("parallel", "parallel", "arbitrary"),
        name="mla_attention",
    )(q, kv, k_rope)


def _extract_top(x, out_ref, k):
    n_rows = x.shape[0]
    rows = lax.broadcasted_iota(jnp.int32, x.shape, 0).astype(F32)

    def body(r, x):
        m = jnp.max(x, axis=0, keepdims=True)
        out_ref[pl.ds(r, 1), :] = m
        first = jnp.min(jnp.where(x == m, rows, float(n_rows)), axis=0, keepdims=True)
        return jnp.where(rows == first, -jnp.inf, x)

    lax.fori_loop(0, k, body, x)


def _peer_select_kernel(qh_ref, keys_ref, st_ref, thr_ref, c_ref, v1_ref, v2_ref, top_ref, *, n_keys, key_half, topk):
    for half, v_ref in ((0, v1_ref), (1, v2_ref)):
        qs = qh_ref[:, half * key_half:(half + 1) * key_half]
        s_t = lax.dot_general(keys_ref[half], qs, _NT, preferred_element_type=F32)
        st_ref[0, half * n_keys:(half + 1) * n_keys, :] = s_t
        _extract_top(s_t, v_ref, topk)
    v1, v2 = v1_ref[...], v2_ref[...]
    assert topk == 16
    slabs = [v1 + v2[0:1]]
    slabs += [v1[0:8] + v2[b:b + 1] for b in range(1, 8)]
    slabs += [v1[0:1] + v2[8:16]]
    _extract_top(jnp.concatenate(slabs, axis=0), top_ref, topk)
    top = top_ref[...]
    mx = top[0:1]
    thr_ref[0] = top[topk - 1:topk]
    c_ref[0] = mx + jnp.log(jnp.sum(jnp.exp(top - mx), axis=0, keepdims=True))


def peer_select(qh, sub_keys, tt=512):
    t = qh.shape[0]
    _, n_keys, key_half = sub_keys.shape
    n_heads = qh.shape[1] // (2 * key_half)
    tt = _tile(t, tt)
    kern = functools.partial(_peer_select_kernel, n_keys=n_keys, key_half=key_half, topk=PEER_TOPK)
    return pl.pallas_call(
        kern,
        grid=(t // tt, n_heads),
        in_specs=[
            pl.BlockSpec((tt, 2 * key_half), lambda i, h: (i, h)),
            pl.BlockSpec(sub_keys.shape, lambda i, h: (0, 0, 0)),
        ],
        out_specs=[
            pl.BlockSpec((1, 2 * n_keys, tt), lambda i, h: (h, 0, i)),
            pl.BlockSpec((1, 1, tt), lambda i, h: (h, 0, i)),
            pl.BlockSpec((1, 1, tt), lambda i, h: (h, 0, i)),
        ],
        out_shape=[
            jax.ShapeDtypeStruct((n_heads, 2 * n_keys, t), F32),
            jax.ShapeDtypeStruct((n_heads, 1, t), F32),
            jax.ShapeDtypeStruct((n_heads, 1, t), F32),
        ],
        scratch_shapes=[pltpu.VMEM((PEER_TOPK, tt), F32)] * 3,
        compiler_params=_params("parallel", "arbitrary"),
        name="peer_select",
    )(qh, sub_keys)


def _gelu_tanh(x):
    return 0.5 * x * (1.0 + jnp.tanh(math.sqrt(2.0 / math.pi) * (x + 0.044715 * (x * x * x))))


def _peer_dense_kernel(h_ref, g_ref, u_ref, v_ref, st_ref, thr_ref, c_ref, o_ref, xn_ref, *, n_heads, n_keys, te):
    e = pl.program_id(1)

    @pl.when(e == 0)
    def _():
        x = h_ref[...]
        xn_ref[...] = _rms_scale(x, g_ref[...]).astype(BF16)
        o_ref[...] = x

    hidden = lax.dot_general(xn_ref[...], u_ref[...], _NT, preferred_element_type=F32)
    rows_per_step = te // n_keys
    blocks = []
    for ii in range(rows_per_step):
        i_row = e * rows_per_step + ii
        gate_t = None
        for h in range(n_heads):
            score = st_ref[h, n_keys:2 * n_keys, :] + st_ref[h, pl.ds(i_row, 1), :]
            g = jnp.where(score >= thr_ref[h], jnp.exp(score - c_ref[h]), 0.0)
            gate_t = g if gate_t is None else gate_t + g
        blocks.append(gate_t)
    gate = jnp.concatenate(blocks, axis=0).T
    gh = (gate * _gelu_tanh(hidden)).astype(BF16)
    o_ref[...] += jnp.dot(gh, v_ref[...], preferred_element_type=F32)


def peer_dense(h, g, u, v, scores_t, thr, c, tt=512, te=512):
    t, d = h.shape
    n_exp = u.shape[0]
    n_heads, two_keys, _ = scores_t.shape
    n_keys = two_keys // 2
    tt, te = _tile(t, tt), _tile(n_exp, te)
    assert te % n_keys == 0 and n_exp == n_keys * n_keys
    kern = functools.partial(_peer_dense_kernel, n_heads=n_heads, n_keys=n_keys, te=te)
    return pl.pallas_call(
        kern,
        grid=(t // tt, n_exp // te),
        in_specs=[
            pl.BlockSpec((tt, d), lambda i, e: (i, 0)),
            pl.BlockSpec((1, d), lambda i, e: (0, 0)),
            pl.BlockSpec((te, d), lambda i, e: (e, 0)),
            pl.BlockSpec((te, d), lambda i, e: (e, 0)),
            pl.BlockSpec((n_heads, two_keys, tt), lambda i, e: (0, 0, i)),
            pl.BlockSpec((n_heads, 1, tt), lambda i, e: (0, 0, i)),
            pl.BlockSpec((n_heads, 1, tt), lambda i, e: (0, 0, i)),
        ],
        out_specs=pl.BlockSpec((tt, d), lambda i, e: (i, 0)),
        out_shape=jax.ShapeDtypeStruct((t, d), F32),
        scratch_shapes=[pltpu.VMEM((tt, d), BF16)],
        compiler_params=_params("parallel", "arbitrary"),
        name="peer_dense",
    )(h, g.reshape(1, d), u, v, scores_t, thr, c)


def _ple_kernel(x_ref, g_ref, wg_ref, p_ref, wp_ref, r_ref, o_ref, xn_ref, pb_ref):
    @pl.when(pl.program_id(1) == 0)
    def _():
        xn_ref[...] = _rms_scale(x_ref[...], g_ref[...]).astype(BF16)
        pb_ref[...] = p_ref[...].astype(BF16)

    gate = 1.0 / (1.0 + jnp.exp(-jnp.dot(xn_ref[...], wg_ref[...], preferred_element_type=F32)))
    o_ref[...] = r_ref[...] + jnp.dot(pb_ref[...], wp_ref[...], preferred_element_type=F32) * gate


def per_layer_embedding(h, p, g, w_g, w_p, tm=512, tn=1024):
    t, d = h.shape
    pd = p.shape[1]
    tm, tn = _tile(t, tm), _tile(d, tn)
    return pl.pallas_call(
        _ple_kernel,
        grid=(t // tm, d // tn),
        in_specs=[
            pl.BlockSpec((tm, d), lambda i, j: (i, 0)),
            pl.BlockSpec((1, d), lambda i, j: (0, 0)),
            pl.BlockSpec((d, tn), lambda i, j: (0, j)),
            pl.BlockSpec((tm, pd), lambda i, j: (i, 0)),
            pl.BlockSpec((pd, tn), lambda i, j: (0, j)),
            pl.BlockSpec((tm, tn), lambda i, j: (i, j)),
        ],
        out_specs=pl.BlockSpec((tm, tn), lambda i, j: (i, j)),
        out_shape=jax.ShapeDtypeStruct((t, d), F32),
        scratch_shapes=[pltpu.VMEM((tm, d), BF16), pltpu.VMEM((tm, pd), BF16)],
        compiler_params=_params("parallel", "arbitrary"),
        name="per_layer_embedding",
    )(h, g.reshape(1, d), w_g, p, w_p, h)


def _pad_cols(w, multiple):
    extra = (-w.shape[-1]) % multiple
    return jnp.pad(w, ((0, 0), (0, extra))) if extra else w


def kernel(x, p, attn_norm, ffn_norm, ple_norm, final_norm, sb_w_qkv, sb_w_o, swa_w_qkv, swa_w_o, swa_sinks,
           mla_w_down, mla_q_norm, mla_kv_norm, mla_w_uq, mla_w_ukv, mla_w_o,
           peer_w_q, peer_sub_keys, peer_u, peer_v, ple_w_p, ple_w_g):
    b, s, d = x.shape
    t = b * s
    depth = p.shape[0]
    h = x.reshape(t, d)
    for i in range(depth):
        kind, j = i % N_MIXERS, i // N_MIXERS
        if kind == 0:
            qkv = norm_matmul(h, attn_norm[i], sb_w_qkv[j].astype(BF16), BF16)
            o = sb_attention(qkv.reshape(b, s, -1), SB_HEADS)
            w_o = sb_w_o[j]
        elif kind == 1:
            qkv = norm_matmul(h, attn_norm[i], swa_w_qkv[j].astype(BF16), BF16, tn=1280)
            head_dim = swa_w_o.shape[1] // SWA_HEADS
            o = swa_attention(qkv.reshape(b, s, -1), swa_sinks[j], SWA_HEADS, SWA_KV_HEADS, head_dim)
            w_o = swa_w_o[j]
        else:
            down = norm_matmul(h, attn_norm[i], _pad_cols(mla_w_down[j], LANES).astype(BF16), F32, tn=2048)
            qk_dim = MLA_NOPE_DIM + MLA_ROPE_DIM
            w_uq_pad = jnp.pad(mla_w_uq[j].reshape(MLA_Q_RANK, MLA_HEADS, qk_dim),
                               ((0, 0), (0, 0), (0, 2 * LANES - qk_dim))).reshape(MLA_Q_RANK, -1)
            q, kv, k_rope = mla_projections(down, mla_q_norm[j], mla_kv_norm[j], w_uq_pad.astype(BF16),
                                            mla_w_ukv[j].astype(BF16), s)
            o = mla_attention(q.reshape(b, s, -1), kv.reshape(b, s, -1), k_rope.reshape(b, s, -1), MLA_HEADS)
            w_o = mla_w_o[j]
        h = matmul_residual(o.reshape(t, -1), w_o.astype(BF16), h)
        qh = norm_matmul(h, ffn_norm[i], peer_w_q[i].astype(BF16), BF16)
        scores_t, thr, c = peer_select(qh, peer_sub_keys[i].astype(BF16))
        h = peer_dense(h, ffn_norm[i], peer_u[i].astype(BF16), peer_v[i].astype(BF16), scores_t, thr, c)
        h = per_layer_embedding(h, p[i].reshape(t, -1), ple_norm[i], ple_w_g[i].astype(BF16), ple_w_p[i].astype(BF16))
    return final_rms_norm(h, final_norm).reshape(b, s, d)
```

```python
import functools
import math

import jax
import jax.numpy as jnp
from jax import lax
from jax.experimental import pallas as pl
from jax.experimental.pallas import tpu as pltpu

F32 = jnp.float32
BF16 = jnp.bfloat16

RMS_EPS = 1e-6
N_MIXERS = 3
SB_HEADS = 16
SWA_HEADS = 32
SWA_KV_HEADS = 4
SWA_WINDOW = 128
MLA_HEADS = 16
MLA_Q_RANK = 512
MLA_KV_RANK = 512
MLA_NOPE_DIM = 128
MLA_ROPE_DIM = 64
MLA_V_DIM = 128
ROPE_THETA = 10000.0
PEER_N_KEYS = 64
PEER_TOPK = 16

LANES = 128
V7X_VMEM_BYTES = 64 * 1024 * 1024
VMEM_LIMIT = V7X_VMEM_BYTES * 3 // 4

_NT = (((1,), (1,)), ((), ()))


def _params(*semantics):
    return pltpu.CompilerParams(dimension_semantics=semantics, vmem_limit_bytes=VMEM_LIMIT)


def _tile(n, want):
    if n <= want:
        return n
    t = (want // LANES) * LANES
    while t > LANES and n % t:
        t -= LANES
    assert n % t == 0, (n, want)
    return t


def _rms_scale(x, g):
    ms = jnp.mean(x * x, axis=-1, keepdims=True)
    return x * lax.rsqrt(ms + RMS_EPS) * g


def _norm_matmul_kernel(x_ref, g_ref, w_ref, o_ref, xn_ref):
    @pl.when(pl.program_id(1) == 0)
    def _():
        xn_ref[...] = _rms_scale(x_ref[...], g_ref[...]).astype(BF16)

    o_ref[...] = jnp.dot(xn_ref[...], w_ref[...], preferred_element_type=F32).astype(o_ref.dtype)


def norm_matmul(x, g, w, out_dtype, tm=512, tn=1024):
    m, k = x.shape
    n = w.shape[1]
    tm, tn = _tile(m, tm), _tile(n, tn)
    return pl.pallas_call(
        _norm_matmul_kernel,
        grid=(m // tm, n // tn),
        in_specs=[
            pl.BlockSpec((tm, k), lambda i, j: (i, 0)),
            pl.BlockSpec((1, k), lambda i, j: (0, 0)),
            pl.BlockSpec((k, tn), lambda i, j: (0, j)),
        ],
        out_specs=pl.BlockSpec((tm, tn), lambda i, j: (i, j)),
        out_shape=jax.ShapeDtypeStruct((m, n), out_dtype),
        scratch_shapes=[pltpu.VMEM((tm, k), BF16)],
        compiler_params=_params("parallel", "arbitrary"),
        name="norm_matmul",
    )(x, g.reshape(1, k), w)


def _matmul_residual_kernel(a_ref, w_ref, r_ref, o_ref):
    o_ref[...] = r_ref[...] + jnp.dot(a_ref[...], w_ref[...], preferred_element_type=F32)


def matmul_residual(a, w, res, tm=512, tn=1024):
    m, k = a.shape
    n = w.shape[1]
    tm, tn = _tile(m, tm), _tile(n, tn)
    return pl.pallas_call(
        _matmul_residual_kernel,
        grid=(m // tm, n // tn),
        in_specs=[
            pl.BlockSpec((tm, k), lambda i, j: (i, 0)),
            pl.BlockSpec((k, tn), lambda i, j: (0, j)),
            pl.BlockSpec((tm, tn), lambda i, j: (i, j)),
        ],
        out_specs=pl.BlockSpec((tm, tn), lambda i, j: (i, j)),
        out_shape=jax.ShapeDtypeStruct((m, n), F32),
        compiler_params=_params("parallel", "arbitrary"),
        name="matmul_residual",
    )(a, w, res)


def _final_norm_kernel(x_ref, g_ref, o_ref):
    o_ref[...] = _rms_scale(x_ref[...], g_ref[...])


def final_rms_norm(x, g, tm=512):
    m, k = x.shape
    tm = _tile(m, tm)
    return pl.pallas_call(
        _final_norm_kernel,
        grid=(m // tm,),
        in_specs=[pl.BlockSpec((tm, k), lambda i: (i, 0)), pl.BlockSpec((1, k), lambda i: (0, 0))],
        out_specs=pl.BlockSpec((tm, k), lambda i: (i, 0)),
        out_shape=jax.ShapeDtypeStruct((m, k), F32),
        compiler_params=_params("parallel"),
        name="final_norm",
    )(x, g.reshape(1, k))


def _sb_attn_kernel(q_ref, k_ref, v_ref, o_ref, *, bq, bk, d, heads, scale):
    i = pl.program_id(2)
    n_diag = bq // bk
    row = lax.broadcasted_iota(jnp.int32, (bk, bk), 0)
    col = lax.broadcasted_iota(jnp.int32, (bk, bk), 1)
    suffix_ones = (row >= col).astype(BF16)
    q_idx = lax.broadcasted_iota(jnp.int32, (bq, bk), 0)
    k_idx = lax.broadcasted_iota(jnp.int32, (bq, bk), 1)

    def scores(hh, m):
        lanes = slice(hh * d, (hh + 1) * d)
        kb = k_ref[0, pl.ds(pl.multiple_of(m * bk, bk), bk), lanes]
        return lax.dot_general(q_ref[0, :, lanes], kb, _NT, preferred_element_type=F32)

    def absorb(hh, m, qk, acc, log_run, key_offset):
        vb = v_ref[0, pl.ds(pl.multiple_of(m * bk, bk), bk), hh * d:(hh + 1) * d]
        z = qk * scale
        neg_z = -z
        log_keep = jnp.minimum(neg_z, 0.0) - jnp.log2(1.0 + jnp.exp2(jnp.minimum(z, neg_z)))
        if key_offset is not None:
            strict = k_idx + key_offset < q_idx
            log_keep = jnp.where(strict, log_keep, 0.0)
        incl = jnp.dot(log_keep.astype(BF16), suffix_ones, preferred_element_type=F32)
        logw = z + incl + log_run
        if key_offset is not None:
            logw = jnp.where(strict, logw, -jnp.inf)
        p = jnp.exp2(logw)
        acc = acc + jnp.dot(p.astype(BF16), vb, preferred_element_type=F32)
        return acc, log_run + incl[:, 0:1]

    def step(m, state, key_offset):
        return tuple(absorb(hh, m, scores(hh, m), *state[hh], key_offset) for hh in range(heads))

    state = tuple((jnp.zeros((bq, d), F32), jnp.zeros((bq, 1), F32)) for _ in range(heads))
    for dd in reversed(range(n_diag)):
        state = step(i * n_diag + dd, state, dd * bk)
    n_before = i * n_diag
    state = lax.fori_loop(0, n_before, lambda t, c: step(n_before - 1 - t, c, None), state)
    for hh in range(heads):
        o_ref[0, :, hh * d:(hh + 1) * d] = state[hh][0].astype(o_ref.dtype)


def sb_attention(qkv, n_heads, bq=1024, bk=256, heads_per_step=2):
    b, s, three_hd = qkv.shape
    d = three_hd // (3 * n_heads)
    assert d == LANES
    bq = min(bq, s)
    bk = min(bk, bq)
    hp = heads_per_step
    assert s % bq == 0 and bq % bk == 0 and n_heads % hp == 0
    n_grp = n_heads // hp
    kern = functools.partial(_sb_attn_kernel, bq=bq, bk=bk, d=d, heads=hp, scale=d ** -0.5 * math.log2(math.e))
    return pl.pallas_call(
        kern,
        grid=(b, n_grp, s // bq),
        in_specs=[
            pl.BlockSpec((1, bq, hp * d), lambda bi, h, i: (bi, i, h)),
            pl.BlockSpec((1, s, hp * d), lambda bi, h, i: (bi, 0, n_grp + h)),
            pl.BlockSpec((1, s, hp * d), lambda bi, h, i: (bi, 0, 2 * n_grp + h)),
        ],
        out_specs=pl.BlockSpec((1, bq, hp * d), lambda bi, h, i: (bi, i, h)),
        out_shape=jax.ShapeDtypeStruct((b, s, n_heads * d), BF16),
        compiler_params=_params("parallel", "parallel", "arbitrary"),
        name="sb_attention",
    )(qkv, qkv, qkv)


def _swa_kernel(q_ref, kp_ref, kc_ref, vp_ref, vc_ref, slope_ref, sink_ref, o_ref, *, n_kv, grp, d, blk, scale):
    i = pl.program_id(1)
    rows = grp * blk
    qpos = lax.broadcasted_iota(jnp.int32, (rows, 2 * blk), 0) & (blk - 1)
    kcol = lax.broadcasted_iota(jnp.int32, (rows, 2 * blk), 1)
    dist = qpos + blk - kcol
    first_key = jnp.where(i > 0, 0, blk)
    in_window = (dist >= 0) & (dist < SWA_WINDOW) & (kcol >= first_key)
    dist_f = dist.astype(F32)
    outs = []
    for kh in range(n_kv):
        qg = jnp.concatenate(
            [q_ref[0, :, (kh * grp + g) * d:(kh * grp + g + 1) * d] for g in range(grp)], axis=0)
        kk = jnp.concatenate([kp_ref[0, :, kh * d:(kh + 1) * d], kc_ref[0, :, kh * d:(kh + 1) * d]], axis=0)
        vv = jnp.concatenate([vp_ref[0, :, kh * d:(kh + 1) * d], vc_ref[0, :, kh * d:(kh + 1) * d]], axis=0)
        z = lax.dot_general(qg, kk, _NT, preferred_element_type=F32) * scale
        z = z - slope_ref[kh] * dist_f
        z = jnp.where(in_window, z, -jnp.inf)
        sink = sink_ref[kh]
        mx = jnp.maximum(jnp.max(z, axis=-1, keepdims=True), sink)
        p = jnp.exp(z - mx)
        den = jnp.sum(p, axis=-1, keepdims=True) + jnp.exp(sink - mx)
        o = jnp.dot(p.astype(BF16), vv, preferred_element_type=F32) / den
        outs.extend(o[g * blk:(g + 1) * blk] for g in range(grp))
    o_ref[0] = jnp.concatenate(outs, axis=1).astype(o_ref.dtype)


def swa_attention(qkv, sinks, n_heads, n_kv, d):
    b, s, _ = qkv.shape
    blk = SWA_WINDOW
    grp = n_heads // n_kv
    qd, kd = n_heads * d, n_kv * d
    assert qd % kd == 0 and kd % LANES == 0 and s % blk == 0
    slopes = 2.0 ** (-8.0 * jnp.arange(1, n_heads + 1, dtype=F32) / n_heads)
    per_row = lambda a: jnp.repeat(a.astype(F32).reshape(n_kv, grp), blk, axis=1).reshape(n_kv, grp * blk, 1)
    kern = functools.partial(_swa_kernel, n_kv=n_kv, grp=grp, d=d, blk=blk, scale=d ** -0.5)
    prev = lambda bi, i: (bi, jnp.maximum(i - 1, 0), qd // kd)
    cur = lambda bi, i: (bi, i, qd // kd)
    prev_v = lambda bi, i: (bi, jnp.maximum(i - 1, 0), qd // kd + 1)
    cur_v = lambda bi, i: (bi, i, qd // kd + 1)
    const = lambda bi, i: (0, 0, 0)
    return pl.pallas_call(
        kern,
        grid=(b, s // blk),
        in_specs=[
            pl.BlockSpec((1, blk, qd), lambda bi, i: (bi, i, 0)),
            pl.BlockSpec((1, blk, kd), prev),
            pl.BlockSpec((1, blk, kd), cur),
            pl.BlockSpec((1, blk, kd), prev_v),
            pl.BlockSpec((1, blk, kd), cur_v),
            pl.BlockSpec((n_kv, grp * blk, 1), const),
            pl.BlockSpec((n_kv, grp * blk, 1), const),
        ],
        out_specs=pl.BlockSpec((1, blk, qd), lambda bi, i: (bi, i, 0)),
        out_shape=jax.ShapeDtypeStruct((b, s, qd), BF16),
        compiler_params=_params("parallel", "arbitrary"),
        name="swa_attention",
    )(qkv, qkv, qkv, qkv, qkv, per_row(slopes), per_row(sinks))


def _rope_lanes(r, cos_ref, sin_lo_ref, sin_hi_ref):
    half = MLA_ROPE_DIM // 2
    return (r * cos_ref[...]
            + pltpu.roll(r, LANES - half, 1) * sin_lo_ref[...]
            + pltpu.roll(r, half, 1) * sin_hi_ref[...])


def _mla_q_kernel(x_ref, g_ref, w_ref, cos_ref, slo_ref, shi_ref, o_ref, *, n_heads):
    xn = _rms_scale(x_ref[...], g_ref[...]).astype(BF16)
    for h in range(n_heads):
        a = jnp.dot(xn, w_ref[:, 2 * LANES * h:2 * LANES * (h + 1)], preferred_element_type=F32)
        o_ref[:, 2 * LANES * h:2 * LANES * h + LANES] = a[:, :LANES].astype(BF16)
        o_ref[:, 2 * LANES * h + LANES:2 * LANES * (h + 1)] = _rope_lanes(
            a[:, LANES:], cos_ref, slo_ref, shi_ref).astype(BF16)


def _mla_kv_kernel(x_ref, g_ref, xr_ref, wk_ref, wvt_ref, cos_ref, slo_ref, shi_ref, kn_ref, vt_ref, kr_ref, *, chunk):
    xn = _rms_scale(x_ref[...], g_ref[...]).astype(BF16)
    for c in range(wk_ref.shape[1] // chunk):
        cols = slice(c * chunk, (c + 1) * chunk)
        kn_ref[:, cols] = jnp.dot(xn, wk_ref[:, cols], preferred_element_type=F32).astype(BF16)
        vt_ref[cols, :] = lax.dot_general(wvt_ref[cols, :], xn, _NT, preferred_element_type=F32).astype(BF16)
    kr_ref[...] = _rope_lanes(xr_ref[...], cos_ref, slo_ref, shi_ref).astype(BF16)


def _rope_tables(s):
    half = MLA_ROPE_DIM // 2
    inv_freq = ROPE_THETA ** (-jnp.arange(half, dtype=F32) / half)
    ang = jnp.arange(s, dtype=F32)[:, None] * inv_freq[None, :]
    cos, sin = jnp.cos(ang), jnp.sin(ang)
    zeros = jnp.zeros((s, LANES - 2 * half), F32)
    z_half = jnp.zeros((s, half), F32)
    cos_t = jnp.concatenate([cos, cos, zeros], axis=1)
    sin_lo = jnp.concatenate([-sin, z_half, zeros], axis=1)
    sin_hi = jnp.concatenate([z_half, sin, zeros], axis=1)
    return cos_t, sin_lo, sin_hi


def mla_projections(down, g_q, g_kv, w_uq_pad, w_kn, w_vt, seq, tm=512):
    t = down.shape[0]
    tm = _tile(seq, tm)
    n_heads = w_uq_pad.shape[1] // (2 * LANES)
    tables = _rope_tables(seq)
    n_pos = seq // tm
    tab_spec = pl.BlockSpec((tm, LANES), lambda i: (i % n_pos, 0))
    q = pl.pallas_call(
        functools.partial(_mla_q_kernel, n_heads=n_heads),
        grid=(t // tm,),
        in_specs=[
            pl.BlockSpec((tm, MLA_Q_RANK), lambda i: (i, 0)),
            pl.BlockSpec((1, MLA_Q_RANK), lambda i: (0, 0)),
            pl.BlockSpec(w_uq_pad.shape, lambda i: (0, 0)),
            tab_spec, tab_spec, tab_spec,
        ],
        out_specs=pl.BlockSpec((tm, w_uq_pad.shape[1]), lambda i: (i, 0)),
        out_shape=jax.ShapeDtypeStruct((t, w_uq_pad.shape[1]), BF16),
        compiler_params=_params("parallel"),
        name="mla_q_proj",
    )(down, g_q.reshape(1, -1), w_uq_pad, *tables)
    assert MLA_Q_RANK == MLA_KV_RANK and (MLA_Q_RANK + MLA_KV_RANK) % LANES == 0
    hv = w_kn.shape[1]
    k_nope, v_t, k_rope = pl.pallas_call(
        functools.partial(_mla_kv_kernel, chunk=_tile(hv, 512)),
        grid=(t // tm,),
        in_specs=[
            pl.BlockSpec((tm, MLA_KV_RANK), lambda i: (i, 1)),
            pl.BlockSpec((1, MLA_KV_RANK), lambda i: (0, 0)),
            pl.BlockSpec((tm, LANES), lambda i: (i, (MLA_Q_RANK + MLA_KV_RANK) // LANES)),
            pl.BlockSpec(w_kn.shape, lambda i: (0, 0)),
            pl.BlockSpec(w_vt.shape, lambda i: (0, 0)),
            tab_spec, tab_spec, tab_spec,
        ],
        out_specs=[pl.BlockSpec((tm, hv), lambda i: (i, 0)),
                   pl.BlockSpec((hv, tm), lambda i: (0, i)),
                   pl.BlockSpec((tm, LANES), lambda i: (i, 0))],
        out_shape=[jax.ShapeDtypeStruct((t, hv), BF16), jax.ShapeDtypeStruct((hv, t), BF16),
                   jax.ShapeDtypeStruct((t, LANES), BF16)],
        compiler_params=_params("parallel"),
        name="mla_kv_proj",
    )(down, g_kv.reshape(1, -1), down, w_kn, w_vt, *tables)
    return q, k_nope, v_t, k_rope


def _mla_attn_kernel(q_ref, kn_ref, kr_ref, vt_ref, o_ref, k_ref, *, blk, heads, scale):
    i = pl.program_id(2)
    width = 2 * LANES

    @pl.when(i == 0)
    def _():
        for hh in range(heads):
            k_ref[hh, :, :LANES] = kn_ref[0, :, hh * LANES:(hh + 1) * LANES]
            k_ref[hh, :, LANES:] = kr_ref[0]

    def scores(hh, m):
        kb = k_ref[hh, pl.ds(pl.multiple_of(m * blk, blk), blk), :]
        return lax.dot_general(kb, q_ref[0, :, hh * width:(hh + 1) * width], _NT, preferred_element_type=F32)

    def absorb(hh, m, z_t, acc_t, mx, den, diagonal):
        v_t = vt_ref[hh * LANES:(hh + 1) * LANES, pl.ds(pl.multiple_of(m * blk, blk), blk)]
        z_t = z_t * scale
        if diagonal:
            key = lax.broadcasted_iota(jnp.int32, (blk, blk), 0)
            qry = lax.broadcasted_iota(jnp.int32, (blk, blk), 1)
            z_t = jnp.where(key <= qry, z_t, -jnp.inf)
        new_mx = jnp.maximum(mx, jnp.max(z_t, axis=0, keepdims=True))
        corr = jnp.exp(mx - new_mx)
        p_t = jnp.exp(z_t - new_mx)
        den = den * corr + jnp.sum(p_t, axis=0, keepdims=True)
        acc_t = acc_t * corr + jnp.dot(v_t, p_t.astype(BF16), preferred_element_type=F32)
        return acc_t, new_mx, den

    def body(m, carry):
        z_cur, state = carry
        z_next = tuple(scores(hh, m + 1) for hh in range(heads))
        state = tuple(absorb(hh, m, z_cur[hh], *state[hh], False) for hh in range(heads))
        return z_next, state

    init = tuple((jnp.zeros((LANES, blk), F32), jnp.full((1, blk), -jnp.inf, F32), jnp.zeros((1, blk), F32))
                 for _ in range(heads))
    z_diag, state = lax.fori_loop(0, i, body, (tuple(scores(hh, 0) for hh in range(heads)), init))
    for hh in range(heads):
        acc_t, _, den = absorb(hh, i, z_diag[hh], *state[hh], True)
        o_ref[0, :, hh * LANES:(hh + 1) * LANES] = (acc_t / den).T.astype(o_ref.dtype)


def mla_attention(q, k_nope, k_rope, v_t, n_heads, blk=512, heads_per_step=2):
    b, s, _ = q.shape
    blk = min(blk, s)
    hp = heads_per_step
    assert s % blk == 0 and n_heads % hp == 0 and MLA_V_DIM == LANES
    scale = (MLA_NOPE_DIM + MLA_ROPE_DIM) ** -0.5
    bq = blk
    return pl.pallas_call(
        functools.partial(_mla_attn_kernel, blk=blk, heads=hp, scale=scale),
        grid=(b, n_heads // hp, s // blk),
        in_specs=[
            pl.BlockSpec((1, bq, hp * 2 * LANES), lambda bi, h, i: (bi, i, h)),
            pl.BlockSpec((1, s, hp * LANES), lambda bi, h, i: (bi, 0, h)),
            pl.BlockSpec((1, s, LANES), lambda bi, h, i: (bi, 0, 0)),
            pl.BlockSpec((hp * LANES, s), lambda bi, h, i: (h, bi)),
        ],
        out_specs=pl.BlockSpec((1, bq, hp * LANES), lambda bi, h, i: (bi, i, h)),
        out_shape=jax.ShapeDtypeStruct((b, s, n_heads * MLA_V_DIM), BF16),
        scratch_shapes=[pltpu.VMEM((hp, s, 2 * LANES), BF16)],
        compiler_params=_params("parallel", "parallel", "arbitrary"),
        name="mla_attention",
    )(q, k_nope, k_rope, v_t)


def _extract_top(x, out_ref, k):
    n_rows = x.shape[0]
    rows = lax.broadcasted_iota(jnp.int32, x.shape, 0).astype(F32)

    def body(r, x):
        m = jnp.max(x, axis=0, keepdims=True)
        out_ref[pl.ds(r, 1), :] = m
        first = jnp.min(jnp.where(x == m, rows, float(n_rows)), axis=0, keepdims=True)
        return jnp.where(rows == first, -jnp.inf, x)

    lax.fori_loop(0, k, body, x)


def _peer_select_kernel(qh_ref, keys_ref, st_ref, et_ref, thr_ref, v1_ref, v2_ref, top_ref, *, n_keys, key_half, topk):
    halves = []
    for half, v_ref in ((0, v1_ref), (1, v2_ref)):
        qs = qh_ref[:, half * key_half:(half + 1) * key_half]
        s_t = lax.dot_general(keys_ref[half], qs, _NT, preferred_element_type=F32)
        st_ref[0, half * n_keys:(half + 1) * n_keys, :] = s_t
        _extract_top(s_t, v_ref, topk)
        halves.append(s_t)
    v1, v2 = v1_ref[...], v2_ref[...]
    assert topk == 16
    slabs = [v1 + v2[0:1]]
    slabs += [v1[0:8] + v2[b:b + 1] for b in range(1, 8)]
    slabs += [v1[0:1] + v2[8:16]]
    _extract_top(jnp.concatenate(slabs, axis=0), top_ref, topk)
    top = top_ref[...]
    thr_ref[0] = top[topk - 1:topk]
    z_sum = jnp.sum(jnp.exp(top - top[0:1]), axis=0, keepdims=True)
    et_ref[0, :n_keys, :] = jnp.exp(halves[0] - v1[0:1])
    et_ref[0, n_keys:, :] = jnp.exp(halves[1] - v2[0:1]) / z_sum


def peer_select(qh, sub_keys, tt=512):
    t = qh.shape[0]
    _, n_keys, key_half = sub_keys.shape
    n_heads = qh.shape[1] // (2 * key_half)
    tt = _tile(t, tt)
    kern = functools.partial(_peer_select_kernel, n_keys=n_keys, key_half=key_half, topk=PEER_TOPK)
    return pl.pallas_call(
        kern,
        grid=(t // tt, n_heads),
        in_specs=[
            pl.BlockSpec((tt, 2 * key_half), lambda i, h: (i, h)),
            pl.BlockSpec(sub_keys.shape, lambda i, h: (0, 0, 0)),
        ],
        out_specs=[
            pl.BlockSpec((1, 2 * n_keys, tt), lambda i, h: (h, 0, i)),
            pl.BlockSpec((1, 2 * n_keys, tt), lambda i, h: (h, 0, i)),
            pl.BlockSpec((1, 1, tt), lambda i, h: (h, 0, i)),
        ],
        out_shape=[
            jax.ShapeDtypeStruct((n_heads, 2 * n_keys, t), F32),
            jax.ShapeDtypeStruct((n_heads, 2 * n_keys, t), F32),
            jax.ShapeDtypeStruct((n_heads, 1, t), F32),
        ],
        scratch_shapes=[pltpu.VMEM((PEER_TOPK, tt), F32)] * 3,
        compiler_params=_params("parallel", "arbitrary"),
        name="peer_select",
    )(qh, sub_keys)


def _gelu_tanh(x):
    return 0.5 * x * (1.0 + jnp.tanh(math.sqrt(2.0 / math.pi) * (x + 0.044715 * (x * x * x))))


def _peer_dense_kernel(h_ref, g_ref, u_ref, v_ref, st_ref, et_ref, thr_ref, o_ref, xn_ref, *, n_heads, n_keys, te):
    e = pl.program_id(1)

    @pl.when(e == 0)
    def _():
        x = h_ref[...]
        xn_ref[...] = _rms_scale(x, g_ref[...]).astype(BF16)
        o_ref[...] = x

    hidden = lax.dot_general(xn_ref[...], u_ref[...], _NT, preferred_element_type=F32)
    rows_per_step = te // n_keys
    blocks = []
    for ii in range(rows_per_step):
        i_row = e * rows_per_step + ii
        gate_t = None
        for h in range(n_heads):
            score = st_ref[h, n_keys:2 * n_keys, :] + st_ref[h, pl.ds(i_row, 1), :]
            weight = et_ref[h, n_keys:2 * n_keys, :] * et_ref[h, pl.ds(i_row, 1), :]
            g = jnp.where(score >= thr_ref[h], weight, 0.0)
            gate_t = g if gate_t is None else gate_t + g
        blocks.append(gate_t)
    gate = jnp.concatenate(blocks, axis=0).T
    gh = (gate * _gelu_tanh(hidden)).astype(BF16)
    o_ref[...] += jnp.dot(gh, v_ref[...], preferred_element_type=F32)


def peer_dense(h, g, u, v, scores_t, factors_t, thr, tt=512, te=512):
    t, d = h.shape
    n_exp = u.shape[0]
    n_heads, two_keys, _ = scores_t.shape
    n_keys = two_keys // 2
    tt, te = _tile(t, tt), _tile(n_exp, te)
    assert te % n_keys == 0 and n_exp == n_keys * n_keys
    kern = functools.partial(_peer_dense_kernel, n_heads=n_heads, n_keys=n_keys, te=te)
    return pl.pallas_call(
        kern,
        grid=(t // tt, n_exp // te),
        in_specs=[
            pl.BlockSpec((tt, d), lambda i, e: (i, 0)),
            pl.BlockSpec((1, d), lambda i, e: (0, 0)),
            pl.BlockSpec((te, d), lambda i, e: (e, 0)),
            pl.BlockSpec((te, d), lambda i, e: (e, 0)),
            pl.BlockSpec((n_heads, two_keys, tt), lambda i, e: (0, 0, i)),
            pl.BlockSpec((n_heads, two_keys, tt), lambda i, e: (0, 0, i)),
            pl.BlockSpec((n_heads, 1, tt), lambda i, e: (0, 0, i)),
        ],
        out_specs=pl.BlockSpec((tt, d), lambda i, e: (i, 0)),
        out_shape=jax.ShapeDtypeStruct((t, d), F32),
        scratch_shapes=[pltpu.VMEM((tt, d), BF16)],
        compiler_params=_params("parallel", "arbitrary"),
        name="peer_dense",
    )(h, g.reshape(1, d), u, v, scores_t, factors_t, thr)


def _ple_kernel(x_ref, g_ref, wg_ref, p_ref, wp_ref, r_ref, o_ref, xn_ref, pb_ref):
    @pl.when(pl.program_id(1) == 0)
    def _():
        xn_ref[...] = _rms_scale(x_ref[...], g_ref[...]).astype(BF16)
        pb_ref[...] = p_ref[...].astype(BF16)

    gate = 1.0 / (1.0 + jnp.exp(-jnp.dot(xn_ref[...], wg_ref[...], preferred_element_type=F32)))
    o_ref[...] = r_ref[...] + jnp.dot(pb_ref[...], wp_ref[...], preferred_element_type=F32) * gate


def per_layer_embedding(h, p, g, w_g, w_p, tm=512, tn=1024):
    t, d = h.shape
    pd = p.shape[1]
    tm, tn = _tile(t, tm), _tile(d, tn)
    return pl.pallas_call(
        _ple_kernel,
        grid=(t // tm, d // tn),
        in_specs=[
            pl.BlockSpec((tm, d), lambda i, j: (i, 0)),
            pl.BlockSpec((1, d), lambda i, j: (0, 0)),
            pl.BlockSpec((d, tn), lambda i, j: (0, j)),
            pl.BlockSpec((tm, pd), lambda i, j: (i, 0)),
            pl.BlockSpec((pd, tn), lambda i, j: (0, j)),
            pl.BlockSpec((tm, tn), lambda i, j: (i, j)),
        ],
        out_specs=pl.BlockSpec((tm, tn), lambda i, j: (i, j)),
        out_shape=jax.ShapeDtypeStruct((t, d), F32),
        scratch_shapes=[pltpu.VMEM((tm, d), BF16), pltpu.VMEM((tm, pd), BF16)],
        compiler_params=_params("parallel", "arbitrary"),
        name="per_layer_embedding",
    )(h, g.reshape(1, d), w_g, p, w_p, h)


def _pad_cols(w, multiple):
    extra = (-w.shape[-1]) % multiple
    return jnp.pad(w, ((0, 0), (0, extra))) if extra else w


def kernel(x, p, attn_norm, ffn_norm, ple_norm, final_norm, sb_w_qkv, sb_w_o, swa_w_qkv, swa_w_o, swa_sinks,
           mla_w_down, mla_q_norm, mla_kv_norm, mla_w_uq, mla_w_ukv, mla_w_o,
           peer_w_q, peer_sub_keys, peer_u, peer_v, ple_w_p, ple_w_g):
    b, s, d = x.shape
    t = b * s
    depth = p.shape[0]
    h = x.reshape(t, d)
    for i in range(depth):
        kind, j = i % N_MIXERS, i // N_MIXERS
        if kind == 0:
            qkv = norm_matmul(h, attn_norm[i], sb_w_qkv[j].astype(BF16), BF16)
            o = sb_attention(qkv.reshape(b, s, -1), SB_HEADS)
            w_o = sb_w_o[j]
        elif kind == 1:
            qkv = norm_matmul(h, attn_norm[i], swa_w_qkv[j].astype(BF16), BF16, tn=1280)
            head_dim = swa_w_o.shape[1] // SWA_HEADS
            o = swa_attention(qkv.reshape(b, s, -1), swa_sinks[j], SWA_HEADS, SWA_KV_HEADS, head_dim)
            w_o = swa_w_o[j]
        else:
            down = norm_matmul(h, attn_norm[i], _pad_cols(mla_w_down[j], LANES).astype(BF16), F32, tn=2048)
            qk_dim = MLA_NOPE_DIM + MLA_ROPE_DIM
            w_uq_pad = jnp.pad(mla_w_uq[j].reshape(MLA_Q_RANK, MLA_HEADS, qk_dim),
                               ((0, 0), (0, 0), (0, 2 * LANES - qk_dim))).reshape(MLA_Q_RANK, -1)
            w_ukv = mla_w_ukv[j].reshape(MLA_KV_RANK, MLA_HEADS, 2, LANES)
            w_kn = w_ukv[:, :, 0].reshape(MLA_KV_RANK, -1).astype(BF16)
            w_vt = w_ukv[:, :, 1].reshape(MLA_KV_RANK, -1).T.astype(BF16)
            q, k_nope, v_t, k_rope = mla_projections(down, mla_q_norm[j], mla_kv_norm[j], w_uq_pad.astype(BF16),
                                                     w_kn, w_vt, s)
            o = mla_attention(q.reshape(b, s, -1), k_nope.reshape(b, s, -1), k_rope.reshape(b, s, -1), v_t, MLA_HEADS)
            w_o = mla_w_o[j]
        h = matmul_residual(o.reshape(t, -1), w_o.astype(BF16), h)
        qh = norm_matmul(h, ffn_norm[i], peer_w_q[i].astype(BF16), BF16)
        scores_t, factors_t, thr = peer_select(qh, peer_sub_keys[i].astype(BF16))
        h = peer_dense(h, ffn_norm[i], peer_u[i].astype(BF16), peer_v[i].astype(BF16), scores_t, factors_t, thr)
        h = per_layer_embedding(h, p[i].reshape(t, -1), ple_norm[i], ple_w_g[i].astype(BF16), ple_w_p[i].astype(BF16))
    return final_rms_norm(h, final_norm).reshape(b, s, d)
```

```python
import functools
import math

import jax
import jax.numpy as jnp
from jax import lax
from jax.experimental import pallas as pl
from jax.experimental.pallas import tpu as pltpu

F32 = jnp.float32
BF16 = jnp.bfloat16

RMS_EPS = 1e-6
N_MIXERS = 3
SB_HEADS = 16
SWA_HEADS = 32
SWA_KV_HEADS = 4
SWA_WINDOW = 128
MLA_HEADS = 16
MLA_Q_RANK = 512
MLA_KV_RANK = 512
MLA_NOPE_DIM = 128
MLA_ROPE_DIM = 64
MLA_V_DIM = 128
ROPE_THETA = 10000.0
PEER_N_KEYS = 64
PEER_TOPK = 16

EXP2_UNDERFLOW = -150.0
LANES = 128
V7X_VMEM_BYTES = 64 * 1024 * 1024
VMEM_LIMIT = V7X_VMEM_BYTES * 3 // 4

_NT = (((1,), (1,)), ((), ()))


def _params(*semantics):
    return pltpu.CompilerParams(dimension_semantics=semantics, vmem_limit_bytes=VMEM_LIMIT)


def _tile(n, want):
    if n <= want:
        return n
    t = (want // LANES) * LANES
    while t > LANES and n % t:
        t -= LANES
    assert n % t == 0, (n, want)
    return t


def _rms_scale(x, g):
    ms = jnp.mean(x * x, axis=-1, keepdims=True)
    return x * lax.rsqrt(ms + RMS_EPS) * g


def _norm_matmul_kernel(x_ref, g_ref, w_ref, o_ref, xn_ref):
    @pl.when(pl.program_id(1) == 0)
    def _():
        xn_ref[...] = _rms_scale(x_ref[...], g_ref[...]).astype(BF16)

    o_ref[...] = jnp.dot(xn_ref[...], w_ref[...], preferred_element_type=F32).astype(o_ref.dtype)


def norm_matmul(x, g, w, out_dtype, tm=512, tn=1024):
    m, k = x.shape
    n = w.shape[1]
    tm, tn = _tile(m, tm), _tile(n, tn)
    return pl.pallas_call(
        _norm_matmul_kernel,
        grid=(m // tm, n // tn),
        in_specs=[
            pl.BlockSpec((tm, k), lambda i, j: (i, 0)),
            pl.BlockSpec((1, k), lambda i, j: (0, 0)),
            pl.BlockSpec((k, tn), lambda i, j: (0, j)),
        ],
        out_specs=pl.BlockSpec((tm, tn), lambda i, j: (i, j)),
        out_shape=jax.ShapeDtypeStruct((m, n), out_dtype),
        scratch_shapes=[pltpu.VMEM((tm, k), BF16)],
        compiler_params=_params("parallel", "arbitrary"),
        name="norm_matmul",
    )(x, g.reshape(1, k), w)


def _matmul_residual_kernel(a_ref, w_ref, r_ref, o_ref):
    o_ref[...] = r_ref[...] + jnp.dot(a_ref[...], w_ref[...], preferred_element_type=F32)


def matmul_residual(a, w, res, tm=512, tn=1024):
    m, k = a.shape
    n = w.shape[1]
    tm, tn = _tile(m, tm), _tile(n, tn)
    return pl.pallas_call(
        _matmul_residual_kernel,
        grid=(m // tm, n // tn),
        in_specs=[
            pl.BlockSpec((tm, k), lambda i, j: (i, 0)),
            pl.BlockSpec((k, tn), lambda i, j: (0, j)),
            pl.BlockSpec((tm, tn), lambda i, j: (i, j)),
        ],
        out_specs=pl.BlockSpec((tm, tn), lambda i, j: (i, j)),
        out_shape=jax.ShapeDtypeStruct((m, n), F32),
        compiler_params=_params("parallel", "arbitrary"),
        name="matmul_residual",
    )(a, w, res)


def _final_norm_kernel(x_ref, g_ref, o_ref):
    o_ref[...] = _rms_scale(x_ref[...], g_ref[...])


def final_rms_norm(x, g, tm=512):
    m, k = x.shape
    tm = _tile(m, tm)
    return pl.pallas_call(
        _final_norm_kernel,
        grid=(m // tm,),
        in_specs=[pl.BlockSpec((tm, k), lambda i: (i, 0)), pl.BlockSpec((1, k), lambda i: (0, 0))],
        out_specs=pl.BlockSpec((tm, k), lambda i: (i, 0)),
        out_shape=jax.ShapeDtypeStruct((m, k), F32),
        compiler_params=_params("parallel"),
        name="final_norm",
    )(x, g.reshape(1, k))


def _sb_attn_kernel(q_ref, k_ref, v_ref, o_ref, kmax2_ref, *, bq, bk, d, heads, scale):
    i = pl.program_id(2)
    n_diag = bq // bk
    row = lax.broadcasted_iota(jnp.int32, (bk, bk), 0)
    col = lax.broadcasted_iota(jnp.int32, (bk, bk), 1)
    suffix_ones = (row >= col).astype(BF16)
    q_idx = lax.broadcasted_iota(jnp.int32, (bq, bk), 0)
    k_idx = lax.broadcasted_iota(jnp.int32, (bq, bk), 1)

    def scores(hh, m):
        lanes = slice(hh * d, (hh + 1) * d)
        kb = k_ref[0, pl.ds(pl.multiple_of(m * bk, bk), bk), lanes]
        return lax.dot_general(q_ref[0, :, lanes], kb, _NT, preferred_element_type=F32)

    def absorb(hh, m, qk, acc, log_run, key_offset):
        vb = v_ref[0, pl.ds(pl.multiple_of(m * bk, bk), bk), hh * d:(hh + 1) * d]
        z = qk * scale
        neg_z = -z
        log_keep = jnp.minimum(neg_z, 0.0) - jnp.log2(1.0 + jnp.exp2(jnp.minimum(z, neg_z)))
        if key_offset is not None:
            strict = k_idx + key_offset < q_idx
            log_keep = jnp.where(strict, log_keep, 0.0)
        incl = jnp.dot(log_keep.astype(BF16), suffix_ones, preferred_element_type=F32)
        logw = z + incl + log_run
        if key_offset is not None:
            logw = jnp.where(strict, logw, -jnp.inf)
        p = jnp.exp2(logw)
        acc = acc + jnp.dot(p.astype(BF16), vb, preferred_element_type=F32)
        return acc, log_run + incl[:, 0:1]

    def step(m, state, key_offset):
        return tuple(absorb(hh, m, scores(hh, m), *state[hh], key_offset) for hh in range(heads))

    @pl.when(i == 0)
    def _():
        for hh in range(heads):
            kf = k_ref[0, :, hh * d:(hh + 1) * d].astype(F32)
            k_norm2 = jnp.max(jnp.sum(kf * kf, axis=-1, keepdims=True), axis=0, keepdims=True)
            kmax2_ref[hh] = jnp.broadcast_to(k_norm2, kmax2_ref.shape[1:])

    logit_bound = []
    for hh in range(heads):
        qf = q_ref[0, :, hh * d:(hh + 1) * d].astype(F32)
        q_norm2 = jnp.sum(qf * qf, axis=-1, keepdims=True)
        logit_bound.append(jnp.sqrt(q_norm2 * kmax2_ref[hh][0:1, 0:1]) * (scale * 1.01) + 1.0)

    def any_weight_left(state):
        worst = [jnp.max(state[hh][1] + logit_bound[hh]) for hh in range(heads)]
        return (functools.reduce(jnp.maximum, worst) >= EXP2_UNDERFLOW).astype(jnp.int32)

    state = tuple((jnp.zeros((bq, d), F32), jnp.zeros((bq, 1), F32)) for _ in range(heads))
    for dd in reversed(range(n_diag)):
        state = step(i * n_diag + dd, state, dd * bk)
    n_before = i * n_diag

    def more(c):
        t, go, _ = c
        return jnp.logical_and(t < n_before, go > 0)

    def walk(c):
        t, _, state = c
        state = step(n_before - 1 - t, state, None)
        return t + 1, any_weight_left(state), state

    _, _, state = lax.while_loop(more, walk, (jnp.int32(0), any_weight_left(state), state))
    for hh in range(heads):
        o_ref[0, :, hh * d:(hh + 1) * d] = state[hh][0].astype(o_ref.dtype)


def sb_attention(qkv, n_heads, bq=512, bk=256, heads_per_step=2):
    b, s, three_hd = qkv.shape
    d = three_hd // (3 * n_heads)
    assert d == LANES
    bq = min(bq, s)
    bk = min(bk, bq)
    hp = heads_per_step
    assert s % bq == 0 and bq % bk == 0 and n_heads % hp == 0
    n_grp = n_heads // hp
    kern = functools.partial(_sb_attn_kernel, bq=bq, bk=bk, d=d, heads=hp, scale=d ** -0.5 * math.log2(math.e))
    return pl.pallas_call(
        kern,
        grid=(b, n_grp, s // bq),
        in_specs=[
            pl.BlockSpec((1, bq, hp * d), lambda bi, h, i: (bi, i, h)),
            pl.BlockSpec((1, s, hp * d), lambda bi, h, i: (bi, 0, n_grp + h)),
            pl.BlockSpec((1, s, hp * d), lambda bi, h, i: (bi, 0, 2 * n_grp + h)),
        ],
        out_specs=pl.BlockSpec((1, bq, hp * d), lambda bi, h, i: (bi, i, h)),
        out_shape=jax.ShapeDtypeStruct((b, s, n_heads * d), BF16),
        scratch_shapes=[pltpu.VMEM((hp, 8, LANES), F32)],
        compiler_params=_params("parallel", "parallel", "arbitrary"),
        name="sb_attention",
    )(qkv, qkv, qkv)


def _swa_kernel(q_ref, kp_ref, kc_ref, vp_ref, vc_ref, slope_ref, sink_ref, o_ref, *, n_kv, grp, d, blk, scale):
    i = pl.program_id(1)
    rows = grp * blk
    qpos = lax.broadcasted_iota(jnp.int32, (rows, 2 * blk), 0) & (blk - 1)
    kcol = lax.broadcasted_iota(jnp.int32, (rows, 2 * blk), 1)
    dist = qpos + blk - kcol
    first_key = jnp.where(i > 0, 0, blk)
    in_window = (dist >= 0) & (dist < SWA_WINDOW) & (kcol >= first_key)
    dist_f = dist.astype(F32)
    outs = []
    for kh in range(n_kv):
        qg = jnp.concatenate(
            [q_ref[0, :, (kh * grp + g) * d:(kh * grp + g + 1) * d] for g in range(grp)], axis=0)
        kk = jnp.concatenate([kp_ref[0, :, kh * d:(kh + 1) * d], kc_ref[0, :, kh * d:(kh + 1) * d]], axis=0)
        vv = jnp.concatenate([vp_ref[0, :, kh * d:(kh + 1) * d], vc_ref[0, :, kh * d:(kh + 1) * d]], axis=0)
        z = lax.dot_general(qg, kk, _NT, preferred_element_type=F32) * scale
        z = z - slope_ref[kh] * dist_f
        z = jnp.where(in_window, z, -jnp.inf)
        sink = sink_ref[kh]
        mx = jnp.maximum(jnp.max(z, axis=-1, keepdims=True), sink)
        p = jnp.exp(z - mx)
        den = jnp.sum(p, axis=-1, keepdims=True) + jnp.exp(sink - mx)
        o = jnp.dot(p.astype(BF16), vv, preferred_element_type=F32) / den
        outs.extend(o[g * blk:(g + 1) * blk] for g in range(grp))
    o_ref[0] = jnp.concatenate(outs, axis=1).astype(o_ref.dtype)


def swa_attention(qkv, sinks, n_heads, n_kv, d):
    b, s, _ = qkv.shape
    blk = SWA_WINDOW
    grp = n_heads // n_kv
    qd, kd = n_heads * d, n_kv * d
    assert qd % kd == 0 and kd % LANES == 0 and s % blk == 0
    slopes = 2.0 ** (-8.0 * jnp.arange(1, n_heads + 1, dtype=F32) / n_heads)
    per_row = lambda a: jnp.repeat(a.astype(F32).reshape(n_kv, grp), blk, axis=1).reshape(n_kv, grp * blk, 1)
    kern = functools.partial(_swa_kernel, n_kv=n_kv, grp=grp, d=d, blk=blk, scale=d ** -0.5)
    prev = lambda bi, i: (bi, jnp.maximum(i - 1, 0), qd // kd)
    cur = lambda bi, i: (bi, i, qd // kd)
    prev_v = lambda bi, i: (bi, jnp.maximum(i - 1, 0), qd // kd + 1)
    cur_v = lambda bi, i: (bi, i, qd // kd + 1)
    const = lambda bi, i: (0, 0, 0)
    return pl.pallas_call(
        kern,
        grid=(b, s // blk),
        in_specs=[
            pl.BlockSpec((1, blk, qd), lambda bi, i: (bi, i, 0)),
            pl.BlockSpec((1, blk, kd), prev),
            pl.BlockSpec((1, blk, kd), cur),
            pl.BlockSpec((1, blk, kd), prev_v),
            pl.BlockSpec((1, blk, kd), cur_v),
            pl.BlockSpec((n_kv, grp * blk, 1), const),
            pl.BlockSpec((n_kv, grp * blk, 1), const),
        ],
        out_specs=pl.BlockSpec((1, blk, qd), lambda bi, i: (bi, i, 0)),
        out_shape=jax.ShapeDtypeStruct((b, s, qd), BF16),
        compiler_params=_params("parallel", "arbitrary"),
        name="swa_attention",
    )(qkv, qkv, qkv, qkv, qkv, per_row(slopes), per_row(sinks))


def _rope_lanes(r, cos_ref, sin_lo_ref, sin_hi_ref):
    half = MLA_ROPE_DIM // 2
    return (r * cos_ref[...]
            + pltpu.roll(r, LANES - half, 1) * sin_lo_ref[...]
            + pltpu.roll(r, half, 1) * sin_hi_ref[...])


def _mla_q_kernel(x_ref, g_ref, w_ref, cos_ref, slo_ref, shi_ref, o_ref, *, n_heads):
    xn = _rms_scale(x_ref[...], g_ref[...]).astype(BF16)
    for h in range(n_heads):
        a = jnp.dot(xn, w_ref[:, 2 * LANES * h:2 * LANES * (h + 1)], preferred_element_type=F32)
        o_ref[:, 2 * LANES * h:2 * LANES * h + LANES] = a[:, :LANES].astype(BF16)
        o_ref[:, 2 * LANES * h + LANES:2 * LANES * (h + 1)] = _rope_lanes(
            a[:, LANES:], cos_ref, slo_ref, shi_ref).astype(BF16)


def _mla_kv_kernel(x_ref, g_ref, xr_ref, wk_ref, wvt_ref, cos_ref, slo_ref, shi_ref, kn_ref, vt_ref, kr_ref, *, chunk):
    xn = _rms_scale(x_ref[...], g_ref[...]).astype(BF16)
    for c in range(wk_ref.shape[1] // chunk):
        cols = slice(c * chunk, (c + 1) * chunk)
        kn_ref[:, cols] = jnp.dot(xn, wk_ref[:, cols], preferred_element_type=F32).astype(BF16)
        vt_ref[cols, :] = lax.dot_general(wvt_ref[cols, :], xn, _NT, preferred_element_type=F32).astype(BF16)
    kr_ref[...] = _rope_lanes(xr_ref[...], cos_ref, slo_ref, shi_ref).astype(BF16)


def _rope_tables(s):
    half = MLA_ROPE_DIM // 2
    inv_freq = ROPE_THETA ** (-jnp.arange(half, dtype=F32) / half)
    ang = jnp.arange(s, dtype=F32)[:, None] * inv_freq[None, :]
    cos, sin = jnp.cos(ang), jnp.sin(ang)
    zeros = jnp.zeros((s, LANES - 2 * half), F32)
    z_half = jnp.zeros((s, half), F32)
    cos_t = jnp.concatenate([cos, cos, zeros], axis=1)
    sin_lo = jnp.concatenate([-sin, z_half, zeros], axis=1)
    sin_hi = jnp.concatenate([z_half, sin, zeros], axis=1)
    return cos_t, sin_lo, sin_hi


def mla_projections(down, g_q, g_kv, w_uq_pad, w_kn, w_vt, seq, tm=512):
    t = down.shape[0]
    tm = _tile(seq, tm)
    n_heads = w_uq_pad.shape[1] // (2 * LANES)
    tables = _rope_tables(seq)
    n_pos = seq // tm
    tab_spec = pl.BlockSpec((tm, LANES), lambda i: (i % n_pos, 0))
    q = pl.pallas_call(
        functools.partial(_mla_q_kernel, n_heads=n_heads),
        grid=(t // tm,),
        in_specs=[
            pl.BlockSpec((tm, MLA_Q_RANK), lambda i: (i, 0)),
            pl.BlockSpec((1, MLA_Q_RANK), lambda i: (0, 0)),
            pl.BlockSpec(w_uq_pad.shape, lambda i: (0, 0)),
            tab_spec, tab_spec, tab_spec,
        ],
        out_specs=pl.BlockSpec((tm, w_uq_pad.shape[1]), lambda i: (i, 0)),
        out_shape=jax.ShapeDtypeStruct((t, w_uq_pad.shape[1]), BF16),
        compiler_params=_params("parallel"),
        name="mla_q_proj",
    )(down, g_q.reshape(1, -1), w_uq_pad, *tables)
    assert MLA_Q_RANK == MLA_KV_RANK and (MLA_Q_RANK + MLA_KV_RANK) % LANES == 0
    hv = w_kn.shape[1]
    k_nope, v_t, k_rope = pl.pallas_call(
        functools.partial(_mla_kv_kernel, chunk=_tile(hv, 512)),
        grid=(t // tm,),
        in_specs=[
            pl.BlockSpec((tm, MLA_KV_RANK), lambda i: (i, 1)),
            pl.BlockSpec((1, MLA_KV_RANK), lambda i: (0, 0)),
            pl.BlockSpec((tm, LANES), lambda i: (i, (MLA_Q_RANK + MLA_KV_RANK) // LANES)),
            pl.BlockSpec(w_kn.shape, lambda i: (0, 0)),
            pl.BlockSpec(w_vt.shape, lambda i: (0, 0)),
            tab_spec, tab_spec, tab_spec,
        ],
        out_specs=[pl.BlockSpec((tm, hv), lambda i: (i, 0)),
                   pl.BlockSpec((hv, tm), lambda i: (0, i)),
                   pl.BlockSpec((tm, LANES), lambda i: (i, 0))],
        out_shape=[jax.ShapeDtypeStruct((t, hv), BF16), jax.ShapeDtypeStruct((hv, t), BF16),
                   jax.ShapeDtypeStruct((t, LANES), BF16)],
        compiler_params=_params("parallel"),
        name="mla_kv_proj",
    )(down, g_kv.reshape(1, -1), down, w_kn, w_vt, *tables)
    return q, k_nope, v_t, k_rope


def _mla_attn_kernel(q_ref, kn_ref, kr_ref, vt_ref, o_ref, k_ref, *, blk, heads, scale):
    i = pl.program_id(2)
    width = 2 * LANES

    @pl.when(i == 0)
    def _():
        for hh in range(heads):
            k_ref[hh, :, :LANES] = kn_ref[0, :, hh * LANES:(hh + 1) * LANES]
            k_ref[hh, :, LANES:] = kr_ref[0]

    def scores(hh, m):
        kb = k_ref[hh, pl.ds(pl.multiple_of(m * blk, blk), blk), :]
        return lax.dot_general(kb, q_ref[0, :, hh * width:(hh + 1) * width], _NT, preferred_element_type=F32)

    def absorb(hh, m, z_t, acc_t, mx, den, diagonal):
        v_t = vt_ref[hh * LANES:(hh + 1) * LANES, pl.ds(pl.multiple_of(m * blk, blk), blk)]
        z_t = z_t * scale
        if diagonal:
            key = lax.broadcasted_iota(jnp.int32, (blk, blk), 0)
            qry = lax.broadcasted_iota(jnp.int32, (blk, blk), 1)
            z_t = jnp.where(key <= qry, z_t, -jnp.inf)
        new_mx = jnp.maximum(mx, jnp.max(z_t, axis=0, keepdims=True))
        corr = jnp.exp(mx - new_mx)
        p_t = jnp.exp(z_t - new_mx)
        den = den * corr + jnp.sum(p_t, axis=0, keepdims=True)
        acc_t = acc_t * corr + jnp.dot(v_t, p_t.astype(BF16), preferred_element_type=F32)
        return acc_t, new_mx, den

    def body(m, carry):
        z_cur, state = carry
        z_next = tuple(scores(hh, m + 1) for hh in range(heads))
        state = tuple(absorb(hh, m, z_cur[hh], *state[hh], False) for hh in range(heads))
        return z_next, state

    init = tuple((jnp.zeros((LANES, blk), F32), jnp.full((1, blk), -jnp.inf, F32), jnp.zeros((1, blk), F32))
                 for _ in range(heads))
    z_diag, state = lax.fori_loop(0, i, body, (tuple(scores(hh, 0) for hh in range(heads)), init))
    for hh in range(heads):
        acc_t, _, den = absorb(hh, i, z_diag[hh], *state[hh], True)
        o_ref[0, :, hh * LANES:(hh + 1) * LANES] = (acc_t / den).T.astype(o_ref.dtype)


def mla_attention(q, k_nope, k_rope, v_t, n_heads, blk=512, heads_per_step=2):
    b, s, _ = q.shape
    blk = min(blk, s)
    hp = heads_per_step
    assert s % blk == 0 and n_heads % hp == 0 and MLA_V_DIM == LANES
    scale = (MLA_NOPE_DIM + MLA_ROPE_DIM) ** -0.5
    bq = blk
    return pl.pallas_call(
        functools.partial(_mla_attn_kernel, blk=blk, heads=hp, scale=scale),
        grid=(b, n_heads // hp, s // blk),
        in_specs=[
            pl.BlockSpec((1, bq, hp * 2 * LANES), lambda bi, h, i: (bi, i, h)),
            pl.BlockSpec((1, s, hp * LANES), lambda bi, h, i: (bi, 0, h)),
            pl.BlockSpec((1, s, LANES), lambda bi, h, i: (bi, 0, 0)),
            pl.BlockSpec((hp * LANES, s), lambda bi, h, i: (h, bi)),
        ],
        out_specs=pl.BlockSpec((1, bq, hp * LANES), lambda bi, h, i: (bi, i, h)),
        out_shape=jax.ShapeDtypeStruct((b, s, n_heads * MLA_V_DIM), BF16),
        scratch_shapes=[pltpu.VMEM((hp, s, 2 * LANES), BF16)],
        compiler_params=_params("parallel", "parallel", "arbitrary"),
        name="mla_attention",
    )(q, k_nope, k_rope, v_t)


def _extract_top(x, out_ref, k):
    n_rows = x.shape[0]
    rows = lax.broadcasted_iota(jnp.int32, x.shape, 0).astype(F32)

    def body(r, x):
        m = jnp.max(x, axis=0, keepdims=True)
        out_ref[pl.ds(r, 1), :] = m
        first = jnp.min(jnp.where(x == m, rows, float(n_rows)), axis=0, keepdims=True)
        return jnp.where(rows == first, -jnp.inf, x)

    lax.fori_loop(0, k, body, x)


def _peer_select_kernel(qh_ref, keys_ref, st_ref, et_ref, thr_ref, v1_ref, v2_ref, top_ref, *, n_keys, key_half, topk):
    halves = []
    for half, v_ref in ((0, v1_ref), (1, v2_ref)):
        qs = qh_ref[:, half * key_half:(half + 1) * key_half]
        s_t = lax.dot_general(keys_ref[half], qs, _NT, preferred_element_type=F32)
        st_ref[0, half * n_keys:(half + 1) * n_keys, :] = s_t
        _extract_top(s_t, v_ref, topk)
        halves.append(s_t)
    v1, v2 = v1_ref[...], v2_ref[...]
    assert topk == 16
    slabs = [v1 + v2[0:1]]
    slabs += [v1[0:8] + v2[b:b + 1] for b in range(1, 8)]
    slabs += [v1[0:1] + v2[8:16]]
    _extract_top(jnp.concatenate(slabs, axis=0), top_ref, topk)
    top = top_ref[...]
    thr_ref[0] = top[topk - 1:topk]
    z_sum = jnp.sum(jnp.exp(top - top[0:1]), axis=0, keepdims=True)
    et_ref[0, :n_keys, :] = jnp.exp(halves[0] - v1[0:1])
    et_ref[0, n_keys:, :] = jnp.exp(halves[1] - v2[0:1]) / z_sum


def peer_select(qh, sub_keys, tt=512):
    t = qh.shape[0]
    _, n_keys, key_half = sub_keys.shape
    n_heads = qh.shape[1] // (2 * key_half)
    tt = _tile(t, tt)
    kern = functools.partial(_peer_select_kernel, n_keys=n_keys, key_half=key_half, topk=PEER_TOPK)
    return pl.pallas_call(
        kern,
        grid=(t // tt, n_heads),
        in_specs=[
            pl.BlockSpec((tt, 2 * key_half), lambda i, h: (i, h)),
            pl.BlockSpec(sub_keys.shape, lambda i, h: (0, 0, 0)),
        ],
        out_specs=[
            pl.BlockSpec((1, 2 * n_keys, tt), lambda i, h: (h, 0, i)),
            pl.BlockSpec((1, 2 * n_keys, tt), lambda i, h: (h, 0, i)),
            pl.BlockSpec((1, 1, tt), lambda i, h: (h, 0, i)),
        ],
        out_shape=[
            jax.ShapeDtypeStruct((n_heads, 2 * n_keys, t), F32),
            jax.ShapeDtypeStruct((n_heads, 2 * n_keys, t), F32),
            jax.ShapeDtypeStruct((n_heads, 1, t), F32),
        ],
        scratch_shapes=[pltpu.VMEM((PEER_TOPK, tt), F32)] * 3,
        compiler_params=_params("parallel", "arbitrary"),
        name="peer_select",
    )(qh, sub_keys)


def _gelu_tanh(x):
    return 0.5 * x * (1.0 + jnp.tanh(math.sqrt(2.0 / math.pi) * (x + 0.044715 * (x * x * x))))


def _peer_dense_kernel(h_ref, g_ref, u_ref, v_ref, st_ref, et_ref, thr_ref, o_ref, xn_ref, *, n_heads, n_keys, te):
    e = pl.program_id(1)

    @pl.when(e == 0)
    def _():
        x = h_ref[...]
        xn_ref[...] = _rms_scale(x, g_ref[...]).astype(BF16)
        o_ref[...] = x

    hidden = lax.dot_general(xn_ref[...], u_ref[...], _NT, preferred_element_type=F32)
    rows_per_step = te // n_keys
    blocks = []
    for ii in range(rows_per_step):
        i_row = e * rows_per_step + ii
        gate_t = None
        for h in range(n_heads):
            score = st_ref[h, n_keys:2 * n_keys, :] + st_ref[h, pl.ds(i_row, 1), :]
            weight = et_ref[h, n_keys:2 * n_keys, :] * et_ref[h, pl.ds(i_row, 1), :]
            g = jnp.where(score >= thr_ref[h], weight, 0.0)
            gate_t = g if gate_t is None else gate_t + g
        blocks.append(gate_t)
    gate = jnp.concatenate(blocks, axis=0).T
    gh = (gate * _gelu_tanh(hidden)).astype(BF16)
    o_ref[...] += jnp.dot(gh, v_ref[...], preferred_element_type=F32)


def peer_dense(h, g, u, v, scores_t, factors_t, thr, tt=512, te=512):
    t, d = h.shape
    n_exp = u.shape[0]
    n_heads, two_keys, _ = scores_t.shape
    n_keys = two_keys // 2
    tt, te = _tile(t, tt), _tile(n_exp, te)
    assert te % n_keys == 0 and n_exp == n_keys * n_keys
    kern = functools.partial(_peer_dense_kernel, n_heads=n_heads, n_keys=n_keys, te=te)
    return pl.pallas_call(
        kern,
        grid=(t // tt, n_exp // te),
        in_specs=[
            pl.BlockSpec((tt, d), lambda i, e: (i, 0)),
            pl.BlockSpec((1, d), lambda i, e: (0, 0)),
            pl.BlockSpec((te, d), lambda i, e: (e, 0)),
            pl.BlockSpec((te, d), lambda i, e: (e, 0)),
            pl.BlockSpec((n_heads, two_keys, tt), lambda i, e: (0, 0, i)),
            pl.BlockSpec((n_heads, two_keys, tt), lambda i, e: (0, 0, i)),
            pl.BlockSpec((n_heads, 1, tt), lambda i, e: (0, 0, i)),
        ],
        out_specs=pl.BlockSpec((tt, d), lambda i, e: (i, 0)),
        out_shape=jax.ShapeDtypeStruct((t, d), F32),
        scratch_shapes=[pltpu.VMEM((tt, d), BF16)],
        compiler_params=_params("parallel", "arbitrary"),
        name="peer_dense",
    )(h, g.reshape(1, d), u, v, scores_t, factors_t, thr)


def _ple_kernel(x_ref, g_ref, wg_ref, p_ref, wp_ref, r_ref, o_ref, xn_ref, pb_ref):
    @pl.when(pl.program_id(1) == 0)
    def _():
        xn_ref[...] = _rms_scale(x_ref[...], g_ref[...]).astype(BF16)
        pb_ref[...] = p_ref[...].astype(BF16)

    gate = 1.0 / (1.0 + jnp.exp(-jnp.dot(xn_ref[...], wg_ref[...], preferred_element_type=F32)))
    o_ref[...] = r_ref[...] + jnp.dot(pb_ref[...], wp_ref[...], preferred_element_type=F32) * gate


def per_layer_embedding(h, p, g, w_g, w_p, tm=512, tn=1024):
    t, d = h.shape
    pd = p.shape[1]
    tm, tn = _tile(t, tm), _tile(d, tn)
    return pl.pallas_call(
        _ple_kernel,
        grid=(t // tm, d // tn),
        in_specs=[
            pl.BlockSpec((tm, d), lambda i, j: (i, 0)),
            pl.BlockSpec((1, d), lambda i, j: (0, 0)),
            pl.BlockSpec((d, tn), lambda i, j: (0, j)),
            pl.BlockSpec((tm, pd), lambda i, j: (i, 0)),
            pl.BlockSpec((pd, tn), lambda i, j: (0, j)),
            pl.BlockSpec((tm, tn), lambda i, j: (i, j)),
        ],
        out_specs=pl.BlockSpec((tm, tn), lambda i, j: (i, j)),
        out_shape=jax.ShapeDtypeStruct((t, d), F32),
        scratch_shapes=[pltpu.VMEM((tm, d), BF16), pltpu.VMEM((tm, pd), BF16)],
        compiler_params=_params("parallel", "arbitrary"),
        name="per_layer_embedding",
    )(h, g.reshape(1, d), w_g, p, w_p, h)


def _pad_cols(w, multiple):
    extra = (-w.shape[-1]) % multiple
    return jnp.pad(w, ((0, 0), (0, extra))) if extra else w


def kernel(x, p, attn_norm, ffn_norm, ple_norm, final_norm, sb_w_qkv, sb_w_o, swa_w_qkv, swa_w_o, swa_sinks,
           mla_w_down, mla_q_norm, mla_kv_norm, mla_w_uq, mla_w_ukv, mla_w_o,
           peer_w_q, peer_sub_keys, peer_u, peer_v, ple_w_p, ple_w_g):
    b, s, d = x.shape
    t = b * s
    depth = p.shape[0]
    h = x.reshape(t, d)
    for i in range(depth):
        kind, j = i % N_MIXERS, i // N_MIXERS
        if kind == 0:
            qkv = norm_matmul(h, attn_norm[i], sb_w_qkv[j].astype(BF16), BF16)
            o = sb_attention(qkv.reshape(b, s, -1), SB_HEADS)
            w_o = sb_w_o[j]
        elif kind == 1:
            qkv = norm_matmul(h, attn_norm[i], swa_w_qkv[j].astype(BF16), BF16, tn=1280)
            head_dim = swa_w_o.shape[1] // SWA_HEADS
            o = swa_attention(qkv.reshape(b, s, -1), swa_sinks[j], SWA_HEADS, SWA_KV_HEADS, head_dim)
            w_o = swa_w_o[j]
        else:
            down = norm_matmul(h, attn_norm[i], _pad_cols(mla_w_down[j], LANES).astype(BF16), F32, tn=2048)
            qk_dim = MLA_NOPE_DIM + MLA_ROPE_DIM
            w_uq_pad = jnp.pad(mla_w_uq[j].reshape(MLA_Q_RANK, MLA_HEADS, qk_dim),
                               ((0, 0), (0, 0), (0, 2 * LANES - qk_dim))).reshape(MLA_Q_RANK, -1)
            w_ukv = mla_w_ukv[j].reshape(MLA_KV_RANK, MLA_HEADS, 2, LANES)
            w_kn = w_ukv[:, :, 0].reshape(MLA_KV_RANK, -1).astype(BF16)
            w_vt = w_ukv[:, :, 1].reshape(MLA_KV_RANK, -1).T.astype(BF16)
            q, k_nope, v_t, k_rope = mla_projections(down, mla_q_norm[j], mla_kv_norm[j], w_uq_pad.astype(BF16),
                                                     w_kn, w_vt, s)
            o = mla_attention(q.reshape(b, s, -1), k_nope.reshape(b, s, -1), k_rope.reshape(b, s, -1), v_t, MLA_HEADS)
            w_o = mla_w_o[j]
        h = matmul_residual(o.reshape(t, -1), w_o.astype(BF16), h)
        qh = norm_matmul(h, ffn_norm[i], peer_w_q[i].astype(BF16), BF16)
        scores_t, factors_t, thr = peer_select(qh, peer_sub_keys[i].astype(BF16))
        h = peer_dense(h, ffn_norm[i], peer_u[i].astype(BF16), peer_v[i].astype(BF16), scores_t, factors_t, thr)
        h = per_layer_embedding(h, p[i].reshape(t, -1), ple_norm[i], ple_w_g[i].astype(BF16), ple_w_p[i].astype(BF16))
    return final_rms_norm(h, final_norm).reshape(b, s, d)
```

```python
import functools
import math

import jax
import jax.numpy as jnp
from jax import lax
from jax.experimental import pallas as pl
from jax.experimental.pallas import tpu as pltpu

F32 = jnp.float32
BF16 = jnp.bfloat16

RMS_EPS = 1e-6
N_MIXERS = 3
SB_HEADS = 16
SWA_HEADS = 32
SWA_KV_HEADS = 4
SWA_WINDOW = 128
MLA_HEADS = 16
MLA_Q_RANK = 512
MLA_KV_RANK = 512
MLA_NOPE_DIM = 128
MLA_ROPE_DIM = 64
MLA_V_DIM = 128
ROPE_THETA = 10000.0
PEER_N_KEYS = 64
PEER_TOPK = 16

EXP2_UNDERFLOW = -150.0
LANES = 128
V7X_VMEM_BYTES = 64 * 1024 * 1024
VMEM_LIMIT = V7X_VMEM_BYTES * 3 // 4

_NT = (((1,), (1,)), ((), ()))


def _params(*semantics):
    return pltpu.CompilerParams(dimension_semantics=semantics, vmem_limit_bytes=VMEM_LIMIT)


def _tile(n, want):
    if n <= want:
        return n
    t = (want // LANES) * LANES
    while t > LANES and n % t:
        t -= LANES
    assert n % t == 0, (n, want)
    return t


def _rms_scale(x, g):
    ms = jnp.mean(x * x, axis=-1, keepdims=True)
    return x * lax.rsqrt(ms + RMS_EPS) * g


def _norm_matmul_kernel(x_ref, g_ref, w_ref, o_ref, xn_ref):
    @pl.when(pl.program_id(1) == 0)
    def _():
        xn_ref[...] = _rms_scale(x_ref[...], g_ref[...]).astype(BF16)

    o_ref[...] = jnp.dot(xn_ref[...], w_ref[...], preferred_element_type=F32).astype(o_ref.dtype)


def norm_matmul(x, g, w, out_dtype, tm=512, tn=1024):
    m, k = x.shape
    n = w.shape[1]
    tm, tn = _tile(m, tm), _tile(n, tn)
    return pl.pallas_call(
        _norm_matmul_kernel,
        grid=(m // tm, n // tn),
        in_specs=[
            pl.BlockSpec((tm, k), lambda i, j: (i, 0)),
            pl.BlockSpec((1, k), lambda i, j: (0, 0)),
            pl.BlockSpec((k, tn), lambda i, j: (0, j)),
        ],
        out_specs=pl.BlockSpec((tm, tn), lambda i, j: (i, j)),
        out_shape=jax.ShapeDtypeStruct((m, n), out_dtype),
        scratch_shapes=[pltpu.VMEM((tm, k), BF16)],
        compiler_params=_params("parallel", "arbitrary"),
        name="norm_matmul",
    )(x, g.reshape(1, k), w)


def _matmul_residual_kernel(a_ref, w_ref, r_ref, o_ref):
    o_ref[...] = r_ref[...] + jnp.dot(a_ref[...], w_ref[...], preferred_element_type=F32)


def matmul_residual(a, w, res, tm=512, tn=1024):
    m, k = a.shape
    n = w.shape[1]
    tm, tn = _tile(m, tm), _tile(n, tn)
    return pl.pallas_call(
        _matmul_residual_kernel,
        grid=(m // tm, n // tn),
        in_specs=[
            pl.BlockSpec((tm, k), lambda i, j: (i, 0)),
            pl.BlockSpec((k, tn), lambda i, j: (0, j)),
            pl.BlockSpec((tm, tn), lambda i, j: (i, j)),
        ],
        out_specs=pl.BlockSpec((tm, tn), lambda i, j: (i, j)),
        out_shape=jax.ShapeDtypeStruct((m, n), F32),
        compiler_params=_params("parallel", "arbitrary"),
        name="matmul_residual",
    )(a, w, res)


def _final_norm_kernel(x_ref, g_ref, o_ref):
    o_ref[...] = _rms_scale(x_ref[...], g_ref[...])


def final_rms_norm(x, g, tm=512):
    m, k = x.shape
    tm = _tile(m, tm)
    return pl.pallas_call(
        _final_norm_kernel,
        grid=(m // tm,),
        in_specs=[pl.BlockSpec((tm, k), lambda i: (i, 0)), pl.BlockSpec((1, k), lambda i: (0, 0))],
        out_specs=pl.BlockSpec((tm, k), lambda i: (i, 0)),
        out_shape=jax.ShapeDtypeStruct((m, k), F32),
        compiler_params=_params("parallel"),
        name="final_norm",
    )(x, g.reshape(1, k))


def _sb_attn_kernel(q_ref, k_ref, v_ref, o_ref, kmax2_ref, *, bq, bk, d, heads, scale):
    i = pl.program_id(2)
    n_diag = bq // bk
    row = lax.broadcasted_iota(jnp.int32, (bk, bk), 0)
    col = lax.broadcasted_iota(jnp.int32, (bk, bk), 1)
    suffix_ones = (row >= col).astype(BF16)
    q_idx = lax.broadcasted_iota(jnp.int32, (bq, bk), 0)
    k_idx = lax.broadcasted_iota(jnp.int32, (bq, bk), 1)

    def scores(hh, m):
        lanes = slice(hh * d, (hh + 1) * d)
        kb = k_ref[0, pl.ds(pl.multiple_of(m * bk, bk), bk), lanes]
        return lax.dot_general(q_ref[0, :, lanes], kb, _NT, preferred_element_type=F32)

    def absorb(hh, m, qk, acc, log_run, key_offset):
        vb = v_ref[0, pl.ds(pl.multiple_of(m * bk, bk), bk), hh * d:(hh + 1) * d]
        z = qk * scale
        neg_z = -z
        log_keep = jnp.minimum(neg_z, 0.0) - jnp.log2(1.0 + jnp.exp2(jnp.minimum(z, neg_z)))
        if key_offset is not None:
            strict = k_idx + key_offset < q_idx
            log_keep = jnp.where(strict, log_keep, 0.0)
        incl = jnp.dot(log_keep.astype(BF16), suffix_ones, preferred_element_type=F32)
        logw = z + incl + log_run
        if key_offset is not None:
            logw = jnp.where(strict, logw, -jnp.inf)
        p = jnp.exp2(logw)
        acc = acc + jnp.dot(p.astype(BF16), vb, preferred_element_type=F32)
        return acc, log_run + incl[:, 0:1]

    def step(m, state, key_offset):
        return tuple(absorb(hh, m, scores(hh, m), *state[hh], key_offset) for hh in range(heads))

    @pl.when(i == 0)
    def _():
        for hh in range(heads):
            kf = k_ref[0, :, hh * d:(hh + 1) * d].astype(F32)
            k_norm2 = jnp.max(jnp.sum(kf * kf, axis=-1, keepdims=True), axis=0, keepdims=True)
            kmax2_ref[hh] = jnp.broadcast_to(k_norm2, kmax2_ref.shape[1:])

    logit_bound = []
    for hh in range(heads):
        qf = q_ref[0, :, hh * d:(hh + 1) * d].astype(F32)
        q_norm2 = jnp.sum(qf * qf, axis=-1, keepdims=True)
        logit_bound.append(jnp.sqrt(q_norm2 * kmax2_ref[hh][0:1, 0:1]) * (scale * 1.01) + 1.0)

    def any_weight_left(state):
        worst = [jnp.max(state[hh][1] + logit_bound[hh]) for hh in range(heads)]
        return (functools.reduce(jnp.maximum, worst) >= EXP2_UNDERFLOW).astype(jnp.int32)

    state = tuple((jnp.zeros((bq, d), F32), jnp.zeros((bq, 1), F32)) for _ in range(heads))
    for dd in reversed(range(n_diag)):
        state = step(i * n_diag + dd, state, dd * bk)
    n_before = i * n_diag

    def more(c):
        t, go, _ = c
        return jnp.logical_and(t < n_before, go > 0)

    def walk(c):
        t, _, state = c
        state = step(n_before - 1 - t, state, None)
        return t + 1, any_weight_left(state), state

    _, _, state = lax.while_loop(more, walk, (jnp.int32(0), any_weight_left(state), state))
    for hh in range(heads):
        o_ref[0, :, hh * d:(hh + 1) * d] = state[hh][0].astype(o_ref.dtype)


def sb_attention(qkv, n_heads, bq=512, bk=256, heads_per_step=2):
    b, s, three_hd = qkv.shape
    d = three_hd // (3 * n_heads)
    assert d == LANES
    bq = min(bq, s)
    bk = min(bk, bq)
    hp = heads_per_step
    assert s % bq == 0 and bq % bk == 0 and n_heads % hp == 0
    n_grp = n_heads // hp
    kern = functools.partial(_sb_attn_kernel, bq=bq, bk=bk, d=d, heads=hp, scale=d ** -0.5 * math.log2(math.e))
    return pl.pallas_call(
        kern,
        grid=(b, n_grp, s // bq),
        in_specs=[
            pl.BlockSpec((1, bq, hp * d), lambda bi, h, i: (bi, i, h)),
            pl.BlockSpec((1, s, hp * d), lambda bi, h, i: (bi, 0, n_grp + h)),
            pl.BlockSpec((1, s, hp * d), lambda bi, h, i: (bi, 0, 2 * n_grp + h)),
        ],
        out_specs=pl.BlockSpec((1, bq, hp * d), lambda bi, h, i: (bi, i, h)),
        out_shape=jax.ShapeDtypeStruct((b, s, n_heads * d), BF16),
        scratch_shapes=[pltpu.VMEM((hp, 8, LANES), F32)],
        compiler_params=_params("parallel", "parallel", "arbitrary"),
        name="sb_attention",
    )(qkv, qkv, qkv)


def _swa_kernel(q_ref, kp_ref, kc_ref, vp_ref, vc_ref, slope_ref, sink_ref, o_ref, *, n_kv, grp, d, blk, scale):
    i = pl.program_id(1)
    rows = grp * blk
    qpos = lax.broadcasted_iota(jnp.int32, (rows, 2 * blk), 0) & (blk - 1)
    kcol = lax.broadcasted_iota(jnp.int32, (rows, 2 * blk), 1)
    dist = qpos + blk - kcol
    first_key = jnp.where(i > 0, 0, blk)
    in_window = (dist >= 0) & (dist < SWA_WINDOW) & (kcol >= first_key)
    dist_f = dist.astype(F32)
    outs = []
    for kh in range(n_kv):
        qg = jnp.concatenate(
            [q_ref[0, :, (kh * grp + g) * d:(kh * grp + g + 1) * d] for g in range(grp)], axis=0)
        kk = jnp.concatenate([kp_ref[0, :, kh * d:(kh + 1) * d], kc_ref[0, :, kh * d:(kh + 1) * d]], axis=0)
        vv = jnp.concatenate([vp_ref[0, :, kh * d:(kh + 1) * d], vc_ref[0, :, kh * d:(kh + 1) * d]], axis=0)
        z = lax.dot_general(qg, kk, _NT, preferred_element_type=F32) * scale
        z = z - slope_ref[kh] * dist_f
        z = jnp.where(in_window, z, -jnp.inf)
        sink = sink_ref[kh]
        mx = jnp.maximum(jnp.max(z, axis=-1, keepdims=True), sink)
        p = jnp.exp(z - mx)
        den = jnp.sum(p, axis=-1, keepdims=True) + jnp.exp(sink - mx)
        o = jnp.dot(p.astype(BF16), vv, preferred_element_type=F32) / den
        outs.extend(o[g * blk:(g + 1) * blk] for g in range(grp))
    o_ref[0] = jnp.concatenate(outs, axis=1).astype(o_ref.dtype)


def swa_attention(qkv, sinks, n_heads, n_kv, d):
    b, s, _ = qkv.shape
    blk = SWA_WINDOW
    grp = n_heads // n_kv
    qd, kd = n_heads * d, n_kv * d
    assert qd % kd == 0 and kd % LANES == 0 and s % blk == 0
    slopes = 2.0 ** (-8.0 * jnp.arange(1, n_heads + 1, dtype=F32) / n_heads)
    per_row = lambda a: jnp.repeat(a.astype(F32).reshape(n_kv, grp), blk, axis=1).reshape(n_kv, grp * blk, 1)
    kern = functools.partial(_swa_kernel, n_kv=n_kv, grp=grp, d=d, blk=blk, scale=d ** -0.5)
    prev = lambda bi, i: (bi, jnp.maximum(i - 1, 0), qd // kd)
    cur = lambda bi, i: (bi, i, qd // kd)
    prev_v = lambda bi, i: (bi, jnp.maximum(i - 1, 0), qd // kd + 1)
    cur_v = lambda bi, i: (bi, i, qd // kd + 1)
    const = lambda bi, i: (0, 0, 0)
    return pl.pallas_call(
        kern,
        grid=(b, s // blk),
        in_specs=[
            pl.BlockSpec((1, blk, qd), lambda bi, i: (bi, i, 0)),
            pl.BlockSpec((1, blk, kd), prev),
            pl.BlockSpec((1, blk, kd), cur),
            pl.BlockSpec((1, blk, kd), prev_v),
            pl.BlockSpec((1, blk, kd), cur_v),
            pl.BlockSpec((n_kv, grp * blk, 1), const),
            pl.BlockSpec((n_kv, grp * blk, 1), const),
        ],
        out_specs=pl.BlockSpec((1, blk, qd), lambda bi, i: (bi, i, 0)),
        out_shape=jax.ShapeDtypeStruct((b, s, qd), BF16),
        compiler_params=_params("parallel", "arbitrary"),
        name="swa_attention",
    )(qkv, qkv, qkv, qkv, qkv, per_row(slopes), per_row(sinks))


def _rope_lanes(r, cos_ref, sin_lo_ref, sin_hi_ref):
    half = MLA_ROPE_DIM // 2
    return (r * cos_ref[...]
            + pltpu.roll(r, LANES - half, 1) * sin_lo_ref[...]
            + pltpu.roll(r, half, 1) * sin_hi_ref[...])


def _mla_q_kernel(x_ref, g_ref, w_ref, cos_ref, slo_ref, shi_ref, o_ref, *, n_heads):
    xn = _rms_scale(x_ref[...], g_ref[...]).astype(BF16)
    for h in range(n_heads):
        a = jnp.dot(xn, w_ref[:, 2 * LANES * h:2 * LANES * (h + 1)], preferred_element_type=F32)
        o_ref[:, 2 * LANES * h:2 * LANES * h + LANES] = a[:, :LANES].astype(BF16)
        o_ref[:, 2 * LANES * h + LANES:2 * LANES * (h + 1)] = _rope_lanes(
            a[:, LANES:], cos_ref, slo_ref, shi_ref).astype(BF16)


def _mla_kv_kernel(x_ref, g_ref, xr_ref, wk_ref, wvt_ref, cos_ref, slo_ref, shi_ref, kn_ref, vt_ref, kr_ref, *, chunk):
    xn = _rms_scale(x_ref[...], g_ref[...]).astype(BF16)
    for c in range(wk_ref.shape[1] // chunk):
        cols = slice(c * chunk, (c + 1) * chunk)
        kn_ref[:, cols] = jnp.dot(xn, wk_ref[:, cols], preferred_element_type=F32).astype(BF16)
        vt_ref[cols, :] = lax.dot_general(wvt_ref[cols, :], xn, _NT, preferred_element_type=F32).astype(BF16)
    kr_ref[...] = _rope_lanes(xr_ref[...], cos_ref, slo_ref, shi_ref).astype(BF16)


def _rope_tables(s):
    half = MLA_ROPE_DIM // 2
    inv_freq = ROPE_THETA ** (-jnp.arange(half, dtype=F32) / half)
    ang = jnp.arange(s, dtype=F32)[:, None] * inv_freq[None, :]
    cos, sin = jnp.cos(ang), jnp.sin(ang)
    zeros = jnp.zeros((s, LANES - 2 * half), F32)
    z_half = jnp.zeros((s, half), F32)
    cos_t = jnp.concatenate([cos, cos, zeros], axis=1)
    sin_lo = jnp.concatenate([-sin, z_half, zeros], axis=1)
    sin_hi = jnp.concatenate([z_half, sin, zeros], axis=1)
    return cos_t, sin_lo, sin_hi


def mla_projections(down, g_q, g_kv, w_uq_pad, w_kn, w_vt, seq, tm=512):
    t = down.shape[0]
    tm = _tile(seq, tm)
    n_heads = w_uq_pad.shape[1] // (2 * LANES)
    tables = _rope_tables(seq)
    n_pos = seq // tm
    tab_spec = pl.BlockSpec((tm, LANES), lambda i: (i % n_pos, 0))
    q = pl.pallas_call(
        functools.partial(_mla_q_kernel, n_heads=n_heads),
        grid=(t // tm,),
        in_specs=[
            pl.BlockSpec((tm, MLA_Q_RANK), lambda i: (i, 0)),
            pl.BlockSpec((1, MLA_Q_RANK), lambda i: (0, 0)),
            pl.BlockSpec(w_uq_pad.shape, lambda i: (0, 0)),
            tab_spec, tab_spec, tab_spec,
        ],
        out_specs=pl.BlockSpec((tm, w_uq_pad.shape[1]), lambda i: (i, 0)),
        out_shape=jax.ShapeDtypeStruct((t, w_uq_pad.shape[1]), BF16),
        compiler_params=_params("parallel"),
        name="mla_q_proj",
    )(down, g_q.reshape(1, -1), w_uq_pad, *tables)
    assert MLA_Q_RANK == MLA_KV_RANK and (MLA_Q_RANK + MLA_KV_RANK) % LANES == 0
    hv = w_kn.shape[1]
    k_nope, v_t, k_rope = pl.pallas_call(
        functools.partial(_mla_kv_kernel, chunk=_tile(hv, 512)),
        grid=(t // tm,),
        in_specs=[
            pl.BlockSpec((tm, MLA_KV_RANK), lambda i: (i, 1)),
            pl.BlockSpec((1, MLA_KV_RANK), lambda i: (0, 0)),
            pl.BlockSpec((tm, LANES), lambda i: (i, (MLA_Q_RANK + MLA_KV_RANK) // LANES)),
            pl.BlockSpec(w_kn.shape, lambda i: (0, 0)),
            pl.BlockSpec(w_vt.shape, lambda i: (0, 0)),
            tab_spec, tab_spec, tab_spec,
        ],
        out_specs=[pl.BlockSpec((tm, hv), lambda i: (i, 0)),
                   pl.BlockSpec((hv, tm), lambda i: (0, i)),
                   pl.BlockSpec((tm, LANES), lambda i: (i, 0))],
        out_shape=[jax.ShapeDtypeStruct((t, hv), BF16), jax.ShapeDtypeStruct((hv, t), BF16),
                   jax.ShapeDtypeStruct((t, LANES), BF16)],
        compiler_params=_params("parallel"),
        name="mla_kv_proj",
    )(down, g_kv.reshape(1, -1), down, w_kn, w_vt, *tables)
    return q, k_nope, v_t, k_rope


def _mla_attn_kernel(q_ref, kn_ref, kr_ref, vt_ref, o_ref, k_ref, *, blk, heads, scale):
    i = pl.program_id(2)
    width = 2 * LANES

    @pl.when(i == 0)
    def _():
        for hh in range(heads):
            k_ref[hh, :, :LANES] = kn_ref[0, :, hh * LANES:(hh + 1) * LANES]
            k_ref[hh, :, LANES:] = kr_ref[0]

    def scores(hh, m):
        kb = k_ref[hh, pl.ds(pl.multiple_of(m * blk, blk), blk), :]
        return lax.dot_general(kb, q_ref[0, :, hh * width:(hh + 1) * width], _NT, preferred_element_type=F32)

    def absorb(hh, m, z_t, acc_t, mx, den, diagonal):
        v_t = vt_ref[hh * LANES:(hh + 1) * LANES, pl.ds(pl.multiple_of(m * blk, blk), blk)]
        z_t = z_t * scale
        if diagonal:
            key = lax.broadcasted_iota(jnp.int32, (blk, blk), 0)
            qry = lax.broadcasted_iota(jnp.int32, (blk, blk), 1)
            z_t = jnp.where(key <= qry, z_t, -jnp.inf)
        new_mx = jnp.maximum(mx, jnp.max(z_t, axis=0, keepdims=True))
        corr = jnp.exp(mx - new_mx)
        p_t = jnp.exp(z_t - new_mx)
        den = den * corr + jnp.sum(p_t, axis=0, keepdims=True)
        acc_t = acc_t * corr + jnp.dot(v_t, p_t.astype(BF16), preferred_element_type=F32)
        return acc_t, new_mx, den

    def body(m, carry):
        z_cur, state = carry
        z_next = tuple(scores(hh, m + 1) for hh in range(heads))
        state = tuple(absorb(hh, m, z_cur[hh], *state[hh], False) for hh in range(heads))
        return z_next, state

    init = tuple((jnp.zeros((LANES, blk), F32), jnp.full((1, blk), -jnp.inf, F32), jnp.zeros((1, blk), F32))
                 for _ in range(heads))
    z_diag, state = lax.fori_loop(0, i, body, (tuple(scores(hh, 0) for hh in range(heads)), init))
    for hh in range(heads):
        acc_t, _, den = absorb(hh, i, z_diag[hh], *state[hh], True)
        o_ref[0, :, hh * LANES:(hh + 1) * LANES] = (acc_t / den).T.astype(o_ref.dtype)


def mla_attention(q, k_nope, k_rope, v_t, n_heads, blk=512, heads_per_step=2):
    b, s, _ = q.shape
    blk = min(blk, s)
    hp = heads_per_step
    assert s % blk == 0 and n_heads % hp == 0 and MLA_V_DIM == LANES
    scale = (MLA_NOPE_DIM + MLA_ROPE_DIM) ** -0.5
    bq = blk
    return pl.pallas_call(
        functools.partial(_mla_attn_kernel, blk=blk, heads=hp, scale=scale),
        grid=(b, n_heads // hp, s // blk),
        in_specs=[
            pl.BlockSpec((1, bq, hp * 2 * LANES), lambda bi, h, i: (bi, i, h)),
            pl.BlockSpec((1, s, hp * LANES), lambda bi, h, i: (bi, 0, h)),
            pl.BlockSpec((1, s, LANES), lambda bi, h, i: (bi, 0, 0)),
            pl.BlockSpec((hp * LANES, s), lambda bi, h, i: (h, bi)),
        ],
        out_specs=pl.BlockSpec((1, bq, hp * LANES), lambda bi, h, i: (bi, i, h)),
        out_shape=jax.ShapeDtypeStruct((b, s, n_heads * MLA_V_DIM), BF16),
        scratch_shapes=[pltpu.VMEM((hp, s, 2 * LANES), BF16)],
        compiler_params=_params("parallel", "parallel", "arbitrary"),
        name="mla_attention",
    )(q, k_nope, k_rope, v_t)


def _extract_top(x, out_ref, k):
    n_rows = x.shape[0]

    def drop_all(r, y):
        m = jnp.max(y, axis=0, keepdims=True)
        out_ref[pl.ds(r, 1), :] = m
        return jnp.where(y == m, -jnp.inf, y)

    lax.fori_loop(0, k, drop_all, x)
    at_least_kth = jnp.sum((x >= out_ref[k - 1:k, :]).astype(F32), axis=0, keepdims=True)

    @pl.when(jnp.max(at_least_kth) > k)
    def _():
        rows = lax.broadcasted_iota(jnp.int32, x.shape, 0).astype(F32)

        def drop_first(r, y):
            m = jnp.max(y, axis=0, keepdims=True)
            out_ref[pl.ds(r, 1), :] = m
            first = jnp.min(jnp.where(y == m, rows, float(n_rows)), axis=0, keepdims=True)
            return jnp.where(rows == first, -jnp.inf, y)

        lax.fori_loop(0, k, drop_first, x)


def _peer_select_kernel(qh_ref, keys_ref, st_ref, et_ref, thr_ref, v1_ref, v2_ref, top_ref, *, n_keys, key_half, topk):
    halves = []
    for half, v_ref in ((0, v1_ref), (1, v2_ref)):
        qs = qh_ref[:, half * key_half:(half + 1) * key_half]
        s_t = lax.dot_general(keys_ref[half], qs, _NT, preferred_element_type=F32)
        st_ref[0, half * n_keys:(half + 1) * n_keys, :] = s_t
        _extract_top(s_t, v_ref, topk)
        halves.append(s_t)
    v1, v2 = v1_ref[...], v2_ref[...]
    assert topk == 16
    slabs = [v1 + v2[0:1]]
    slabs += [v1[0:8] + v2[b:b + 1] for b in range(1, 8)]
    slabs += [v1[0:1] + v2[8:16]]
    _extract_top(jnp.concatenate(slabs, axis=0), top_ref, topk)
    top = top_ref[...]
    thr_ref[0] = top[topk - 1:topk]
    z_sum = jnp.sum(jnp.exp(top - top[0:1]), axis=0, keepdims=True)
    et_ref[0, :n_keys, :] = jnp.exp(halves[0] - v1[0:1])
    et_ref[0, n_keys:, :] = jnp.exp(halves[1] - v2[0:1]) / z_sum


def peer_select(qh, sub_keys, tt=512):
    t = qh.shape[0]
    _, n_keys, key_half = sub_keys.shape
    n_heads = qh.shape[1] // (2 * key_half)
    tt = _tile(t, tt)
    kern = functools.partial(_peer_select_kernel, n_keys=n_keys, key_half=key_half, topk=PEER_TOPK)
    return pl.pallas_call(
        kern,
        grid=(t // tt, n_heads),
        in_specs=[
            pl.BlockSpec((tt, 2 * key_half), lambda i, h: (i, h)),
            pl.BlockSpec(sub_keys.shape, lambda i, h: (0, 0, 0)),
        ],
        out_specs=[
            pl.BlockSpec((1, 2 * n_keys, tt), lambda i, h: (h, 0, i)),
            pl.BlockSpec((1, 2 * n_keys, tt), lambda i, h: (h, 0, i)),
            pl.BlockSpec((1, 1, tt), lambda i, h: (h, 0, i)),
        ],
        out_shape=[
            jax.ShapeDtypeStruct((n_heads, 2 * n_keys, t), F32),
            jax.ShapeDtypeStruct((n_heads, 2 * n_keys, t), F32),
            jax.ShapeDtypeStruct((n_heads, 1, t), F32),
        ],
        scratch_shapes=[pltpu.VMEM((PEER_TOPK, tt), F32)] * 3,
        compiler_params=_params("parallel", "arbitrary"),
        name="peer_select",
    )(qh, sub_keys)


def _gelu_tanh(x):
    return 0.5 * x * (1.0 + jnp.tanh(math.sqrt(2.0 / math.pi) * (x + 0.044715 * (x * x * x))))


def _peer_dense_kernel(h_ref, g_ref, u_ref, v_ref, st_ref, et_ref, thr_ref, o_ref, xn_ref, *, n_heads, n_keys, te):
    e = pl.program_id(1)

    @pl.when(e == 0)
    def _():
        x = h_ref[...]
        xn_ref[...] = _rms_scale(x, g_ref[...]).astype(BF16)
        o_ref[...] = x

    hidden = lax.dot_general(xn_ref[...], u_ref[...], _NT, preferred_element_type=F32)
    rows_per_step = te // n_keys
    blocks = []
    for ii in range(rows_per_step):
        i_row = e * rows_per_step + ii
        gate_t = None
        for h in range(n_heads):
            score = st_ref[h, n_keys:2 * n_keys, :] + st_ref[h, pl.ds(i_row, 1), :]
            weight = et_ref[h, n_keys:2 * n_keys, :] * et_ref[h, pl.ds(i_row, 1), :]
            g = jnp.where(score >= thr_ref[h], weight, 0.0)
            gate_t = g if gate_t is None else gate_t + g
        blocks.append(gate_t)
    gate = jnp.concatenate(blocks, axis=0).T
    gh = (gate * _gelu_tanh(hidden)).astype(BF16)
    o_ref[...] += jnp.dot(gh, v_ref[...], preferred_element_type=F32)


def peer_dense(h, g, u, v, scores_t, factors_t, thr, tt=512, te=512):
    t, d = h.shape
    n_exp = u.shape[0]
    n_heads, two_keys, _ = scores_t.shape
    n_keys = two_keys // 2
    tt, te = _tile(t, tt), _tile(n_exp, te)
    assert te % n_keys == 0 and n_exp == n_keys * n_keys
    kern = functools.partial(_peer_dense_kernel, n_heads=n_heads, n_keys=n_keys, te=te)
    return pl.pallas_call(
        kern,
        grid=(t // tt, n_exp // te),
        in_specs=[
            pl.BlockSpec((tt, d), lambda i, e: (i, 0)),
            pl.BlockSpec((1, d), lambda i, e: (0, 0)),
            pl.BlockSpec((te, d), lambda i, e: (e, 0)),
            pl.BlockSpec((te, d), lambda i, e: (e, 0)),
            pl.BlockSpec((n_heads, two_keys, tt), lambda i, e: (0, 0, i)),
            pl.BlockSpec((n_heads, two_keys, tt), lambda i, e: (0, 0, i)),
            pl.BlockSpec((n_heads, 1, tt), lambda i, e: (0, 0, i)),
        ],
        out_specs=pl.BlockSpec((tt, d), lambda i, e: (i, 0)),
        out_shape=jax.ShapeDtypeStruct((t, d), F32),
        scratch_shapes=[pltpu.VMEM((tt, d), BF16)],
        compiler_params=_params("parallel", "arbitrary"),
        name="peer_dense",
    )(h, g.reshape(1, d), u, v, scores_t, factors_t, thr)


def _ple_kernel(x_ref, g_ref, wg_ref, p_ref, wp_ref, r_ref, o_ref, xn_ref, pb_ref):
    @pl.when(pl.program_id(1) == 0)
    def _():
        xn_ref[...] = _rms_scale(x_ref[...], g_ref[...]).astype(BF16)
        pb_ref[...] = p_ref[...].astype(BF16)

    gate = 1.0 / (1.0 + jnp.exp(-jnp.dot(xn_ref[...], wg_ref[...], preferred_element_type=F32)))
    o_ref[...] = r_ref[...] + jnp.dot(pb_ref[...], wp_ref[...], preferred_element_type=F32) * gate


def per_layer_embedding(h, p, g, w_g, w_p, tm=512, tn=1024):
    t, d = h.shape
    pd = p.shape[1]
    tm, tn = _tile(t, tm), _tile(d, tn)
    return pl.pallas_call(
        _ple_kernel,
        grid=(t // tm, d // tn),
        in_specs=[
            pl.BlockSpec((tm, d), lambda i, j: (i, 0)),
            pl.BlockSpec((1, d), lambda i, j: (0, 0)),
            pl.BlockSpec((d, tn), lambda i, j: (0, j)),
            pl.BlockSpec((tm, pd), lambda i, j: (i, 0)),
            pl.BlockSpec((pd, tn), lambda i, j: (0, j)),
            pl.BlockSpec((tm, tn), lambda i, j: (i, j)),
        ],
        out_specs=pl.BlockSpec((tm, tn), lambda i, j: (i, j)),
        out_shape=jax.ShapeDtypeStruct((t, d), F32),
        scratch_shapes=[pltpu.VMEM((tm, d), BF16), pltpu.VMEM((tm, pd), BF16)],
        compiler_params=_params("parallel", "arbitrary"),
        name="per_layer_embedding",
    )(h, g.reshape(1, d), w_g, p, w_p, h)


def _pad_cols(w, multiple):
    extra = (-w.shape[-1]) % multiple
    return jnp.pad(w, ((0, 0), (0, extra))) if extra else w


def kernel(x, p, attn_norm, ffn_norm, ple_norm, final_norm, sb_w_qkv, sb_w_o, swa_w_qkv, swa_w_o, swa_sinks,
           mla_w_down, mla_q_norm, mla_kv_norm, mla_w_uq, mla_w_ukv, mla_w_o,
           peer_w_q, peer_sub_keys, peer_u, peer_v, ple_w_p, ple_w_g):
    b, s, d = x.shape
    t = b * s
    depth = p.shape[0]
    h = x.reshape(t, d)
    for i in range(depth):
        kind, j = i % N_MIXERS, i // N_MIXERS
        if kind == 0:
            qkv = norm_matmul(h, attn_norm[i], sb_w_qkv[j].astype(BF16), BF16)
            o = sb_attention(qkv.reshape(b, s, -1), SB_HEADS)
            w_o = sb_w_o[j]
        elif kind == 1:
            qkv = norm_matmul(h, attn_norm[i], swa_w_qkv[j].astype(BF16), BF16, tn=1280)
            head_dim = swa_w_o.shape[1] // SWA_HEADS
            o = swa_attention(qkv.reshape(b, s, -1), swa_sinks[j], SWA_HEADS, SWA_KV_HEADS, head_dim)
            w_o = swa_w_o[j]
        else:
            down = norm_matmul(h, attn_norm[i], _pad_cols(mla_w_down[j], LANES).astype(BF16), F32, tn=2048)
            qk_dim = MLA_NOPE_DIM + MLA_ROPE_DIM
            w_uq_pad = jnp.pad(mla_w_uq[j].reshape(MLA_Q_RANK, MLA_HEADS, qk_dim),
                               ((0, 0), (0, 0), (0, 2 * LANES - qk_dim))).reshape(MLA_Q_RANK, -1)
            w_ukv = mla_w_ukv[j].reshape(MLA_KV_RANK, MLA_HEADS, 2, LANES)
            w_kn = w_ukv[:, :, 0].reshape(MLA_KV_RANK, -1).astype(BF16)
            w_vt = w_ukv[:, :, 1].reshape(MLA_KV_RANK, -1).T.astype(BF16)
            q, k_nope, v_t, k_rope = mla_projections(down, mla_q_norm[j], mla_kv_norm[j], w_uq_pad.astype(BF16),
                                                     w_kn, w_vt, s)
            o = mla_attention(q.reshape(b, s, -1), k_nope.reshape(b, s, -1), k_rope.reshape(b, s, -1), v_t, MLA_HEADS)
            w_o = mla_w_o[j]
        h = matmul_residual(o.reshape(t, -1), w_o.astype(BF16), h)
        qh = norm_matmul(h, ffn_norm[i], peer_w_q[i].astype(BF16), BF16)
        scores_t, factors_t, thr = peer_select(qh, peer_sub_keys[i].astype(BF16))
        h = peer_dense(h, ffn_norm[i], peer_u[i].astype(BF16), peer_v[i].astype(BF16), scores_t, factors_t, thr)
        h = per_layer_embedding(h, p[i].reshape(t, -1), ple_norm[i], ple_w_g[i].astype(BF16), ple_w_p[i].astype(BF16))
    return final_rms_norm(h, final_norm).reshape(b, s, d)
```

```python
import functools
import math

import jax
import jax.numpy as jnp
from jax import lax
from jax.experimental import pallas as pl
from jax.experimental.pallas import tpu as pltpu

F32 = jnp.float32
BF16 = jnp.bfloat16

RMS_EPS = 1e-6
N_MIXERS = 3
SB_HEADS = 16
SWA_HEADS = 32
SWA_KV_HEADS = 4
SWA_WINDOW = 128
MLA_HEADS = 16
MLA_Q_RANK = 512
MLA_KV_RANK = 512
MLA_NOPE_DIM = 128
MLA_ROPE_DIM = 64
MLA_V_DIM = 128
ROPE_THETA = 10000.0
PEER_N_KEYS = 64
PEER_TOPK = 16

NORM_MARGIN = 1.02
DENOMINATOR_FLOOR = 2.0 ** -80
EXP2_UNDERFLOW = -150.0
LANES = 128
V7X_VMEM_BYTES = 64 * 1024 * 1024
VMEM_LIMIT = V7X_VMEM_BYTES * 7 // 8

_NT = (((1,), (1,)), ((), ()))


def _params(*semantics):
    return pltpu.CompilerParams(dimension_semantics=semantics, vmem_limit_bytes=VMEM_LIMIT)


def _tile(n, want):
    if n <= want:
        return n
    t = (want // LANES) * LANES
    while t > LANES and n % t:
        t -= LANES
    assert n % t == 0, (n, want)
    return t


def _rms_scale(x, g):
    ms = jnp.mean(x * x, axis=-1, keepdims=True)
    return x * lax.rsqrt(ms + RMS_EPS) * g


def _norm_matmul_kernel(x_ref, g_ref, w_ref, o_ref, xn_ref):
    @pl.when(pl.program_id(1) == 0)
    def _():
        xn_ref[...] = _rms_scale(x_ref[...], g_ref[...]).astype(BF16)

    o_ref[...] = jnp.dot(xn_ref[...], w_ref[...], preferred_element_type=F32).astype(o_ref.dtype)


def norm_matmul(x, g, w, out_dtype, tm=512, tn=1024):
    m, k = x.shape
    n = w.shape[1]
    tm, tn = _tile(m, tm), _tile(n, tn)
    return pl.pallas_call(
        _norm_matmul_kernel,
        grid=(m // tm, n // tn),
        in_specs=[
            pl.BlockSpec((tm, k), lambda i, j: (i, 0)),
            pl.BlockSpec((1, k), lambda i, j: (0, 0)),
            pl.BlockSpec((k, tn), lambda i, j: (0, j)),
        ],
        out_specs=pl.BlockSpec((tm, tn), lambda i, j: (i, j)),
        out_shape=jax.ShapeDtypeStruct((m, n), out_dtype),
        scratch_shapes=[pltpu.VMEM((tm, k), BF16)],
        compiler_params=_params("parallel", "arbitrary"),
        name="norm_matmul",
    )(x, g.reshape(1, k), w)


def _matmul_residual_kernel(a_ref, w_ref, r_ref, o_ref):
    o_ref[...] = r_ref[...] + jnp.dot(a_ref[...], w_ref[...], preferred_element_type=F32)


def matmul_residual(a, w, res, tm=512, tn=1024):
    m, k = a.shape
    n = w.shape[1]
    tm, tn = _tile(m, tm), _tile(n, tn)
    return pl.pallas_call(
        _matmul_residual_kernel,
        grid=(m // tm, n // tn),
        in_specs=[
            pl.BlockSpec((tm, k), lambda i, j: (i, 0)),
            pl.BlockSpec((k, tn), lambda i, j: (0, j)),
            pl.BlockSpec((tm, tn), lambda i, j: (i, j)),
        ],
        out_specs=pl.BlockSpec((tm, tn), lambda i, j: (i, j)),
        out_shape=jax.ShapeDtypeStruct((m, n), F32),
        compiler_params=_params("parallel", "arbitrary"),
        name="matmul_residual",
    )(a, w, res)


def _final_norm_kernel(x_ref, g_ref, o_ref):
    o_ref[...] = _rms_scale(x_ref[...], g_ref[...])


def final_rms_norm(x, g, tm=512):
    m, k = x.shape
    tm = _tile(m, tm)
    return pl.pallas_call(
        _final_norm_kernel,
        grid=(m // tm,),
        in_specs=[pl.BlockSpec((tm, k), lambda i: (i, 0)), pl.BlockSpec((1, k), lambda i: (0, 0))],
        out_specs=pl.BlockSpec((tm, k), lambda i: (i, 0)),
        out_shape=jax.ShapeDtypeStruct((m, k), F32),
        compiler_params=_params("parallel"),
        name="final_norm",
    )(x, g.reshape(1, k))


def _sb_attn_kernel(q_ref, k_ref, v_ref, o_ref, kmax2_ref, *, bq, bk, d, heads, scale):
    i = pl.program_id(2)
    n_diag = bq // bk
    row = lax.broadcasted_iota(jnp.int32, (bk, bk), 0)
    col = lax.broadcasted_iota(jnp.int32, (bk, bk), 1)
    suffix_ones = (row >= col).astype(BF16)
    q_idx = lax.broadcasted_iota(jnp.int32, (bq, bk), 0)
    k_idx = lax.broadcasted_iota(jnp.int32, (bq, bk), 1)

    def scores(hh, m):
        lanes = slice(hh * d, (hh + 1) * d)
        kb = k_ref[0, pl.ds(pl.multiple_of(m * bk, bk), bk), lanes]
        return lax.dot_general(q_ref[0, :, lanes], kb, _NT, preferred_element_type=F32)

    def absorb(hh, m, qk, acc, log_run, key_offset):
        vb = v_ref[0, pl.ds(pl.multiple_of(m * bk, bk), bk), hh * d:(hh + 1) * d]
        z = qk * scale
        neg_z = -z
        log_keep = jnp.minimum(neg_z, 0.0) - jnp.log2(1.0 + jnp.exp2(jnp.minimum(z, neg_z)))
        if key_offset is not None:
            strict = k_idx + key_offset < q_idx
            log_keep = jnp.where(strict, log_keep, 0.0)
        incl = jnp.dot(log_keep.astype(BF16), suffix_ones, preferred_element_type=F32)
        logw = z + incl + log_run
        if key_offset is not None:
            logw = jnp.where(strict, logw, -jnp.inf)
        p = jnp.exp2(logw)
        acc = acc + jnp.dot(p.astype(BF16), vb, preferred_element_type=F32)
        return acc, log_run + incl[:, 0:1]

    def step(m, state, key_offset):
        return tuple(absorb(hh, m, scores(hh, m), *state[hh], key_offset) for hh in range(heads))

    @pl.when(i == 0)
    def _():
        for hh in range(heads):
            kf = k_ref[0, :, hh * d:(hh + 1) * d].astype(F32)
            k_norm2 = jnp.max(jnp.sum(kf * kf, axis=-1, keepdims=True), axis=0, keepdims=True)
            kmax2_ref[hh] = jnp.broadcast_to(k_norm2, kmax2_ref.shape[1:])

    logit_bound = []
    for hh in range(heads):
        qf = q_ref[0, :, hh * d:(hh + 1) * d].astype(F32)
        q_norm2 = jnp.sum(qf * qf, axis=-1, keepdims=True)
        logit_bound.append(jnp.sqrt(q_norm2 * kmax2_ref[hh][0:1, 0:1]) * (scale * 1.01) + 1.0)

    def any_weight_left(state):
        worst = [jnp.max(state[hh][1] + logit_bound[hh]) for hh in range(heads)]
        return (functools.reduce(jnp.maximum, worst) >= EXP2_UNDERFLOW).astype(jnp.int32)

    state = tuple((jnp.zeros((bq, d), F32), jnp.zeros((bq, 1), F32)) for _ in range(heads))
    for dd in reversed(range(n_diag)):
        state = step(i * n_diag + dd, state, dd * bk)
    n_before = i * n_diag

    def more(c):
        t, go, _ = c
        return jnp.logical_and(t < n_before, go > 0)

    def walk(c):
        t, _, state = c
        state = step(n_before - 1 - t, state, None)
        return t + 1, any_weight_left(state), state

    _, _, state = lax.while_loop(more, walk, (jnp.int32(0), any_weight_left(state), state))
    for hh in range(heads):
        o_ref[0, :, hh * d:(hh + 1) * d] = state[hh][0].astype(o_ref.dtype)


def sb_attention(qkv, n_heads, bq=512, bk=256, heads_per_step=2):
    b, s, three_hd = qkv.shape
    d = three_hd // (3 * n_heads)
    assert d == LANES
    bq = min(bq, s)
    bk = min(bk, bq)
    hp = heads_per_step
    assert s % bq == 0 and bq % bk == 0 and n_heads % hp == 0
    n_grp = n_heads // hp
    kern = functools.partial(_sb_attn_kernel, bq=bq, bk=bk, d=d, heads=hp, scale=d ** -0.5 * math.log2(math.e))
    return pl.pallas_call(
        kern,
        grid=(b, n_grp, s // bq),
        in_specs=[
            pl.BlockSpec((1, bq, hp * d), lambda bi, h, i: (bi, i, h)),
            pl.BlockSpec((1, s, hp * d), lambda bi, h, i: (bi, 0, n_grp + h)),
            pl.BlockSpec((1, s, hp * d), lambda bi, h, i: (bi, 0, 2 * n_grp + h)),
        ],
        out_specs=pl.BlockSpec((1, bq, hp * d), lambda bi, h, i: (bi, i, h)),
        out_shape=jax.ShapeDtypeStruct((b, s, n_heads * d), BF16),
        scratch_shapes=[pltpu.VMEM((hp, 8, LANES), F32)],
        compiler_params=_params("parallel", "parallel", "arbitrary"),
        name="sb_attention",
    )(qkv, qkv, qkv)


def _swa_kernel(q_ref, kp_ref, kc_ref, vp_ref, vc_ref, slope_ref, sink_ref, o_ref, *, n_kv, grp, d, blk, scale):
    i = pl.program_id(1)
    rows = grp * blk
    qpos = lax.broadcasted_iota(jnp.int32, (rows, 2 * blk), 0) & (blk - 1)
    kcol = lax.broadcasted_iota(jnp.int32, (rows, 2 * blk), 1)
    dist = qpos + blk - kcol
    first_key = jnp.where(i > 0, 0, blk)
    in_window = (dist >= 0) & (dist < SWA_WINDOW) & (kcol >= first_key)
    dist_f = dist.astype(F32)
    outs = []
    for kh in range(n_kv):
        qg = jnp.concatenate(
            [q_ref[0, :, (kh * grp + g) * d:(kh * grp + g + 1) * d] for g in range(grp)], axis=0)
        kk = jnp.concatenate([kp_ref[0, :, kh * d:(kh + 1) * d], kc_ref[0, :, kh * d:(kh + 1) * d]], axis=0)
        vv = jnp.concatenate([vp_ref[0, :, kh * d:(kh + 1) * d], vc_ref[0, :, kh * d:(kh + 1) * d]], axis=0)
        z = lax.dot_general(qg, kk, _NT, preferred_element_type=F32) * scale
        z = z - slope_ref[kh] * dist_f
        z = jnp.where(in_window, z, -jnp.inf)
        sink = sink_ref[kh]
        mx = jnp.maximum(jnp.max(z, axis=-1, keepdims=True), sink)
        p = jnp.exp(z - mx)
        den = jnp.sum(p, axis=-1, keepdims=True) + jnp.exp(sink - mx)
        o = jnp.dot(p.astype(BF16), vv, preferred_element_type=F32) / den
        outs.extend(o[g * blk:(g + 1) * blk] for g in range(grp))
    o_ref[0] = jnp.concatenate(outs, axis=1).astype(o_ref.dtype)


def swa_attention(qkv, sinks, n_heads, n_kv, d):
    b, s, _ = qkv.shape
    blk = SWA_WINDOW
    grp = n_heads // n_kv
    qd, kd = n_heads * d, n_kv * d
    assert qd % kd == 0 and kd % LANES == 0 and s % blk == 0
    slopes = 2.0 ** (-8.0 * jnp.arange(1, n_heads + 1, dtype=F32) / n_heads)
    per_row = lambda a: jnp.repeat(a.astype(F32).reshape(n_kv, grp), blk, axis=1).reshape(n_kv, grp * blk, 1)
    kern = functools.partial(_swa_kernel, n_kv=n_kv, grp=grp, d=d, blk=blk, scale=d ** -0.5)
    prev = lambda bi, i: (bi, jnp.maximum(i - 1, 0), qd // kd)
    cur = lambda bi, i: (bi, i, qd // kd)
    prev_v = lambda bi, i: (bi, jnp.maximum(i - 1, 0), qd // kd + 1)
    cur_v = lambda bi, i: (bi, i, qd // kd + 1)
    const = lambda bi, i: (0, 0, 0)
    return pl.pallas_call(
        kern,
        grid=(b, s // blk),
        in_specs=[
            pl.BlockSpec((1, blk, qd), lambda bi, i: (bi, i, 0)),
            pl.BlockSpec((1, blk, kd), prev),
            pl.BlockSpec((1, blk, kd), cur),
            pl.BlockSpec((1, blk, kd), prev_v),
            pl.BlockSpec((1, blk, kd), cur_v),
            pl.BlockSpec((n_kv, grp * blk, 1), const),
            pl.BlockSpec((n_kv, grp * blk, 1), const),
        ],
        out_specs=pl.BlockSpec((1, blk, qd), lambda bi, i: (bi, i, 0)),
        out_shape=jax.ShapeDtypeStruct((b, s, qd), BF16),
        compiler_params=_params("parallel", "arbitrary"),
        name="swa_attention",
    )(qkv, qkv, qkv, qkv, qkv, per_row(slopes), per_row(sinks))


def _rope_lanes(r, cos_ref, sin_lo_ref, sin_hi_ref):
    half = MLA_ROPE_DIM // 2
    return (r * cos_ref[...]
            + pltpu.roll(r, LANES - half, 1) * sin_lo_ref[...]
            + pltpu.roll(r, half, 1) * sin_hi_ref[...])


def _mla_q_kernel(x_ref, g_ref, w_ref, cos_ref, slo_ref, shi_ref, o_ref, *, n_heads):
    xn = _rms_scale(x_ref[...], g_ref[...]).astype(BF16)
    lane = lax.broadcasted_iota(jnp.int32, (x_ref.shape[0], LANES), 1)
    for h in range(n_heads):
        a = jnp.dot(xn, w_ref[:, 2 * LANES * h:2 * LANES * (h + 1)], preferred_element_type=F32)
        nope = a[:, :LANES]
        rope = _rope_lanes(a[:, LANES:], cos_ref, slo_ref, shi_ref)
        norm2 = jnp.sum(nope * nope, axis=-1, keepdims=True) + jnp.sum(rope * rope, axis=-1, keepdims=True)
        o_ref[:, 2 * LANES * h:2 * LANES * h + LANES] = nope.astype(BF16)
        o_ref[:, 2 * LANES * h + LANES:2 * LANES * (h + 1)] = jnp.where(
            lane == MLA_ROPE_DIM, NORM_MARGIN * jnp.sqrt(norm2), rope).astype(BF16)


def _mla_kv_kernel(x_ref, g_ref, xr_ref, w_ref, cos_ref, slo_ref, shi_ref, kv_ref, kr_ref, *, chunk):
    xn = _rms_scale(x_ref[...], g_ref[...]).astype(BF16)
    for c in range(w_ref.shape[1] // chunk):
        cols = slice(c * chunk, (c + 1) * chunk)
        kv_ref[:, cols] = jnp.dot(xn, w_ref[:, cols], preferred_element_type=F32).astype(BF16)
    kr_ref[...] = _rope_lanes(xr_ref[...], cos_ref, slo_ref, shi_ref).astype(BF16)


def _rope_tables(s):
    half = MLA_ROPE_DIM // 2
    inv_freq = ROPE_THETA ** (-jnp.arange(half, dtype=F32) / half)
    ang = jnp.arange(s, dtype=F32)[:, None] * inv_freq[None, :]
    cos, sin = jnp.cos(ang), jnp.sin(ang)
    zeros = jnp.zeros((s, LANES - 2 * half), F32)
    z_half = jnp.zeros((s, half), F32)
    cos_t = jnp.concatenate([cos, cos, zeros], axis=1)
    sin_lo = jnp.concatenate([-sin, z_half, zeros], axis=1)
    sin_hi = jnp.concatenate([z_half, sin, zeros], axis=1)
    return cos_t, sin_lo, sin_hi


def mla_projections(down, g_q, g_kv, w_uq_pad, w_ukv, seq, tm=512):
    t = down.shape[0]
    tm = _tile(seq, tm)
    n_heads = w_uq_pad.shape[1] // (2 * LANES)
    tables = _rope_tables(seq)
    n_pos = seq // tm
    tab_spec = pl.BlockSpec((tm, LANES), lambda i: (i % n_pos, 0))
    q = pl.pallas_call(
        functools.partial(_mla_q_kernel, n_heads=n_heads),
        grid=(t // tm,),
        in_specs=[
            pl.BlockSpec((tm, MLA_Q_RANK), lambda i: (i, 0)),
            pl.BlockSpec((1, MLA_Q_RANK), lambda i: (0, 0)),
            pl.BlockSpec(w_uq_pad.shape, lambda i: (0, 0)),
            tab_spec, tab_spec, tab_spec,
        ],
        out_specs=pl.BlockSpec((tm, w_uq_pad.shape[1]), lambda i: (i, 0)),
        out_shape=jax.ShapeDtypeStruct((t, w_uq_pad.shape[1]), BF16),
        compiler_params=_params("parallel"),
        name="mla_q_proj",
    )(down, g_q.reshape(1, -1), w_uq_pad, *tables)
    assert MLA_Q_RANK == MLA_KV_RANK and (MLA_Q_RANK + MLA_KV_RANK) % LANES == 0
    kv, k_rope = pl.pallas_call(
        functools.partial(_mla_kv_kernel, chunk=_tile(w_ukv.shape[1], 1024)),
        grid=(t // tm,),
        in_specs=[
            pl.BlockSpec((tm, MLA_KV_RANK), lambda i: (i, 1)),
            pl.BlockSpec((1, MLA_KV_RANK), lambda i: (0, 0)),
            pl.BlockSpec((tm, LANES), lambda i: (i, (MLA_Q_RANK + MLA_KV_RANK) // LANES)),
            pl.BlockSpec(w_ukv.shape, lambda i: (0, 0)),
            tab_spec, tab_spec, tab_spec,
        ],
        out_specs=[pl.BlockSpec((tm, w_ukv.shape[1]), lambda i: (i, 0)),
                   pl.BlockSpec((tm, LANES), lambda i: (i, 0))],
        out_shape=[jax.ShapeDtypeStruct((t, w_ukv.shape[1]), BF16), jax.ShapeDtypeStruct((t, LANES), BF16)],
        compiler_params=_params("parallel"),
        name="mla_kv_proj",
    )(down, g_kv.reshape(1, -1), down, w_ukv, *tables)
    return q, kv, k_rope


def _mla_attn_kernel(q_ref, kv_ref, kr_ref, o_ref, k_ref, v_ref, *, bq, bk, heads, scale):
    i = pl.program_id(2)
    width = 2 * LANES

    @pl.when(i == 0)
    def _():
        kr = kr_ref[0]
        krf = kr.astype(F32)
        kr_norm2 = jnp.sum(krf * krf, axis=-1, keepdims=True)
        lane = lax.broadcasted_iota(jnp.int32, kr.shape, 1)
        for hh in range(heads):
            kn = kv_ref[0, :, hh * width:hh * width + LANES]
            knf = kn.astype(F32)
            k_max = jnp.sqrt(jnp.max(jnp.sum(knf * knf, axis=-1, keepdims=True) + kr_norm2, axis=0, keepdims=True))
            k_ref[hh, :, :LANES] = kn
            k_ref[hh, :, LANES:] = jnp.where(lane == MLA_ROPE_DIM, (-NORM_MARGIN * k_max).astype(BF16), kr)
            v_ref[hh, :, :LANES] = kv_ref[0, :, hh * width + LANES:(hh + 1) * width]
            v_ref[hh, :, LANES:] = jnp.ones((kr.shape[0], LANES), BF16)

    n_diag = bq // bk
    n_before = i * n_diag
    row = lax.broadcasted_iota(jnp.int32, (bk, bk), 0)
    col = lax.broadcasted_iota(jnp.int32, (bk, bk), 1)
    causal = col <= row

    def shifted_logits(hh, m, first_row):
        kb = k_ref[hh, pl.ds(pl.multiple_of(m * bk, bk), bk), :]
        q = q_ref[0, first_row:, hh * width:(hh + 1) * width]
        return lax.dot_general(q, kb, _NT, preferred_element_type=F32)

    def values(hh, m):
        return v_ref[hh, pl.ds(pl.multiple_of(m * bk, bk), bk), :]

    def quick_step(hh, m, acc, dd):
        first_row = 0 if dd is None else dd * bk
        p = jnp.exp2(shifted_logits(hh, m, first_row) * (scale * math.log2(math.e)))
        if dd is not None:
            on_diagonal = jnp.where(causal, p[:bk], 0.0)
            p = on_diagonal if p.shape[0] == bk else jnp.concatenate([on_diagonal, p[bk:]], axis=0)
        pv = jnp.dot(p.astype(BF16), values(hh, m), preferred_element_type=F32)
        return acc + pv if first_row == 0 else jnp.concatenate([acc[:first_row], acc[first_row:] + pv], axis=0)

    acc = tuple(jnp.zeros((bq, width), F32) for _ in range(heads))
    acc = lax.fori_loop(
        0, n_before, lambda m, a: tuple(quick_step(hh, m, a[hh], None) for hh in range(heads)), acc)
    for dd in range(n_diag):
        acc = tuple(quick_step(hh, n_before + dd, acc[hh], dd) for hh in range(heads))
    smallest = functools.reduce(jnp.minimum, [jnp.min(acc[hh][:, LANES:]) for hh in range(heads)])
    for hh in range(heads):
        o_ref[0, :, hh * LANES:(hh + 1) * LANES] = (acc[hh][:, :LANES] / acc[hh][:, LANES:]).astype(o_ref.dtype)

    @pl.when(jnp.logical_not(smallest >= DENOMINATOR_FLOOR))
    def _():
        def careful_step(hh, m, state, dd):
            acc, mx, den = state
            z = shifted_logits(hh, m, 0) * scale
            if dd is not None:
                q_idx = lax.broadcasted_iota(jnp.int32, (bq, bk), 0)
                k_idx = lax.broadcasted_iota(jnp.int32, (bq, bk), 1)
                z = jnp.where(k_idx + dd * bk <= q_idx, z, -jnp.inf)
            new_mx = jnp.maximum(mx, jnp.max(z, axis=-1, keepdims=True))
            corr = jnp.exp(mx - new_mx)
            pz = jnp.exp(z - new_mx)
            den = den * corr + jnp.sum(pz, axis=-1, keepdims=True)
            pv = jnp.dot(pz.astype(BF16), values(hh, m)[:, :LANES], preferred_element_type=F32)
            return acc * corr + pv, new_mx, den

        for hh in range(heads):
            state = (jnp.zeros((bq, LANES), F32), jnp.full((bq, 1), -jnp.inf, F32), jnp.zeros((bq, 1), F32))
            state = lax.fori_loop(0, n_before, lambda m, c: careful_step(hh, m, c, None), state)
            for dd in range(n_diag):
                state = careful_step(hh, n_before + dd, state, dd)
            o_ref[0, :, hh * LANES:(hh + 1) * LANES] = (state[0] / state[2]).astype(o_ref.dtype)


def mla_attention(q, kv, k_rope, n_heads, bq=1024, bk=1024, heads_per_step=2):
    b, s, _ = q.shape
    bq = min(bq, s)
    bk = min(bk, bq)
    hp = heads_per_step
    assert s % bq == 0 and bq % bk == 0 and n_heads % hp == 0 and MLA_V_DIM == LANES
    scale = (MLA_NOPE_DIM + MLA_ROPE_DIM) ** -0.5
    return pl.pallas_call(
        functools.partial(_mla_attn_kernel, bq=bq, bk=bk, heads=hp, scale=scale),
        grid=(b, n_heads // hp, s // bq),
        in_specs=[
            pl.BlockSpec((1, bq, hp * 2 * LANES), lambda bi, h, i: (bi, i, h)),
            pl.BlockSpec((1, s, hp * 2 * LANES), lambda bi, h, i: (bi, 0, h)),
            pl.BlockSpec((1, s, LANES), lambda bi, h, i: (bi, 0, 0)),
        ],
        out_specs=pl.BlockSpec((1, bq, hp * LANES), lambda bi, h, i: (bi, i, h)),
        out_shape=jax.ShapeDtypeStruct((b, s, n_heads * MLA_V_DIM), BF16),
        scratch_shapes=[pltpu.VMEM((hp, s, 2 * LANES), BF16), pltpu.VMEM((hp, s, 2 * LANES), BF16)],
        compiler_params=_params("parallel", "parallel", "arbitrary"),
        name="mla_attention",
    )(q, kv, k_rope)


def _extract_top(x, out_ref, k):
    n_rows = x.shape[0]

    def drop_all(r, y):
        m = jnp.max(y, axis=0, keepdims=True)
        out_ref[pl.ds(r, 1), :] = m
        return jnp.where(y == m, -jnp.inf, y)

    lax.fori_loop(0, k, drop_all, x)
    at_least_kth = jnp.sum((x >= out_ref[k - 1:k, :]).astype(F32), axis=0, keepdims=True)

    @pl.when(jnp.max(at_least_kth) > k)
    def _():
        rows = lax.broadcasted_iota(jnp.int32, x.shape, 0).astype(F32)

        def drop_first(r, y):
            m = jnp.max(y, axis=0, keepdims=True)
            out_ref[pl.ds(r, 1), :] = m
            first = jnp.min(jnp.where(y == m, rows, float(n_rows)), axis=0, keepdims=True)
            return jnp.where(rows == first, -jnp.inf, y)

        lax.fori_loop(0, k, drop_first, x)


def _peer_select_kernel(qh_ref, keys_ref, st_ref, et_ref, thr_ref, v1_ref, v2_ref, top_ref, *, n_keys, key_half, topk):
    halves = []
    for half, v_ref in ((0, v1_ref), (1, v2_ref)):
        qs = qh_ref[:, half * key_half:(half + 1) * key_half]
        s_t = lax.dot_general(keys_ref[half], qs, _NT, preferred_element_type=F32)
        st_ref[0, half * n_keys:(half + 1) * n_keys, :] = s_t
        _extract_top(s_t, v_ref, topk)
        halves.append(s_t)
    v1, v2 = v1_ref[...], v2_ref[...]
    assert topk == 16
    slabs = [v1 + v2[0:1]]
    slabs += [v1[0:8] + v2[b:b + 1] for b in range(1, 8)]
    slabs += [v1[0:1] + v2[8:16]]
    _extract_top(jnp.concatenate(slabs, axis=0), top_ref, topk)
    top = top_ref[...]
    thr_ref[0] = top[topk - 1:topk]
    z_sum = jnp.sum(jnp.exp(top - top[0:1]), axis=0, keepdims=True)
    et_ref[0, :n_keys, :] = jnp.exp(halves[0] - v1[0:1])
    et_ref[0, n_keys:, :] = jnp.exp(halves[1] - v2[0:1]) / z_sum


def peer_select(qh, sub_keys, tt=512):
    t = qh.shape[0]
    _, n_keys, key_half = sub_keys.shape
    n_heads = qh.shape[1] // (2 * key_half)
    tt = _tile(t, tt)
    kern = functools.partial(_peer_select_kernel, n_keys=n_keys, key_half=key_half, topk=PEER_TOPK)
    return pl.pallas_call(
        kern,
        grid=(t // tt, n_heads),
        in_specs=[
            pl.BlockSpec((tt, 2 * key_half), lambda i, h: (i, h)),
            pl.BlockSpec(sub_keys.shape, lambda i, h: (0, 0, 0)),
        ],
        out_specs=[
            pl.BlockSpec((1, 2 * n_keys, tt), lambda i, h: (h, 0, i)),
            pl.BlockSpec((1, 2 * n_keys, tt), lambda i, h: (h, 0, i)),
            pl.BlockSpec((1, 1, tt), lambda i, h: (h, 0, i)),
        ],
        out_shape=[
            jax.ShapeDtypeStruct((n_heads, 2 * n_keys, t), F32),
            jax.ShapeDtypeStruct((n_heads, 2 * n_keys, t), F32),
            jax.ShapeDtypeStruct((n_heads, 1, t), F32),
        ],
        scratch_shapes=[pltpu.VMEM((PEER_TOPK, tt), F32)] * 3,
        compiler_params=_params("parallel", "arbitrary"),
        name="peer_select",
    )(qh, sub_keys)


def _gelu_tanh(x):
    return 0.5 * x * (1.0 + jnp.tanh(math.sqrt(2.0 / math.pi) * (x + 0.044715 * (x * x * x))))


def _peer_dense_kernel(h_ref, g_ref, u_ref, v_ref, st_ref, et_ref, thr_ref, o_ref, xn_ref, *, n_heads, n_keys, te):
    e = pl.program_id(1)

    @pl.when(e == 0)
    def _():
        x = h_ref[...]
        xn_ref[...] = _rms_scale(x, g_ref[...]).astype(BF16)
        o_ref[...] = x

    hidden = lax.dot_general(xn_ref[...], u_ref[...], _NT, preferred_element_type=F32)
    rows_per_step = te // n_keys
    blocks = []
    for ii in range(rows_per_step):
        i_row = e * rows_per_step + ii
        gate_t = None
        for h in range(n_heads):
            score = st_ref[h, n_keys:2 * n_keys, :] + st_ref[h, pl.ds(i_row, 1), :]
            weight = et_ref[h, n_keys:2 * n_keys, :] * et_ref[h, pl.ds(i_row, 1), :]
            g = jnp.where(score >= thr_ref[h], weight, 0.0)
            gate_t = g if gate_t is None else gate_t + g
        blocks.append(gate_t)
    gate = jnp.concatenate(blocks, axis=0).T
    gh = (gate * _gelu_tanh(hidden)).astype(BF16)
    o_ref[...] += jnp.dot(gh, v_ref[...], preferred_element_type=F32)


def peer_dense(h, g, u, v, scores_t, factors_t, thr, tt=512, te=512):
    t, d = h.shape
    n_exp = u.shape[0]
    n_heads, two_keys, _ = scores_t.shape
    n_keys = two_keys // 2
    tt, te = _tile(t, tt), _tile(n_exp, te)
    assert te % n_keys == 0 and n_exp == n_keys * n_keys
    kern = functools.partial(_peer_dense_kernel, n_heads=n_heads, n_keys=n_keys, te=te)
    return pl.pallas_call(
        kern,
        grid=(t // tt, n_exp // te),
        in_specs=[
            pl.BlockSpec((tt, d), lambda i, e: (i, 0)),
            pl.BlockSpec((1, d), lambda i, e: (0, 0)),
            pl.BlockSpec((te, d), lambda i, e: (e, 0)),
            pl.BlockSpec((te, d), lambda i, e: (e, 0)),
            pl.BlockSpec((n_heads, two_keys, tt), lambda i, e: (0, 0, i)),
            pl.BlockSpec((n_heads, two_keys, tt), lambda i, e: (0, 0, i)),
            pl.BlockSpec((n_heads, 1, tt), lambda i, e: (0, 0, i)),
        ],
        out_specs=pl.BlockSpec((tt, d), lambda i, e: (i, 0)),
        out_shape=jax.ShapeDtypeStruct((t, d), F32),
        scratch_shapes=[pltpu.VMEM((tt, d), BF16)],
        compiler_params=_params("parallel", "arbitrary"),
        name="peer_dense",
    )(h, g.reshape(1, d), u, v, scores_t, factors_t, thr)


def _ple_kernel(x_ref, g_ref, wg_ref, p_ref, wp_ref, r_ref, o_ref, xn_ref, pb_ref):
    @pl.when(pl.program_id(1) == 0)
    def _():
        xn_ref[...] = _rms_scale(x_ref[...], g_ref[...]).astype(BF16)
        pb_ref[...] = p_ref[...].astype(BF16)

    gate = 1.0 / (1.0 + jnp.exp(-jnp.dot(xn_ref[...], wg_ref[...], preferred_element_type=F32)))
    o_ref[...] = r_ref[...] + jnp.dot(pb_ref[...], wp_ref[...], preferred_element_type=F32) * gate


def per_layer_embedding(h, p, g, w_g, w_p, tm=512, tn=1024):
    t, d = h.shape
    pd = p.shape[1]
    tm, tn = _tile(t, tm), _tile(d, tn)
    return pl.pallas_call(
        _ple_kernel,
        grid=(t // tm, d // tn),
        in_specs=[
            pl.BlockSpec((tm, d), lambda i, j: (i, 0)),
            pl.BlockSpec((1, d), lambda i, j: (0, 0)),
            pl.BlockSpec((d, tn), lambda i, j: (0, j)),
            pl.BlockSpec((tm, pd), lambda i, j: (i, 0)),
            pl.BlockSpec((pd, tn), lambda i, j: (0, j)),
            pl.BlockSpec((tm, tn), lambda i, j: (i, j)),
        ],
        out_specs=pl.BlockSpec((tm, tn), lambda i, j: (i, j)),
        out_shape=jax.ShapeDtypeStruct((t, d), F32),
        scratch_shapes=[pltpu.VMEM((tm, d), BF16), pltpu.VMEM((tm, pd), BF16)],
        compiler_params=_params("parallel", "arbitrary"),
        name="per_layer_embedding",
    )(h, g.reshape(1, d), w_g, p, w_p, h)


def _pad_cols(w, multiple):
    extra = (-w.shape[-1]) % multiple
    return jnp.pad(w, ((0, 0), (0, extra))) if extra else w


def kernel(x, p, attn_norm, ffn_norm, ple_norm, final_norm, sb_w_qkv, sb_w_o, swa_w_qkv, swa_w_o, swa_sinks,
           mla_w_down, mla_q_norm, mla_kv_norm, mla_w_uq, mla_w_ukv, mla_w_o,
           peer_w_q, peer_sub_keys, peer_u, peer_v, ple_w_p, ple_w_g):
    b, s, d = x.shape
    t = b * s
    depth = p.shape[0]
    h = x.reshape(t, d)
    for i in range(depth):
        kind, j = i % N_MIXERS, i // N_MIXERS
        if kind == 0:
            qkv = norm_matmul(h, attn_norm[i], sb_w_qkv[j].astype(BF16), BF16)
            o = sb_attention(qkv.reshape(b, s, -1), SB_HEADS)
            w_o = sb_w_o[j]
        elif kind == 1:
            qkv = norm_matmul(h, attn_norm[i], swa_w_qkv[j].astype(BF16), BF16, tn=1280)
            head_dim = swa_w_o.shape[1] // SWA_HEADS
            o = swa_attention(qkv.reshape(b, s, -1), swa_sinks[j], SWA_HEADS, SWA_KV_HEADS, head_dim)
            w_o = swa_w_o[j]
        else:
            down = norm_matmul(h, attn_norm[i], _pad_cols(mla_w_down[j], LANES).astype(BF16), F32, tn=2048)
            qk_dim = MLA_NOPE_DIM + MLA_ROPE_DIM
            w_uq_pad = jnp.pad(mla_w_uq[j].reshape(MLA_Q_RANK, MLA_HEADS, qk_dim),
                               ((0, 0), (0, 0), (0, 2 * LANES - qk_dim))).reshape(MLA_Q_RANK, -1)
            q, kv, k_rope = mla_projections(down, mla_q_norm[j], mla_kv_norm[j], w_uq_pad.astype(BF16),
                                            mla_w_ukv[j].astype(BF16), s)
            o = mla_attention(q.reshape(b, s, -1), kv.reshape(b, s, -1), k_rope.reshape(b, s, -1), MLA_HEADS)
            w_o = mla_w_o[j]
        h = matmul_residual(o.reshape(t, -1), w_o.astype(BF16), h)
        qh = norm_matmul(h, ffn_norm[i], peer_w_q[i].astype(BF16), BF16)
        scores_t, factors_t, thr = peer_select(qh, peer_sub_keys[i].astype(BF16))
        h = peer_dense(h, ffn_norm[i], peer_u[i].astype(BF16), peer_v[i].astype(BF16), scores_t, factors_t, thr)
        h = per_layer_embedding(h, p[i].reshape(t, -1), ple_norm[i], ple_w_g[i].astype(BF16), ple_w_p[i].astype(BF16))
    return final_rms_norm(h, final_norm).reshape(b, s, d)
```

```python
import functools
import math

import jax
import jax.numpy as jnp
from jax import lax
from jax.experimental import pallas as pl
from jax.experimental.pallas import tpu as pltpu

F32 = jnp.float32
BF16 = jnp.bfloat16

RMS_EPS = 1e-6
N_MIXERS = 3
SB_HEADS = 16
SWA_HEADS = 32
SWA_KV_HEADS = 4
SWA_WINDOW = 128
MLA_HEADS = 16
MLA_Q_RANK = 512
MLA_KV_RANK = 512
MLA_NOPE_DIM = 128
MLA_ROPE_DIM = 64
MLA_V_DIM = 128
ROPE_THETA = 10000.0
PEER_N_KEYS = 64
PEER_TOPK = 16

NORM_MARGIN = 1.02
DENOMINATOR_FLOOR = 2.0 ** -80
EXP2_UNDERFLOW = -150.0
LANES = 128
V7X_VMEM_BYTES = 64 * 1024 * 1024
VMEM_LIMIT = V7X_VMEM_BYTES * 7 // 8

_NT = (((1,), (1,)), ((), ()))


def _params(*semantics):
    return pltpu.CompilerParams(dimension_semantics=semantics, vmem_limit_bytes=VMEM_LIMIT)


def _tile(n, want):
    if n <= want:
        return n
    t = (want // LANES) * LANES
    while t > LANES and n % t:
        t -= LANES
    assert n % t == 0, (n, want)
    return t


def _rms_scale(x, g):
    ms = jnp.mean(x * x, axis=-1, keepdims=True)
    return x * lax.rsqrt(ms + RMS_EPS) * g


def _norm_matmul_kernel(x_ref, g_ref, w_ref, o_ref, xn_ref):
    @pl.when(pl.program_id(1) == 0)
    def _():
        xn_ref[...] = _rms_scale(x_ref[...], g_ref[...]).astype(BF16)

    o_ref[...] = jnp.dot(xn_ref[...], w_ref[...], preferred_element_type=F32).astype(o_ref.dtype)


def norm_matmul(x, g, w, out_dtype, tm=1024, tn=1024):
    m, k = x.shape
    n = w.shape[1]
    tm, tn = _tile(m, tm), _tile(n, tn)
    return pl.pallas_call(
        _norm_matmul_kernel,
        grid=(m // tm, n // tn),
        in_specs=[
            pl.BlockSpec((tm, k), lambda i, j: (i, 0)),
            pl.BlockSpec((1, k), lambda i, j: (0, 0)),
            pl.BlockSpec((k, tn), lambda i, j: (0, j)),
        ],
        out_specs=pl.BlockSpec((tm, tn), lambda i, j: (i, j)),
        out_shape=jax.ShapeDtypeStruct((m, n), out_dtype),
        scratch_shapes=[pltpu.VMEM((tm, k), BF16)],
        compiler_params=_params("parallel", "arbitrary"),
        name="norm_matmul",
    )(x, g.reshape(1, k), w)


def _matmul_residual_kernel(a_ref, w_ref, r_ref, o_ref):
    o_ref[...] = r_ref[...] + jnp.dot(a_ref[...], w_ref[...], preferred_element_type=F32)


def matmul_residual(a, w, res, tm=1024, tn=1024):
    m, k = a.shape
    n = w.shape[1]
    tm, tn = _tile(m, tm), _tile(n, tn)
    return pl.pallas_call(
        _matmul_residual_kernel,
        grid=(m // tm, n // tn),
        in_specs=[
            pl.BlockSpec((tm, k), lambda i, j: (i, 0)),
            pl.BlockSpec((k, tn), lambda i, j: (0, j)),
            pl.BlockSpec((tm, tn), lambda i, j: (i, j)),
        ],
        out_specs=pl.BlockSpec((tm, tn), lambda i, j: (i, j)),
        out_shape=jax.ShapeDtypeStruct((m, n), F32),
        compiler_params=_params("parallel", "arbitrary"),
        name="matmul_residual",
    )(a, w, res)


def _final_norm_kernel(x_ref, g_ref, o_ref):
    o_ref[...] = _rms_scale(x_ref[...], g_ref[...])


def final_rms_norm(x, g, tm=512):
    m, k = x.shape
    tm = _tile(m, tm)
    return pl.pallas_call(
        _final_norm_kernel,
        grid=(m // tm,),
        in_specs=[pl.BlockSpec((tm, k), lambda i: (i, 0)), pl.BlockSpec((1, k), lambda i: (0, 0))],
        out_specs=pl.BlockSpec((tm, k), lambda i: (i, 0)),
        out_shape=jax.ShapeDtypeStruct((m, k), F32),
        compiler_params=_params("parallel"),
        name="final_norm",
    )(x, g.reshape(1, k))


def _sb_attn_kernel(q_ref, k_ref, v_ref, o_ref, kmax2_ref, *, bq, bk, d, heads, scale):
    i = pl.program_id(2)
    n_diag = bq // bk
    row = lax.broadcasted_iota(jnp.int32, (bk, bk), 0)
    col = lax.broadcasted_iota(jnp.int32, (bk, bk), 1)
    suffix_ones = (row >= col).astype(BF16)
    q_idx = lax.broadcasted_iota(jnp.int32, (bq, bk), 0)
    k_idx = lax.broadcasted_iota(jnp.int32, (bq, bk), 1)

    def scores(hh, m):
        lanes = slice(hh * d, (hh + 1) * d)
        kb = k_ref[0, pl.ds(pl.multiple_of(m * bk, bk), bk), lanes]
        return lax.dot_general(q_ref[0, :, lanes], kb, _NT, preferred_element_type=F32)

    def absorb(hh, m, qk, acc, log_run, key_offset):
        vb = v_ref[0, pl.ds(pl.multiple_of(m * bk, bk), bk), hh * d:(hh + 1) * d]
        z = qk * scale
        neg_z = -z
        log_keep = jnp.minimum(neg_z, 0.0) - jnp.log2(1.0 + jnp.exp2(jnp.minimum(z, neg_z)))
        if key_offset is not None:
            strict = k_idx + key_offset < q_idx
            log_keep = jnp.where(strict, log_keep, 0.0)
        incl = jnp.dot(log_keep.astype(BF16), suffix_ones, preferred_element_type=F32)
        logw = z + incl + log_run
        if key_offset is not None:
            logw = jnp.where(strict, logw, -jnp.inf)
        p = jnp.exp2(logw)
        acc = acc + jnp.dot(p.astype(BF16), vb, preferred_element_type=F32)
        return acc, log_run + incl[:, 0:1]

    def step(m, state, key_offset):
        return tuple(absorb(hh, m, scores(hh, m), *state[hh], key_offset) for hh in range(heads))

    @pl.when(i == 0)
    def _():
        for hh in range(heads):
            kf = k_ref[0, :, hh * d:(hh + 1) * d].astype(F32)
            k_norm2 = jnp.max(jnp.sum(kf * kf, axis=-1, keepdims=True), axis=0, keepdims=True)
            kmax2_ref[hh] = jnp.broadcast_to(k_norm2, kmax2_ref.shape[1:])

    logit_bound = []
    for hh in range(heads):
        qf = q_ref[0, :, hh * d:(hh + 1) * d].astype(F32)
        q_norm2 = jnp.sum(qf * qf, axis=-1, keepdims=True)
        logit_bound.append(jnp.sqrt(q_norm2 * kmax2_ref[hh][0:1, 0:1]) * (scale * 1.01) + 1.0)

    def any_weight_left(state):
        worst = [jnp.max(state[hh][1] + logit_bound[hh]) for hh in range(heads)]
        return (functools.reduce(jnp.maximum, worst) >= EXP2_UNDERFLOW).astype(jnp.int32)

    state = tuple((jnp.zeros((bq, d), F32), jnp.zeros((bq, 1), F32)) for _ in range(heads))
    for dd in reversed(range(n_diag)):
        state = step(i * n_diag + dd, state, dd * bk)
    n_before = i * n_diag

    def more(c):
        t, go, _ = c
        return jnp.logical_and(t < n_before, go > 0)

    def walk(c):
        t, _, state = c
        state = step(n_before - 1 - t, state, None)
        return t + 1, any_weight_left(state), state

    _, _, state = lax.while_loop(more, walk, (jnp.int32(0), any_weight_left(state), state))
    for hh in range(heads):
        o_ref[0, :, hh * d:(hh + 1) * d] = state[hh][0].astype(o_ref.dtype)


def sb_attention(qkv, n_heads, bq=512, bk=256, heads_per_step=2):
    b, s, three_hd = qkv.shape
    d = three_hd // (3 * n_heads)
    assert d == LANES
    bq = min(bq, s)
    bk = min(bk, bq)
    hp = heads_per_step
    assert s % bq == 0 and bq % bk == 0 and n_heads % hp == 0
    n_grp = n_heads // hp
    kern = functools.partial(_sb_attn_kernel, bq=bq, bk=bk, d=d, heads=hp, scale=d ** -0.5 * math.log2(math.e))
    return pl.pallas_call(
        kern,
        grid=(b, n_grp, s // bq),
        in_specs=[
            pl.BlockSpec((1, bq, hp * d), lambda bi, h, i: (bi, i, h)),
            pl.BlockSpec((1, s, hp * d), lambda bi, h, i: (bi, 0, n_grp + h)),
            pl.BlockSpec((1, s, hp * d), lambda bi, h, i: (bi, 0, 2 * n_grp + h)),
        ],
        out_specs=pl.BlockSpec((1, bq, hp * d), lambda bi, h, i: (bi, i, h)),
        out_shape=jax.ShapeDtypeStruct((b, s, n_heads * d), BF16),
        scratch_shapes=[pltpu.VMEM((hp, 8, LANES), F32)],
        compiler_params=_params("parallel", "parallel", "arbitrary"),
        name="sb_attention",
    )(qkv, qkv, qkv)


def _swa_kernel(q_ref, kp_ref, kc_ref, vp_ref, vc_ref, slope_ref, sink_ref, o_ref, *, n_kv, grp, d, blk, scale):
    i = pl.program_id(1)
    rows = grp * blk
    qpos = lax.broadcasted_iota(jnp.int32, (rows, 2 * blk), 0) & (blk - 1)
    kcol = lax.broadcasted_iota(jnp.int32, (rows, 2 * blk), 1)
    dist = qpos + blk - kcol
    first_key = jnp.where(i > 0, 0, blk)
    in_window = (dist >= 0) & (dist < SWA_WINDOW) & (kcol >= first_key)
    dist_f = dist.astype(F32)
    outs = []
    for kh in range(n_kv):
        qg = jnp.concatenate(
            [q_ref[0, :, (kh * grp + g) * d:(kh * grp + g + 1) * d] for g in range(grp)], axis=0)
        kk = jnp.concatenate([kp_ref[0, :, kh * d:(kh + 1) * d], kc_ref[0, :, kh * d:(kh + 1) * d]], axis=0)
        vv = jnp.concatenate([vp_ref[0, :, kh * d:(kh + 1) * d], vc_ref[0, :, kh * d:(kh + 1) * d]], axis=0)
        z = lax.dot_general(qg, kk, _NT, preferred_element_type=F32) * scale
        z = z - slope_ref[kh] * dist_f
        z = jnp.where(in_window, z, -jnp.inf)
        sink = sink_ref[kh]
        mx = jnp.maximum(jnp.max(z, axis=-1, keepdims=True), sink)
        p = jnp.exp(z - mx)
        den = jnp.sum(p, axis=-1, keepdims=True) + jnp.exp(sink - mx)
        o = jnp.dot(p.astype(BF16), vv, preferred_element_type=F32) / den
        outs.extend(o[g * blk:(g + 1) * blk] for g in range(grp))
    o_ref[0] = jnp.concatenate(outs, axis=1).astype(o_ref.dtype)


def swa_attention(qkv, sinks, n_heads, n_kv, d):
    b, s, _ = qkv.shape
    blk = SWA_WINDOW
    grp = n_heads // n_kv
    qd, kd = n_heads * d, n_kv * d
    assert qd % kd == 0 and kd % LANES == 0 and s % blk == 0
    slopes = 2.0 ** (-8.0 * jnp.arange(1, n_heads + 1, dtype=F32) / n_heads)
    per_row = lambda a: jnp.repeat(a.astype(F32).reshape(n_kv, grp), blk, axis=1).reshape(n_kv, grp * blk, 1)
    kern = functools.partial(_swa_kernel, n_kv=n_kv, grp=grp, d=d, blk=blk, scale=d ** -0.5)
    prev = lambda bi, i: (bi, jnp.maximum(i - 1, 0), qd // kd)
    cur = lambda bi, i: (bi, i, qd // kd)
    prev_v = lambda bi, i: (bi, jnp.maximum(i - 1, 0), qd // kd + 1)
    cur_v = lambda bi, i: (bi, i, qd // kd + 1)
    const = lambda bi, i: (0, 0, 0)
    return pl.pallas_call(
        kern,
        grid=(b, s // blk),
        in_specs=[
            pl.BlockSpec((1, blk, qd), lambda bi, i: (bi, i, 0)),
            pl.BlockSpec((1, blk, kd), prev),
            pl.BlockSpec((1, blk, kd), cur),
            pl.BlockSpec((1, blk, kd), prev_v),
            pl.BlockSpec((1, blk, kd), cur_v),
            pl.BlockSpec((n_kv, grp * blk, 1), const),
            pl.BlockSpec((n_kv, grp * blk, 1), const),
        ],
        out_specs=pl.BlockSpec((1, blk, qd), lambda bi, i: (bi, i, 0)),
        out_shape=jax.ShapeDtypeStruct((b, s, qd), BF16),
        compiler_params=_params("parallel", "arbitrary"),
        name="swa_attention",
    )(qkv, qkv, qkv, qkv, qkv, per_row(slopes), per_row(sinks))


def _rope_lanes(r, cos_ref, sin_lo_ref, sin_hi_ref):
    half = MLA_ROPE_DIM // 2
    return (r * cos_ref[...]
            + pltpu.roll(r, LANES - half, 1) * sin_lo_ref[...]
            + pltpu.roll(r, half, 1) * sin_hi_ref[...])


def _mla_q_kernel(x_ref, g_ref, w_ref, cos_ref, slo_ref, shi_ref, o_ref, *, n_heads):
    xn = _rms_scale(x_ref[...], g_ref[...]).astype(BF16)
    lane = lax.broadcasted_iota(jnp.int32, (x_ref.shape[0], LANES), 1)
    for h in range(n_heads):
        a = jnp.dot(xn, w_ref[:, 2 * LANES * h:2 * LANES * (h + 1)], preferred_element_type=F32)
        nope = a[:, :LANES]
        rope = _rope_lanes(a[:, LANES:], cos_ref, slo_ref, shi_ref)
        norm2 = jnp.sum(nope * nope, axis=-1, keepdims=True) + jnp.sum(rope * rope, axis=-1, keepdims=True)
        o_ref[:, 2 * LANES * h:2 * LANES * h + LANES] = nope.astype(BF16)
        o_ref[:, 2 * LANES * h + LANES:2 * LANES * (h + 1)] = jnp.where(
            lane == MLA_ROPE_DIM, NORM_MARGIN * jnp.sqrt(norm2), rope).astype(BF16)


def _mla_kv_kernel(x_ref, g_ref, xr_ref, w_ref, cos_ref, slo_ref, shi_ref, kv_ref, kr_ref, *, chunk):
    xn = _rms_scale(x_ref[...], g_ref[...]).astype(BF16)
    for c in range(w_ref.shape[1] // chunk):
        cols = slice(c * chunk, (c + 1) * chunk)
        kv_ref[:, cols] = jnp.dot(xn, w_ref[:, cols], preferred_element_type=F32).astype(BF16)
    kr_ref[...] = _rope_lanes(xr_ref[...], cos_ref, slo_ref, shi_ref).astype(BF16)


def _rope_tables(s):
    half = MLA_ROPE_DIM // 2
    inv_freq = ROPE_THETA ** (-jnp.arange(half, dtype=F32) / half)
    ang = jnp.arange(s, dtype=F32)[:, None] * inv_freq[None, :]
    cos, sin = jnp.cos(ang), jnp.sin(ang)
    zeros = jnp.zeros((s, LANES - 2 * half), F32)
    z_half = jnp.zeros((s, half), F32)
    cos_t = jnp.concatenate([cos, cos, zeros], axis=1)
    sin_lo = jnp.concatenate([-sin, z_half, zeros], axis=1)
    sin_hi = jnp.concatenate([z_half, sin, zeros], axis=1)
    return cos_t, sin_lo, sin_hi


def mla_projections(down, g_q, g_kv, w_uq_pad, w_ukv, seq, tm=512):
    t = down.shape[0]
    tm = _tile(seq, tm)
    n_heads = w_uq_pad.shape[1] // (2 * LANES)
    tables = _rope_tables(seq)
    n_pos = seq // tm
    tab_spec = pl.BlockSpec((tm, LANES), lambda i: (i % n_pos, 0))
    q = pl.pallas_call(
        functools.partial(_mla_q_kernel, n_heads=n_heads),
        grid=(t // tm,),
        in_specs=[
            pl.BlockSpec((tm, MLA_Q_RANK), lambda i: (i, 0)),
            pl.BlockSpec((1, MLA_Q_RANK), lambda i: (0, 0)),
            pl.BlockSpec(w_uq_pad.shape, lambda i: (0, 0)),
            tab_spec, tab_spec, tab_spec,
        ],
        out_specs=pl.BlockSpec((tm, w_uq_pad.shape[1]), lambda i: (i, 0)),
        out_shape=jax.ShapeDtypeStruct((t, w_uq_pad.shape[1]), BF16),
        compiler_params=_params("parallel"),
        name="mla_q_proj",
    )(down, g_q.reshape(1, -1), w_uq_pad, *tables)
    assert MLA_Q_RANK == MLA_KV_RANK and (MLA_Q_RANK + MLA_KV_RANK) % LANES == 0
    kv, k_rope = pl.pallas_call(
        functools.partial(_mla_kv_kernel, chunk=_tile(w_ukv.shape[1], 1024)),
        grid=(t // tm,),
        in_specs=[
            pl.BlockSpec((tm, MLA_KV_RANK), lambda i: (i, 1)),
            pl.BlockSpec((1, MLA_KV_RANK), lambda i: (0, 0)),
            pl.BlockSpec((tm, LANES), lambda i: (i, (MLA_Q_RANK + MLA_KV_RANK) // LANES)),
            pl.BlockSpec(w_ukv.shape, lambda i: (0, 0)),
            tab_spec, tab_spec, tab_spec,
        ],
        out_specs=[pl.BlockSpec((tm, w_ukv.shape[1]), lambda i: (i, 0)),
                   pl.BlockSpec((tm, LANES), lambda i: (i, 0))],
        out_shape=[jax.ShapeDtypeStruct((t, w_ukv.shape[1]), BF16), jax.ShapeDtypeStruct((t, LANES), BF16)],
        compiler_params=_params("parallel"),
        name="mla_kv_proj",
    )(down, g_kv.reshape(1, -1), down, w_ukv, *tables)
    return q, kv, k_rope


def _mla_attn_kernel(q_ref, kv_ref, kr_ref, o_ref, k_ref, v_ref, *, bq, bk, heads, scale):
    i = pl.program_id(2)
    width = 2 * LANES

    @pl.when(i == 0)
    def _():
        kr = kr_ref[0]
        krf = kr.astype(F32)
        kr_norm2 = jnp.sum(krf * krf, axis=-1, keepdims=True)
        lane = lax.broadcasted_iota(jnp.int32, kr.shape, 1)
        for hh in range(heads):
            kn = kv_ref[0, :, hh * width:hh * width + LANES]
            knf = kn.astype(F32)
            k_max = jnp.sqrt(jnp.max(jnp.sum(knf * knf, axis=-1, keepdims=True) + kr_norm2, axis=0, keepdims=True))
            k_ref[hh, :, :LANES] = kn
            k_ref[hh, :, LANES:] = jnp.where(lane == MLA_ROPE_DIM, (-NORM_MARGIN * k_max).astype(BF16), kr)
            v_ref[hh, :, :LANES] = kv_ref[0, :, hh * width + LANES:(hh + 1) * width]
            v_ref[hh, :, LANES:] = jnp.ones((kr.shape[0], LANES), BF16)

    n_diag = bq // bk
    n_before = i * n_diag
    row = lax.broadcasted_iota(jnp.int32, (bk, bk), 0)
    col = lax.broadcasted_iota(jnp.int32, (bk, bk), 1)
    causal = col <= row

    def shifted_logits(hh, m, first_row):
        kb = k_ref[hh, pl.ds(pl.multiple_of(m * bk, bk), bk), :]
        q = q_ref[0, first_row:, hh * width:(hh + 1) * width]
        return lax.dot_general(q, kb, _NT, preferred_element_type=F32)

    def values(hh, m):
        return v_ref[hh, pl.ds(pl.multiple_of(m * bk, bk), bk), :]

    def quick_step(hh, m, acc, dd):
        first_row = 0 if dd is None else dd * bk
        p = jnp.exp2(shifted_logits(hh, m, first_row) * (scale * math.log2(math.e)))
        if dd is not None:
            on_diagonal = jnp.where(causal, p[:bk], 0.0)
            p = on_diagonal if p.shape[0] == bk else jnp.concatenate([on_diagonal, p[bk:]], axis=0)
        pv = jnp.dot(p.astype(BF16), values(hh, m), preferred_element_type=F32)
        return acc + pv if first_row == 0 else jnp.concatenate([acc[:first_row], acc[first_row:] + pv], axis=0)

    acc = tuple(jnp.zeros((bq, width), F32) for _ in range(heads))
    acc = lax.fori_loop(
        0, n_before, lambda m, a: tuple(quick_step(hh, m, a[hh], None) for hh in range(heads)), acc)
    for dd in range(n_diag):
        acc = tuple(quick_step(hh, n_before + dd, acc[hh], dd) for hh in range(heads))
    smallest = functools.reduce(jnp.minimum, [jnp.min(acc[hh][:, LANES:]) for hh in range(heads)])
    for hh in range(heads):
        o_ref[0, :, hh * LANES:(hh + 1) * LANES] = (acc[hh][:, :LANES] / acc[hh][:, LANES:]).astype(o_ref.dtype)

    @pl.when(jnp.logical_not(smallest >= DENOMINATOR_FLOOR))
    def _():
        def careful_step(hh, m, state, dd):
            acc, mx, den = state
            z = shifted_logits(hh, m, 0) * scale
            if dd is not None:
                q_idx = lax.broadcasted_iota(jnp.int32, (bq, bk), 0)
                k_idx = lax.broadcasted_iota(jnp.int32, (bq, bk), 1)
                z = jnp.where(k_idx + dd * bk <= q_idx, z, -jnp.inf)
            new_mx = jnp.maximum(mx, jnp.max(z, axis=-1, keepdims=True))
            corr = jnp.exp(mx - new_mx)
            pz = jnp.exp(z - new_mx)
            den = den * corr + jnp.sum(pz, axis=-1, keepdims=True)
            pv = jnp.dot(pz.astype(BF16), values(hh, m)[:, :LANES], preferred_element_type=F32)
            return acc * corr + pv, new_mx, den

        for hh in range(heads):
            state = (jnp.zeros((bq, LANES), F32), jnp.full((bq, 1), -jnp.inf, F32), jnp.zeros((bq, 1), F32))
            state = lax.fori_loop(0, n_before, lambda m, c: careful_step(hh, m, c, None), state)
            for dd in range(n_diag):
                state = careful_step(hh, n_before + dd, state, dd)
            o_ref[0, :, hh * LANES:(hh + 1) * LANES] = (state[0] / state[2]).astype(o_ref.dtype)


def mla_attention(q, kv, k_rope, n_heads, bq=1024, bk=1024, heads_per_step=2):
    b, s, _ = q.shape
    bq = min(bq, s)
    bk = min(bk, bq)
    hp = heads_per_step
    assert s % bq == 0 and bq % bk == 0 and n_heads % hp == 0 and MLA_V_DIM == LANES
    scale = (MLA_NOPE_DIM + MLA_ROPE_DIM) ** -0.5
    return pl.pallas_call(
        functools.partial(_mla_attn_kernel, bq=bq, bk=bk, heads=hp, scale=scale),
        grid=(b, n_heads // hp, s // bq),
        in_specs=[
            pl.BlockSpec((1, bq, hp * 2 * LANES), lambda bi, h, i: (bi, i, h)),
            pl.BlockSpec((1, s, hp * 2 * LANES), lambda bi, h, i: (bi, 0, h)),
            pl.BlockSpec((1, s, LANES), lambda bi, h, i: (bi, 0, 0)),
        ],
        out_specs=pl.BlockSpec((1, bq, hp * LANES), lambda bi, h, i: (bi, i, h)),
        out_shape=jax.ShapeDtypeStruct((b, s, n_heads * MLA_V_DIM), BF16),
        scratch_shapes=[pltpu.VMEM((hp, s, 2 * LANES), BF16), pltpu.VMEM((hp, s, 2 * LANES), BF16)],
        compiler_params=_params("parallel", "parallel", "arbitrary"),
        name="mla_attention",
    )(q, kv, k_rope)


def _extract_top(xs, out_refs, k):
    def drop_all(r, ys):
        out = []
        for y, out_ref in zip(ys, out_refs):
            m = jnp.max(y, axis=0, keepdims=True)
            out_ref[pl.ds(r, 1), :] = m
            out.append(jnp.where(y == m, -jnp.inf, y))
        return tuple(out)

    lax.fori_loop(0, k, drop_all, tuple(xs))
    most = [jnp.max(jnp.sum((x >= out_ref[k - 1:k, :]).astype(F32), axis=0, keepdims=True))
            for x, out_ref in zip(xs, out_refs)]

    @pl.when(functools.reduce(jnp.maximum, most) > k)
    def _():
        def drop_first(r, ys):
            out = []
            for y, out_ref in zip(ys, out_refs):
                rows = lax.broadcasted_iota(jnp.int32, y.shape, 0).astype(F32)
                m = jnp.max(y, axis=0, keepdims=True)
                out_ref[pl.ds(r, 1), :] = m
                first = jnp.min(jnp.where(y == m, rows, float(y.shape[0])), axis=0, keepdims=True)
                out.append(jnp.where(rows == first, -jnp.inf, y))
            return tuple(out)

        lax.fori_loop(0, k, drop_first, tuple(xs))


def _peer_select_kernel(qh_ref, keys_ref, st_ref, et_ref, thr_ref, v_ref, top_ref, *, heads, n_keys, key_half, topk):
    halves = []
    for hh in range(heads):
        for half in range(2):
            lanes = slice((2 * hh + half) * key_half, (2 * hh + half + 1) * key_half)
            s_t = lax.dot_general(keys_ref[half], qh_ref[:, lanes], _NT, preferred_element_type=F32)
            st_ref[hh, half * n_keys:(half + 1) * n_keys, :] = s_t
            halves.append(s_t)
    for c in range(2 * heads):
        _extract_top([halves[c]], [v_ref.at[c]], topk)
    assert topk == 16
    candidates = []
    for hh in range(heads):
        v1, v2 = v_ref[2 * hh], v_ref[2 * hh + 1]
        slabs = [v1 + v2[0:1]]
        slabs += [v1[0:8] + v2[b:b + 1] for b in range(1, 8)]
        slabs += [v1[0:1] + v2[8:16]]
        candidates.append(jnp.concatenate(slabs, axis=0))
    for hh in range(heads):
        _extract_top([candidates[hh]], [top_ref.at[hh]], topk)
    for hh in range(heads):
        top = top_ref[hh]
        thr_ref[hh] = top[topk - 1:topk]
        z_sum = jnp.sum(jnp.exp(top - top[0:1]), axis=0, keepdims=True)
        et_ref[hh, :n_keys, :] = jnp.exp(halves[2 * hh] - v_ref[2 * hh][0:1])
        et_ref[hh, n_keys:, :] = jnp.exp(halves[2 * hh + 1] - v_ref[2 * hh + 1][0:1]) / z_sum


def peer_select(qh, sub_keys, tt=512, heads_per_step=2):
    t = qh.shape[0]
    _, n_keys, key_half = sub_keys.shape
    n_heads = qh.shape[1] // (2 * key_half)
    tt = _tile(t, tt)
    hp = heads_per_step
    assert n_heads % hp == 0
    kern = functools.partial(_peer_select_kernel, heads=hp, n_keys=n_keys, key_half=key_half, topk=PEER_TOPK)
    return pl.pallas_call(
        kern,
        grid=(t // tt, n_heads // hp),
        in_specs=[
            pl.BlockSpec((tt, hp * 2 * key_half), lambda i, h: (i, h)),
            pl.BlockSpec(sub_keys.shape, lambda i, h: (0, 0, 0)),
        ],
        out_specs=[
            pl.BlockSpec((hp, 2 * n_keys, tt), lambda i, h: (h, 0, i)),
            pl.BlockSpec((hp, 2 * n_keys, tt), lambda i, h: (h, 0, i)),
            pl.BlockSpec((hp, 1, tt), lambda i, h: (h, 0, i)),
        ],
        out_shape=[
            jax.ShapeDtypeStruct((n_heads, 2 * n_keys, t), F32),
            jax.ShapeDtypeStruct((n_heads, 2 * n_keys, t), F32),
            jax.ShapeDtypeStruct((n_heads, 1, t), F32),
        ],
        scratch_shapes=[pltpu.VMEM((2 * hp, PEER_TOPK, tt), F32), pltpu.VMEM((hp, PEER_TOPK, tt), F32)],
        compiler_params=_params("parallel", "arbitrary"),
        name="peer_select",
    )(qh, sub_keys)


def _gelu_tanh(x):
    return 0.5 * x * (1.0 + jnp.tanh(math.sqrt(2.0 / math.pi) * (x + 0.044715 * (x * x * x))))


def _peer_dense_kernel(h_ref, g_ref, u_ref, v_ref, st_ref, et_ref, thr_ref, o_ref, xn_ref, *, n_heads, n_keys, te):
    e = pl.program_id(1)

    @pl.when(e == 0)
    def _():
        x = h_ref[...]
        xn_ref[...] = _rms_scale(x, g_ref[...]).astype(BF16)
        o_ref[...] = x

    hidden = lax.dot_general(xn_ref[...], u_ref[...], _NT, preferred_element_type=F32)
    rows_per_step = te // n_keys
    blocks = []
    for ii in range(rows_per_step):
        i_row = e * rows_per_step + ii
        gate_t = None
        for h in range(n_heads):
            score = st_ref[h, n_keys:2 * n_keys, :] + st_ref[h, pl.ds(i_row, 1), :]
            weight = et_ref[h, n_keys:2 * n_keys, :] * et_ref[h, pl.ds(i_row, 1), :]
            g = jnp.where(score >= thr_ref[h], weight, 0.0)
            gate_t = g if gate_t is None else gate_t + g
        blocks.append(gate_t)
    gate = jnp.concatenate(blocks, axis=0).T
    gh = (gate * _gelu_tanh(hidden)).astype(BF16)
    o_ref[...] += jnp.dot(gh, v_ref[...], preferred_element_type=F32)


def peer_dense(h, g, u, v, scores_t, factors_t, thr, tt=512, te=512):
    t, d = h.shape
    n_exp = u.shape[0]
    n_heads, two_keys, _ = scores_t.shape
    n_keys = two_keys // 2
    tt, te = _tile(t, tt), _tile(n_exp, te)
    assert te % n_keys == 0 and n_exp == n_keys * n_keys
    kern = functools.partial(_peer_dense_kernel, n_heads=n_heads, n_keys=n_keys, te=te)
    return pl.pallas_call(
        kern,
        grid=(t // tt, n_exp // te),
        in_specs=[
            pl.BlockSpec((tt, d), lambda i, e: (i, 0)),
            pl.BlockSpec((1, d), lambda i, e: (0, 0)),
            pl.BlockSpec((te, d), lambda i, e: (e, 0)),
            pl.BlockSpec((te, d), lambda i, e: (e, 0)),
            pl.BlockSpec((n_heads, two_keys, tt), lambda i, e: (0, 0, i)),
            pl.BlockSpec((n_heads, two_keys, tt), lambda i, e: (0, 0, i)),
            pl.BlockSpec((n_heads, 1, tt), lambda i, e: (0, 0, i)),
        ],
        out_specs=pl.BlockSpec((tt, d), lambda i, e: (i, 0)),
        out_shape=jax.ShapeDtypeStruct((t, d), F32),
        scratch_shapes=[pltpu.VMEM((tt, d), BF16)],
        compiler_params=_params("parallel", "arbitrary"),
        name="peer_dense",
    )(h, g.reshape(1, d), u, v, scores_t, factors_t, thr)


def _ple_kernel(x_ref, g_ref, wg_ref, p_ref, wp_ref, r_ref, o_ref, xn_ref, pb_ref):
    @pl.when(pl.program_id(1) == 0)
    def _():
        xn_ref[...] = _rms_scale(x_ref[...], g_ref[...]).astype(BF16)
        pb_ref[...] = p_ref[...].astype(BF16)

    gate = 1.0 / (1.0 + jnp.exp(-jnp.dot(xn_ref[...], wg_ref[...], preferred_element_type=F32)))
    o_ref[...] = r_ref[...] + jnp.dot(pb_ref[...], wp_ref[...], preferred_element_type=F32) * gate


def per_layer_embedding(h, p, g, w_g, w_p, tm=1024, tn=512):
    t, d = h.shape
    pd = p.shape[1]
    tm, tn = _tile(t, tm), _tile(d, tn)
    return pl.pallas_call(
        _ple_kernel,
        grid=(t // tm, d // tn),
        in_specs=[
            pl.BlockSpec((tm, d), lambda i, j: (i, 0)),
            pl.BlockSpec((1, d), lambda i, j: (0, 0)),
            pl.BlockSpec((d, tn), lambda i, j: (0, j)),
            pl.BlockSpec((tm, pd), lambda i, j: (i, 0)),
            pl.BlockSpec((pd, tn), lambda i, j: (0, j)),
            pl.BlockSpec((tm, tn), lambda i, j: (i, j)),
        ],
        out_specs=pl.BlockSpec((tm, tn), lambda i, j: (i, j)),
        out_shape=jax.ShapeDtypeStruct((t, d), F32),
        scratch_shapes=[pltpu.VMEM((tm, d), BF16), pltpu.VMEM((tm, pd), BF16)],
        compiler_params=_params("parallel", "arbitrary"),
        name="per_layer_embedding",
    )(h, g.reshape(1, d), w_g, p, w_p, h)


def _pad_cols(w, multiple):
    extra = (-w.shape[-1]) % multiple
    return jnp.pad(w, ((0, 0), (0, extra))) if extra else w


def kernel(x, p, attn_norm, ffn_norm, ple_norm, final_norm, sb_w_qkv, sb_w_o, swa_w_qkv, swa_w_o, swa_sinks,
           mla_w_down, mla_q_norm, mla_kv_norm, mla_w_uq, mla_w_ukv, mla_w_o,
           peer_w_q, peer_sub_keys, peer_u, peer_v, ple_w_p, ple_w_g):
    b, s, d = x.shape
    t = b * s
    depth = p.shape[0]
    h = x.reshape(t, d)
    for i in range(depth):
        kind, j = i % N_MIXERS, i // N_MIXERS
        if kind == 0:
            qkv = norm_matmul(h, attn_norm[i], sb_w_qkv[j].astype(BF16), BF16)
            o = sb_attention(qkv.reshape(b, s, -1), SB_HEADS)
            w_o = sb_w_o[j]
        elif kind == 1:
            qkv = norm_matmul(h, attn_norm[i], swa_w_qkv[j].astype(BF16), BF16, tn=1280)
            head_dim = swa_w_o.shape[1] // SWA_HEADS
            o = swa_attention(qkv.reshape(b, s, -1), swa_sinks[j], SWA_HEADS, SWA_KV_HEADS, head_dim)
            w_o = swa_w_o[j]
        else:
            down = norm_matmul(h, attn_norm[i], _pad_cols(mla_w_down[j], LANES).astype(BF16), F32, tn=2048)
            qk_dim = MLA_NOPE_DIM + MLA_ROPE_DIM
            w_uq_pad = jnp.pad(mla_w_uq[j].reshape(MLA_Q_RANK, MLA_HEADS, qk_dim),
                               ((0, 0), (0, 0), (0, 2 * LANES - qk_dim))).reshape(MLA_Q_RANK, -1)
            q, kv, k_rope = mla_projections(down, mla_q_norm[j], mla_kv_norm[j], w_uq_pad.astype(BF16),
                                            mla_w_ukv[j].astype(BF16), s)
            o = mla_attention(q.reshape(b, s, -1), kv.reshape(b, s, -1), k_rope.reshape(b, s, -1), MLA_HEADS)
            w_o = mla_w_o[j]
        h = matmul_residual(o.reshape(t, -1), w_o.astype(BF16), h)
        qh = norm_matmul(h, ffn_norm[i], peer_w_q[i].astype(BF16), BF16)
        scores_t, factors_t, thr = peer_select(qh, peer_sub_keys[i].astype(BF16))
        h = peer_dense(h, ffn_norm[i], peer_u[i].astype(BF16), peer_v[i].astype(BF16), scores_t, factors_t, thr)
        h = per_layer_embedding(h, p[i].reshape(t, -1), ple_norm[i], ple_w_g[i].astype(BF16), ple_w_p[i].astype(BF16))
    return final_rms_norm(h, final_norm).reshape(b, s, d)
```

```python
import functools
import math

import jax
import jax.numpy as jnp
from jax import lax
from jax.experimental import pallas as pl
from jax.experimental.pallas import tpu as pltpu

F32 = jnp.float32
BF16 = jnp.bfloat16

RMS_EPS = 1e-6
N_MIXERS = 3
SB_HEADS = 16
SWA_HEADS = 32
SWA_KV_HEADS = 4
SWA_WINDOW = 128
MLA_HEADS = 16
MLA_Q_RANK = 512
MLA_KV_RANK = 512
MLA_NOPE_DIM = 128
MLA_ROPE_DIM = 64
MLA_V_DIM = 128
ROPE_THETA = 10000.0
PEER_N_KEYS = 64
PEER_TOPK = 16

NORM_MARGIN = 1.02
DENOMINATOR_FLOOR = 2.0 ** -80
EXP2_UNDERFLOW = -150.0
LANES = 128
V7X_VMEM_BYTES = 64 * 1024 * 1024
VMEM_LIMIT = V7X_VMEM_BYTES * 7 // 8

_NT = (((1,), (1,)), ((), ()))


def _params(*semantics):
    return pltpu.CompilerParams(dimension_semantics=semantics, vmem_limit_bytes=VMEM_LIMIT)


def _tile(n, want):
    if n <= want:
        return n
    t = (want // LANES) * LANES
    while t > LANES and n % t:
        t -= LANES
    assert n % t == 0, (n, want)
    return t


def _rms_scale(x, g):
    ms = jnp.mean(x * x, axis=-1, keepdims=True)
    return x * lax.rsqrt(ms + RMS_EPS) * g


def _norm_matmul_kernel(x_ref, g_ref, w_ref, o_ref, xn_ref):
    @pl.when(pl.program_id(1) == 0)
    def _():
        xn_ref[...] = _rms_scale(x_ref[...], g_ref[...]).astype(BF16)

    o_ref[...] = jnp.dot(xn_ref[...], w_ref[...], preferred_element_type=F32).astype(o_ref.dtype)


def norm_matmul(x, g, w, out_dtype, tm=1024, tn=1024):
    m, k = x.shape
    n = w.shape[1]
    tm, tn = _tile(m, tm), _tile(n, tn)
    return pl.pallas_call(
        _norm_matmul_kernel,
        grid=(m // tm, n // tn),
        in_specs=[
            pl.BlockSpec((tm, k), lambda i, j: (i, 0)),
            pl.BlockSpec((1, k), lambda i, j: (0, 0)),
            pl.BlockSpec((k, tn), lambda i, j: (0, j)),
        ],
        out_specs=pl.BlockSpec((tm, tn), lambda i, j: (i, j)),
        out_shape=jax.ShapeDtypeStruct((m, n), out_dtype),
        scratch_shapes=[pltpu.VMEM((tm, k), BF16)],
        compiler_params=_params("parallel", "arbitrary"),
        name="norm_matmul",
    )(x, g.reshape(1, k), w)


def _matmul_residual_kernel(a_ref, w_ref, r_ref, o_ref):
    o_ref[...] = r_ref[...] + jnp.dot(a_ref[...], w_ref[...], preferred_element_type=F32)


def matmul_residual(a, w, res, tm=1024, tn=1024):
    m, k = a.shape
    n = w.shape[1]
    tm, tn = _tile(m, tm), _tile(n, tn)
    return pl.pallas_call(
        _matmul_residual_kernel,
        grid=(m // tm, n // tn),
        in_specs=[
            pl.BlockSpec((tm, k), lambda i, j: (i, 0)),
            pl.BlockSpec((k, tn), lambda i, j: (0, j)),
            pl.BlockSpec((tm, tn), lambda i, j: (i, j)),
        ],
        out_specs=pl.BlockSpec((tm, tn), lambda i, j: (i, j)),
        out_shape=jax.ShapeDtypeStruct((m, n), F32),
        compiler_params=_params("parallel", "arbitrary"),
        name="matmul_residual",
    )(a, w, res)


def _final_norm_kernel(x_ref, g_ref, o_ref):
    o_ref[...] = _rms_scale(x_ref[...], g_ref[...])


def final_rms_norm(x, g, tm=1024):
    m, k = x.shape
    tm = _tile(m, tm)
    return pl.pallas_call(
        _final_norm_kernel,
        grid=(m // tm,),
        in_specs=[pl.BlockSpec((tm, k), lambda i: (i, 0)), pl.BlockSpec((1, k), lambda i: (0, 0))],
        out_specs=pl.BlockSpec((tm, k), lambda i: (i, 0)),
        out_shape=jax.ShapeDtypeStruct((m, k), F32),
        compiler_params=_params("parallel"),
        name="final_norm",
    )(x, g.reshape(1, k))


def _sb_attn_kernel(q_ref, k_ref, v_ref, o_ref, kmax2_ref, *, bq, bk, d, heads, scale):
    i = pl.program_id(2)
    n_diag = bq // bk
    row = lax.broadcasted_iota(jnp.int32, (bk, bk), 0)
    col = lax.broadcasted_iota(jnp.int32, (bk, bk), 1)
    suffix_ones = (row >= col).astype(BF16)
    q_idx = lax.broadcasted_iota(jnp.int32, (bq, bk), 0)
    k_idx = lax.broadcasted_iota(jnp.int32, (bq, bk), 1)

    def scores(hh, m):
        lanes = slice(hh * d, (hh + 1) * d)
        kb = k_ref[0, pl.ds(pl.multiple_of(m * bk, bk), bk), lanes]
        return lax.dot_general(q_ref[0, :, lanes], kb, _NT, preferred_element_type=F32)

    def absorb(hh, m, qk, acc, log_run, key_offset):
        vb = v_ref[0, pl.ds(pl.multiple_of(m * bk, bk), bk), hh * d:(hh + 1) * d]
        z = qk * scale
        neg_z = -z
        log_keep = jnp.minimum(neg_z, 0.0) - jnp.log2(1.0 + jnp.exp2(jnp.minimum(z, neg_z)))
        if key_offset is not None:
            strict = k_idx + key_offset < q_idx
            log_keep = jnp.where(strict, log_keep, 0.0)
        incl = jnp.dot(log_keep.astype(BF16), suffix_ones, preferred_element_type=F32)
        logw = z + incl + log_run
        if key_offset is not None:
            logw = jnp.where(strict, logw, -jnp.inf)
        p = jnp.exp2(logw)
        acc = acc + jnp.dot(p.astype(BF16), vb, preferred_element_type=F32)
        return acc, log_run + incl[:, 0:1]

    def step(m, state, key_offset):
        return tuple(absorb(hh, m, scores(hh, m), *state[hh], key_offset) for hh in range(heads))

    @pl.when(i == 0)
    def _():
        for hh in range(heads):
            kf = k_ref[0, :, hh * d:(hh + 1) * d].astype(F32)
            k_norm2 = jnp.max(jnp.sum(kf * kf, axis=-1, keepdims=True), axis=0, keepdims=True)
            kmax2_ref[hh] = jnp.broadcast_to(k_norm2, kmax2_ref.shape[1:])

    logit_bound = []
    for hh in range(heads):
        qf = q_ref[0, :, hh * d:(hh + 1) * d].astype(F32)
        q_norm2 = jnp.sum(qf * qf, axis=-1, keepdims=True)
        logit_bound.append(jnp.sqrt(q_norm2 * kmax2_ref[hh][0:1, 0:1]) * (scale * 1.01) + 1.0)

    def any_weight_left(state):
        worst = [jnp.max(state[hh][1] + logit_bound[hh]) for hh in range(heads)]
        return (functools.reduce(jnp.maximum, worst) >= EXP2_UNDERFLOW).astype(jnp.int32)

    state = tuple((jnp.zeros((bq, d), F32), jnp.zeros((bq, 1), F32)) for _ in range(heads))
    for dd in reversed(range(n_diag)):
        state = step(i * n_diag + dd, state, dd * bk)
    n_before = i * n_diag

    def more(c):
        t, go, _ = c
        return jnp.logical_and(t < n_before, go > 0)

    def walk(c):
        t, _, state = c
        state = step(n_before - 1 - t, state, None)
        return t + 1, any_weight_left(state), state

    _, _, state = lax.while_loop(more, walk, (jnp.int32(0), any_weight_left(state), state))
    for hh in range(heads):
        o_ref[0, :, hh * d:(hh + 1) * d] = state[hh][0].astype(o_ref.dtype)


def sb_attention(qkv, n_heads, bq=512, bk=256, heads_per_step=2):
    b, s, three_hd = qkv.shape
    d = three_hd // (3 * n_heads)
    assert d == LANES
    bq = min(bq, s)
    bk = min(bk, bq)
    hp = heads_per_step
    assert s % bq == 0 and bq % bk == 0 and n_heads % hp == 0
    n_grp = n_heads // hp
    kern = functools.partial(_sb_attn_kernel, bq=bq, bk=bk, d=d, heads=hp, scale=d ** -0.5 * math.log2(math.e))
    return pl.pallas_call(
        kern,
        grid=(b, n_grp, s // bq),
        in_specs=[
            pl.BlockSpec((1, bq, hp * d), lambda bi, h, i: (bi, i, h)),
            pl.BlockSpec((1, s, hp * d), lambda bi, h, i: (bi, 0, n_grp + h)),
            pl.BlockSpec((1, s, hp * d), lambda bi, h, i: (bi, 0, 2 * n_grp + h)),
        ],
        out_specs=pl.BlockSpec((1, bq, hp * d), lambda bi, h, i: (bi, i, h)),
        out_shape=jax.ShapeDtypeStruct((b, s, n_heads * d), BF16),
        scratch_shapes=[pltpu.VMEM((hp, 8, LANES), F32)],
        compiler_params=_params("parallel", "parallel", "arbitrary"),
        name="sb_attention",
    )(qkv, qkv, qkv)


def _swa_kernel(q_ref, kp_ref, kc_ref, vp_ref, vc_ref, slope_ref, sink_ref, o_ref, *, n_kv, grp, d, blk, scale):
    i = pl.program_id(1)
    rows = grp * blk
    qpos = lax.broadcasted_iota(jnp.int32, (rows, 2 * blk), 0) & (blk - 1)
    kcol = lax.broadcasted_iota(jnp.int32, (rows, 2 * blk), 1)
    dist = qpos + blk - kcol
    first_key = jnp.where(i > 0, 0, blk)
    in_window = (dist >= 0) & (dist < SWA_WINDOW) & (kcol >= first_key)
    dist_f = dist.astype(F32)
    outs = []
    for kh in range(n_kv):
        qg = jnp.concatenate(
            [q_ref[0, :, (kh * grp + g) * d:(kh * grp + g + 1) * d] for g in range(grp)], axis=0)
        kk = jnp.concatenate([kp_ref[0, :, kh * d:(kh + 1) * d], kc_ref[0, :, kh * d:(kh + 1) * d]], axis=0)
        vv = jnp.concatenate([vp_ref[0, :, kh * d:(kh + 1) * d], vc_ref[0, :, kh * d:(kh + 1) * d]], axis=0)
        z = lax.dot_general(qg, kk, _NT, preferred_element_type=F32) * scale
        z = z - slope_ref[kh] * dist_f
        z = jnp.where(in_window, z, -jnp.inf)
        sink = sink_ref[kh]
        mx = jnp.maximum(jnp.max(z, axis=-1, keepdims=True), sink)
        p = jnp.exp(z - mx)
        den = jnp.sum(p, axis=-1, keepdims=True) + jnp.exp(sink - mx)
        o = jnp.dot(p.astype(BF16), vv, preferred_element_type=F32) / den
        outs.extend(o[g * blk:(g + 1) * blk] for g in range(grp))
    o_ref[0] = jnp.concatenate(outs, axis=1).astype(o_ref.dtype)


def swa_attention(qkv, sinks, n_heads, n_kv, d):
    b, s, _ = qkv.shape
    blk = SWA_WINDOW
    grp = n_heads // n_kv
    qd, kd = n_heads * d, n_kv * d
    assert qd % kd == 0 and kd % LANES == 0 and s % blk == 0
    slopes = 2.0 ** (-8.0 * jnp.arange(1, n_heads + 1, dtype=F32) / n_heads)
    per_row = lambda a: jnp.repeat(a.astype(F32).reshape(n_kv, grp), blk, axis=1).reshape(n_kv, grp * blk, 1)
    kern = functools.partial(_swa_kernel, n_kv=n_kv, grp=grp, d=d, blk=blk, scale=d ** -0.5)
    prev = lambda bi, i: (bi, jnp.maximum(i - 1, 0), qd // kd)
    cur = lambda bi, i: (bi, i, qd // kd)
    prev_v = lambda bi, i: (bi, jnp.maximum(i - 1, 0), qd // kd + 1)
    cur_v = lambda bi, i: (bi, i, qd // kd + 1)
    const = lambda bi, i: (0, 0, 0)
    return pl.pallas_call(
        kern,
        grid=(b, s // blk),
        in_specs=[
            pl.BlockSpec((1, blk, qd), lambda bi, i: (bi, i, 0)),
            pl.BlockSpec((1, blk, kd), prev),
            pl.BlockSpec((1, blk, kd), cur),
            pl.BlockSpec((1, blk, kd), prev_v),
            pl.BlockSpec((1, blk, kd), cur_v),
            pl.BlockSpec((n_kv, grp * blk, 1), const),
            pl.BlockSpec((n_kv, grp * blk, 1), const),
        ],
        out_specs=pl.BlockSpec((1, blk, qd), lambda bi, i: (bi, i, 0)),
        out_shape=jax.ShapeDtypeStruct((b, s, qd), BF16),
        compiler_params=_params("parallel", "arbitrary"),
        name="swa_attention",
    )(qkv, qkv, qkv, qkv, qkv, per_row(slopes), per_row(sinks))


def _rope_lanes(r, cos_ref, sin_lo_ref, sin_hi_ref):
    half = MLA_ROPE_DIM // 2
    return (r * cos_ref[...]
            + pltpu.roll(r, LANES - half, 1) * sin_lo_ref[...]
            + pltpu.roll(r, half, 1) * sin_hi_ref[...])


def _mla_q_kernel(x_ref, g_ref, w_ref, cos_ref, slo_ref, shi_ref, o_ref, *, n_heads):
    xn = _rms_scale(x_ref[...], g_ref[...]).astype(BF16)
    lane = lax.broadcasted_iota(jnp.int32, (x_ref.shape[0], LANES), 1)
    for h in range(n_heads):
        a = jnp.dot(xn, w_ref[:, 2 * LANES * h:2 * LANES * (h + 1)], preferred_element_type=F32)
        nope = a[:, :LANES]
        rope = _rope_lanes(a[:, LANES:], cos_ref, slo_ref, shi_ref)
        norm2 = jnp.sum(nope * nope + rope * rope, axis=-1, keepdims=True)
        o_ref[:, 2 * LANES * h:2 * LANES * h + LANES] = nope.astype(BF16)
        o_ref[:, 2 * LANES * h + LANES:2 * LANES * (h + 1)] = jnp.where(
            lane == MLA_ROPE_DIM, NORM_MARGIN * jnp.sqrt(norm2), rope).astype(BF16)


def _mla_kv_kernel(x_ref, g_ref, xr_ref, w_ref, cos_ref, slo_ref, shi_ref, kv_ref, kr_ref, *, chunk):
    xn = _rms_scale(x_ref[...], g_ref[...]).astype(BF16)
    for c in range(w_ref.shape[1] // chunk):
        cols = slice(c * chunk, (c + 1) * chunk)
        kv_ref[:, cols] = jnp.dot(xn, w_ref[:, cols], preferred_element_type=F32).astype(BF16)
    kr_ref[...] = _rope_lanes(xr_ref[...], cos_ref, slo_ref, shi_ref).astype(BF16)


def _rope_tables(s):
    half = MLA_ROPE_DIM // 2
    inv_freq = ROPE_THETA ** (-jnp.arange(half, dtype=F32) / half)
    ang = jnp.arange(s, dtype=F32)[:, None] * inv_freq[None, :]
    cos, sin = jnp.cos(ang), jnp.sin(ang)
    zeros = jnp.zeros((s, LANES - 2 * half), F32)
    z_half = jnp.zeros((s, half), F32)
    cos_t = jnp.concatenate([cos, cos, zeros], axis=1)
    sin_lo = jnp.concatenate([-sin, z_half, zeros], axis=1)
    sin_hi = jnp.concatenate([z_half, sin, zeros], axis=1)
    return cos_t, sin_lo, sin_hi


def mla_projections(down, g_q, g_kv, w_uq_pad, w_ukv, seq, tm=512):
    t = down.shape[0]
    tm = _tile(seq, tm)
    n_heads = w_uq_pad.shape[1] // (2 * LANES)
    tables = _rope_tables(seq)
    n_pos = seq // tm
    tab_spec = pl.BlockSpec((tm, LANES), lambda i: (i % n_pos, 0))
    q = pl.pallas_call(
        functools.partial(_mla_q_kernel, n_heads=n_heads),
        grid=(t // tm,),
        in_specs=[
            pl.BlockSpec((tm, MLA_Q_RANK), lambda i: (i, 0)),
            pl.BlockSpec((1, MLA_Q_RANK), lambda i: (0, 0)),
            pl.BlockSpec(w_uq_pad.shape, lambda i: (0, 0)),
            tab_spec, tab_spec, tab_spec,
        ],
        out_specs=pl.BlockSpec((tm, w_uq_pad.shape[1]), lambda i: (i, 0)),
        out_shape=jax.ShapeDtypeStruct((t, w_uq_pad.shape[1]), BF16),
        compiler_params=_params("parallel"),
        name="mla_q_proj",
    )(down, g_q.reshape(1, -1), w_uq_pad, *tables)
    assert MLA_Q_RANK == MLA_KV_RANK and (MLA_Q_RANK + MLA_KV_RANK) % LANES == 0
    kv, k_rope = pl.pallas_call(
        functools.partial(_mla_kv_kernel, chunk=_tile(w_ukv.shape[1], 1024)),
        grid=(t // tm,),
        in_specs=[
            pl.BlockSpec((tm, MLA_KV_RANK), lambda i: (i, 1)),
            pl.BlockSpec((1, MLA_KV_RANK), lambda i: (0, 0)),
            pl.BlockSpec((tm, LANES), lambda i: (i, (MLA_Q_RANK + MLA_KV_RANK) // LANES)),
            pl.BlockSpec(w_ukv.shape, lambda i: (0, 0)),
            tab_spec, tab_spec, tab_spec,
        ],
        out_specs=[pl.BlockSpec((tm, w_ukv.shape[1]), lambda i: (i, 0)),
                   pl.BlockSpec((tm, LANES), lambda i: (i, 0))],
        out_shape=[jax.ShapeDtypeStruct((t, w_ukv.shape[1]), BF16), jax.ShapeDtypeStruct((t, LANES), BF16)],
        compiler_params=_params("parallel"),
        name="mla_kv_proj",
    )(down, g_kv.reshape(1, -1), down, w_ukv, *tables)
    return q, kv, k_rope


def _mla_attn_kernel(q_ref, kv_ref, kr_ref, o_ref, k_ref, v_ref, *, bq, bk, heads, scale):
    i = pl.program_id(2)
    width = 2 * LANES

    @pl.when(i == 0)
    def _():
        kr = kr_ref[0]
        krf = kr.astype(F32)
        kr_norm2 = jnp.sum(krf * krf, axis=-1, keepdims=True)
        lane = lax.broadcasted_iota(jnp.int32, kr.shape, 1)
        for hh in range(heads):
            kn = kv_ref[0, :, hh * width:hh * width + LANES]
            knf = kn.astype(F32)
            k_max = jnp.sqrt(jnp.max(jnp.sum(knf * knf, axis=-1, keepdims=True) + kr_norm2, axis=0, keepdims=True))
            k_ref[hh, :, :LANES] = kn
            k_ref[hh, :, LANES:] = jnp.where(lane == MLA_ROPE_DIM, (-NORM_MARGIN * k_max).astype(BF16), kr)
            v_ref[hh, :, :LANES] = kv_ref[0, :, hh * width + LANES:(hh + 1) * width]
            v_ref[hh, :, LANES:] = jnp.ones((kr.shape[0], LANES), BF16)

    n_diag = bq // bk
    n_before = i * n_diag
    row = lax.broadcasted_iota(jnp.int32, (bk, bk), 0)
    col = lax.broadcasted_iota(jnp.int32, (bk, bk), 1)
    causal = col <= row

    def shifted_logits(hh, m, first_row):
        kb = k_ref[hh, pl.ds(pl.multiple_of(m * bk, bk), bk), :]
        q = q_ref[0, first_row:, hh * width:(hh + 1) * width]
        return lax.dot_general(q, kb, _NT, preferred_element_type=F32)

    def values(hh, m):
        return v_ref[hh, pl.ds(pl.multiple_of(m * bk, bk), bk), :]

    def quick_step(hh, m, acc, dd):
        first_row = 0 if dd is None else dd * bk
        p = jnp.exp2(shifted_logits(hh, m, first_row) * (scale * math.log2(math.e)))
        if dd is not None:
            on_diagonal = jnp.where(causal, p[:bk], 0.0)
            p = on_diagonal if p.shape[0] == bk else jnp.concatenate([on_diagonal, p[bk:]], axis=0)
        pv = jnp.dot(p.astype(BF16), values(hh, m), preferred_element_type=F32)
        return acc + pv if first_row == 0 else jnp.concatenate([acc[:first_row], acc[first_row:] + pv], axis=0)

    acc = tuple(jnp.zeros((bq, width), F32) for _ in range(heads))
    acc = lax.fori_loop(
        0, n_before, lambda m, a: tuple(quick_step(hh, m, a[hh], None) for hh in range(heads)), acc)
    for dd in range(n_diag):
        acc = tuple(quick_step(hh, n_before + dd, acc[hh], dd) for hh in range(heads))
    smallest = functools.reduce(jnp.minimum, [jnp.min(acc[hh][:, LANES:]) for hh in range(heads)])
    for hh in range(heads):
        o_ref[0, :, hh * LANES:(hh + 1) * LANES] = (acc[hh][:, :LANES] / acc[hh][:, LANES:]).astype(o_ref.dtype)

    @pl.when(jnp.logical_not(smallest >= DENOMINATOR_FLOOR))
    def _():
        def careful_step(hh, m, state, dd):
            acc, mx, den = state
            z = shifted_logits(hh, m, 0) * scale
            if dd is not None:
                q_idx = lax.broadcasted_iota(jnp.int32, (bq, bk), 0)
                k_idx = lax.broadcasted_iota(jnp.int32, (bq, bk), 1)
                z = jnp.where(k_idx + dd * bk <= q_idx, z, -jnp.inf)
            new_mx = jnp.maximum(mx, jnp.max(z, axis=-1, keepdims=True))
            corr = jnp.exp(mx - new_mx)
            pz = jnp.exp(z - new_mx)
            den = den * corr + jnp.sum(pz, axis=-1, keepdims=True)
            pv = jnp.dot(pz.astype(BF16), values(hh, m)[:, :LANES], preferred_element_type=F32)
            return acc * corr + pv, new_mx, den

        for hh in range(heads):
            state = (jnp.zeros((bq, LANES), F32), jnp.full((bq, 1), -jnp.inf, F32), jnp.zeros((bq, 1), F32))
            state = lax.fori_loop(0, n_before, lambda m, c: careful_step(hh, m, c, None), state)
            for dd in range(n_diag):
                state = careful_step(hh, n_before + dd, state, dd)
            o_ref[0, :, hh * LANES:(hh + 1) * LANES] = (state[0] / state[2]).astype(o_ref.dtype)


def mla_attention(q, kv, k_rope, n_heads, bq=1024, bk=1024, heads_per_step=2):
    b, s, _ = q.shape
    bq = min(bq, s)
    bk = min(bk, bq)
    hp = heads_per_step
    assert s % bq == 0 and bq % bk == 0 and n_heads % hp == 0 and MLA_V_DIM == LANES
    scale = (MLA_NOPE_DIM + MLA_ROPE_DIM) ** -0.5
    return pl.pallas_call(
        functools.partial(_mla_attn_kernel, bq=bq, bk=bk, heads=hp, scale=scale),
        grid=(b, n_heads // hp, s // bq),
        in_specs=[
            pl.BlockSpec((1, bq, hp * 2 * LANES), lambda bi, h, i: (bi, i, h)),
            pl.BlockSpec((1, s, hp * 2 * LANES), lambda bi, h, i: (bi, 0, h)),
            pl.BlockSpec((1, s, LANES), lambda bi, h, i: (bi, 0, 0)),
        ],
        out_specs=pl.BlockSpec((1, bq, hp * LANES), lambda bi, h, i: (bi, i, h)),
        out_shape=jax.ShapeDtypeStruct((b, s, n_heads * MLA_V_DIM), BF16),
        scratch_shapes=[pltpu.VMEM((hp, s, 2 * LANES), BF16), pltpu.VMEM((hp, s, 2 * LANES), BF16)],
        compiler_params=_params("parallel", "parallel", "arbitrary"),
        name="mla_attention",
    )(q, kv, k_rope)


def _extract_top(xs, out_refs, k):
    def drop_all(r, ys):
        out = []
        for y, out_ref in zip(ys, out_refs):
            m = jnp.max(y, axis=0, keepdims=True)
            out_ref[pl.ds(r, 1), :] = m
            out.append(jnp.where(y == m, -jnp.inf, y))
        return tuple(out)

    lax.fori_loop(0, k, drop_all, tuple(xs))
    most = [jnp.max(jnp.sum((x >= out_ref[k - 1:k, :]).astype(F32), axis=0, keepdims=True))
            for x, out_ref in zip(xs, out_refs)]

    @pl.when(functools.reduce(jnp.maximum, most) > k)
    def _():
        def drop_first(r, ys):
            out = []
            for y, out_ref in zip(ys, out_refs):
                rows = lax.broadcasted_iota(jnp.int32, y.shape, 0).astype(F32)
                m = jnp.max(y, axis=0, keepdims=True)
                out_ref[pl.ds(r, 1), :] = m
                first = jnp.min(jnp.where(y == m, rows, float(y.shape[0])), axis=0, keepdims=True)
                out.append(jnp.where(rows == first, -jnp.inf, y))
            return tuple(out)

        lax.fori_loop(0, k, drop_first, tuple(xs))


def _peer_select_kernel(qh_ref, keys_ref, st_ref, et_ref, thr_ref, v_ref, top_ref, *, heads, n_keys, key_half, topk):
    halves = []
    for hh in range(heads):
        for half in range(2):
            lanes = slice((2 * hh + half) * key_half, (2 * hh + half + 1) * key_half)
            s_t = lax.dot_general(keys_ref[half], qh_ref[:, lanes], _NT, preferred_element_type=F32)
            st_ref[hh, half * n_keys:(half + 1) * n_keys, :] = s_t
            halves.append(s_t)
    for c in range(2 * heads):
        _extract_top([halves[c]], [v_ref.at[c]], topk)
    assert topk == 16
    candidates = []
    for hh in range(heads):
        v1, v2 = v_ref[2 * hh], v_ref[2 * hh + 1]
        slabs = [v1 + v2[0:1]]
        slabs += [v1[0:8] + v2[b:b + 1] for b in range(1, 8)]
        slabs += [v1[0:1] + v2[8:16]]
        candidates.append(jnp.concatenate(slabs, axis=0))
    for hh in range(heads):
        _extract_top([candidates[hh]], [top_ref.at[hh]], topk)
    for hh in range(heads):
        top = top_ref[hh]
        thr_ref[hh] = top[topk - 1:topk]
        z_sum = jnp.sum(jnp.exp(top - top[0:1]), axis=0, keepdims=True)
        et_ref[hh, :n_keys, :] = jnp.exp(halves[2 * hh] - v_ref[2 * hh][0:1])
        et_ref[hh, n_keys:, :] = jnp.exp(halves[2 * hh + 1] - v_ref[2 * hh + 1][0:1]) / z_sum


def peer_select(qh, sub_keys, tt=512, heads_per_step=2):
    t = qh.shape[0]
    _, n_keys, key_half = sub_keys.shape
    n_heads = qh.shape[1] // (2 * key_half)
    tt = _tile(t, tt)
    hp = heads_per_step
    assert n_heads % hp == 0
    kern = functools.partial(_peer_select_kernel, heads=hp, n_keys=n_keys, key_half=key_half, topk=PEER_TOPK)
    return pl.pallas_call(
        kern,
        grid=(t // tt, n_heads // hp),
        in_specs=[
            pl.BlockSpec((tt, hp * 2 * key_half), lambda i, h: (i, h)),
            pl.BlockSpec(sub_keys.shape, lambda i, h: (0, 0, 0)),
        ],
        out_specs=[
            pl.BlockSpec((hp, 2 * n_keys, tt), lambda i, h: (h, 0, i)),
            pl.BlockSpec((hp, 2 * n_keys, tt), lambda i, h: (h, 0, i)),
            pl.BlockSpec((hp, 1, tt), lambda i, h: (h, 0, i)),
        ],
        out_shape=[
            jax.ShapeDtypeStruct((n_heads, 2 * n_keys, t), F32),
            jax.ShapeDtypeStruct((n_heads, 2 * n_keys, t), F32),
            jax.ShapeDtypeStruct((n_heads, 1, t), F32),
        ],
        scratch_shapes=[pltpu.VMEM((2 * hp, PEER_TOPK, tt), F32), pltpu.VMEM((hp, PEER_TOPK, tt), F32)],
        compiler_params=_params("parallel", "arbitrary"),
        name="peer_select",
    )(qh, sub_keys)


def _gelu_tanh(x):
    return 0.5 * x * (1.0 + jnp.tanh(math.sqrt(2.0 / math.pi) * (x + 0.044715 * (x * x * x))))


def _peer_dense_kernel(h_ref, g_ref, u_ref, v_ref, st_ref, et_ref, thr_ref, o_ref, xn_ref, *, n_heads, n_keys, te):
    e = pl.program_id(1)

    @pl.when(e == 0)
    def _():
        x = h_ref[...]
        xn_ref[...] = _rms_scale(x, g_ref[...]).astype(BF16)
        o_ref[...] = x

    hidden = lax.dot_general(xn_ref[...], u_ref[...], _NT, preferred_element_type=F32)
    rows_per_step = te // n_keys
    blocks = []
    for ii in range(rows_per_step):
        i_row = e * rows_per_step + ii
        gate_t = None
        for h in range(n_heads):
            score = st_ref[h, n_keys:2 * n_keys, :] + st_ref[h, pl.ds(i_row, 1), :]
            weight = et_ref[h, n_keys:2 * n_keys, :] * et_ref[h, pl.ds(i_row, 1), :]
            g = jnp.where(score >= thr_ref[h], weight, 0.0)
            gate_t = g if gate_t is None else gate_t + g
        blocks.append(gate_t)
    gate = jnp.concatenate(blocks, axis=0).T
    gh = (gate * _gelu_tanh(hidden)).astype(BF16)
    o_ref[...] += jnp.dot(gh, v_ref[...], preferred_element_type=F32)


def peer_dense(h, g, u, v, scores_t, factors_t, thr, tt=512, te=1024):
    t, d = h.shape
    n_exp = u.shape[0]
    n_heads, two_keys, _ = scores_t.shape
    n_keys = two_keys // 2
    tt, te = _tile(t, tt), _tile(n_exp, te)
    assert te % n_keys == 0 and n_exp == n_keys * n_keys
    kern = functools.partial(_peer_dense_kernel, n_heads=n_heads, n_keys=n_keys, te=te)
    return pl.pallas_call(
        kern,
        grid=(t // tt, n_exp // te),
        in_specs=[
            pl.BlockSpec((tt, d), lambda i, e: (i, 0)),
            pl.BlockSpec((1, d), lambda i, e: (0, 0)),
            pl.BlockSpec((te, d), lambda i, e: (e, 0)),
            pl.BlockSpec((te, d), lambda i, e: (e, 0)),
            pl.BlockSpec((n_heads, two_keys, tt), lambda i, e: (0, 0, i)),
            pl.BlockSpec((n_heads, two_keys, tt), lambda i, e: (0, 0, i)),
            pl.BlockSpec((n_heads, 1, tt), lambda i, e: (0, 0, i)),
        ],
        out_specs=pl.BlockSpec((tt, d), lambda i, e: (i, 0)),
        out_shape=jax.ShapeDtypeStruct((t, d), F32),
        scratch_shapes=[pltpu.VMEM((tt, d), BF16)],
        compiler_params=_params("parallel", "arbitrary"),
        name="peer_dense",
    )(h, g.reshape(1, d), u, v, scores_t, factors_t, thr)


def _ple_kernel(x_ref, g_ref, wg_ref, p_ref, wp_ref, r_ref, o_ref, xn_ref, pb_ref):
    @pl.when(pl.program_id(1) == 0)
    def _():
        xn_ref[...] = _rms_scale(x_ref[...], g_ref[...]).astype(BF16)
        pb_ref[...] = p_ref[...].astype(BF16)

    gate = 1.0 / (1.0 + jnp.exp(-jnp.dot(xn_ref[...], wg_ref[...], preferred_element_type=F32)))
    o_ref[...] = r_ref[...] + jnp.dot(pb_ref[...], wp_ref[...], preferred_element_type=F32) * gate


def per_layer_embedding(h, p, g, w_g, w_p, tm=1024, tn=512):
    t, d = h.shape
    pd = p.shape[1]
    tm, tn = _tile(t, tm), _tile(d, tn)
    return pl.pallas_call(
        _ple_kernel,
        grid=(t // tm, d // tn),
        in_specs=[
            pl.BlockSpec((tm, d), lambda i, j: (i, 0)),
            pl.BlockSpec((1, d), lambda i, j: (0, 0)),
            pl.BlockSpec((d, tn), lambda i, j: (0, j)),
            pl.BlockSpec((tm, pd), lambda i, j: (i, 0)),
            pl.BlockSpec((pd, tn), lambda i, j: (0, j)),
            pl.BlockSpec((tm, tn), lambda i, j: (i, j)),
        ],
        out_specs=pl.BlockSpec((tm, tn), lambda i, j: (i, j)),
        out_shape=jax.ShapeDtypeStruct((t, d), F32),
        scratch_shapes=[pltpu.VMEM((tm, d), BF16), pltpu.VMEM((tm, pd), BF16)],
        compiler_params=_params("parallel", "arbitrary"),
        name="per_layer_embedding",
    )(h, g.reshape(1, d), w_g, p, w_p, h)


def _pad_cols(w, multiple):
    extra = (-w.shape[-1]) % multiple
    return jnp.pad(w, ((0, 0), (0, extra))) if extra else w


def kernel(x, p, attn_norm, ffn_norm, ple_norm, final_norm, sb_w_qkv, sb_w_o, swa_w_qkv, swa_w_o, swa_sinks,
           mla_w_down, mla_q_norm, mla_kv_norm, mla_w_uq, mla_w_ukv, mla_w_o,
           peer_w_q, peer_sub_keys, peer_u, peer_v, ple_w_p, ple_w_g):
    b, s, d = x.shape
    t = b * s
    depth = p.shape[0]
    h = x.reshape(t, d)
    for i in range(depth):
        kind, j = i % N_MIXERS, i // N_MIXERS
        if kind == 0:
            qkv = norm_matmul(h, attn_norm[i], sb_w_qkv[j].astype(BF16), BF16)
            o = sb_attention(qkv.reshape(b, s, -1), SB_HEADS)
            w_o = sb_w_o[j]
        elif kind == 1:
            qkv = norm_matmul(h, attn_norm[i], swa_w_qkv[j].astype(BF16), BF16, tn=1280)
            head_dim = swa_w_o.shape[1] // SWA_HEADS
            o = swa_attention(qkv.reshape(b, s, -1), swa_sinks[j], SWA_HEADS, SWA_KV_HEADS, head_dim)
            w_o = swa_w_o[j]
        else:
            down = norm_matmul(h, attn_norm[i], _pad_cols(mla_w_down[j], LANES).astype(BF16), F32, tn=2048)
            qk_dim = MLA_NOPE_DIM + MLA_ROPE_DIM
            w_uq_pad = jnp.pad(mla_w_uq[j].reshape(MLA_Q_RANK, MLA_HEADS, qk_dim),
                               ((0, 0), (0, 0), (0, 2 * LANES - qk_dim))).reshape(MLA_Q_RANK, -1)
            q, kv, k_rope = mla_projections(down, mla_q_norm[j], mla_kv_norm[j], w_uq_pad.astype(BF16),
                                            mla_w_ukv[j].astype(BF16), s)
            o = mla_attention(q.reshape(b, s, -1), kv.reshape(b, s, -1), k_rope.reshape(b, s, -1), MLA_HEADS)
            w_o = mla_w_o[j]
        h = matmul_residual(o.reshape(t, -1), w_o.astype(BF16), h)
        qh = norm_matmul(h, ffn_norm[i], peer_w_q[i].astype(BF16), BF16)
        scores_t, factors_t, thr = peer_select(qh, peer_sub_keys[i].astype(BF16))
        h = peer_dense(h, ffn_norm[i], peer_u[i].astype(BF16), peer_v[i].astype(BF16), scores_t, factors_t, thr)
        h = per_layer_embedding(h, p[i].reshape(t, -1), ple_norm[i], ple_w_g[i].astype(BF16), ple_w_p[i].astype(BF16))
    return final_rms_norm(h, final_norm).reshape(b, s, d)
```

```python
import functools
import math

import jax
import jax.numpy as jnp
from jax import lax
from jax.experimental import pallas as pl
from jax.experimental.pallas import tpu as pltpu

F32 = jnp.float32
BF16 = jnp.bfloat16

RMS_EPS = 1e-6
N_MIXERS = 3
SB_HEADS = 16
SWA_HEADS = 32
SWA_KV_HEADS = 4
SWA_WINDOW = 128
MLA_HEADS = 16
MLA_Q_RANK = 512
MLA_KV_RANK = 512
MLA_NOPE_DIM = 128
MLA_ROPE_DIM = 64
MLA_V_DIM = 128
ROPE_THETA = 10000.0
PEER_N_KEYS = 64
PEER_TOPK = 16

NORM_MARGIN = 1.02
DENOMINATOR_FLOOR = 2.0 ** -80
EXP2_UNDERFLOW = -150.0
LANES = 128
V7X_VMEM_BYTES = 64 * 1024 * 1024
VMEM_LIMIT = V7X_VMEM_BYTES * 7 // 8

_NT = (((1,), (1,)), ((), ()))


def _params(*semantics):
    return pltpu.CompilerParams(dimension_semantics=semantics, vmem_limit_bytes=VMEM_LIMIT)


def _tile(n, want):
    if n <= want:
        return n
    t = (want // LANES) * LANES
    while t > LANES and n % t:
        t -= LANES
    assert n % t == 0, (n, want)
    return t


def _rms_scale(x, g):
    ms = jnp.mean(x * x, axis=-1, keepdims=True)
    return x * lax.rsqrt(ms + RMS_EPS) * g


def _norm_matmul_kernel(x_ref, g_ref, w_ref, o_ref, xn_ref):
    @pl.when(pl.program_id(1) == 0)
    def _():
        xn_ref[...] = _rms_scale(x_ref[...], g_ref[...]).astype(BF16)

    o_ref[...] = jnp.dot(xn_ref[...], w_ref[...], preferred_element_type=F32).astype(o_ref.dtype)


def norm_matmul(x, g, w, out_dtype, tm=1024, tn=1024):
    m, k = x.shape
    n = w.shape[1]
    tm, tn = _tile(m, tm), _tile(n, tn)
    return pl.pallas_call(
        _norm_matmul_kernel,
        grid=(m // tm, n // tn),
        in_specs=[
            pl.BlockSpec((tm, k), lambda i, j: (i, 0)),
            pl.BlockSpec((1, k), lambda i, j: (0, 0)),
            pl.BlockSpec((k, tn), lambda i, j: (0, j)),
        ],
        out_specs=pl.BlockSpec((tm, tn), lambda i, j: (i, j)),
        out_shape=jax.ShapeDtypeStruct((m, n), out_dtype),
        scratch_shapes=[pltpu.VMEM((tm, k), BF16)],
        compiler_params=_params("parallel", "arbitrary"),
        name="norm_matmul",
    )(x, g.reshape(1, k), w)


def _matmul_residual_kernel(a_ref, w_ref, r_ref, o_ref):
    o_ref[...] = r_ref[...] + jnp.dot(a_ref[...], w_ref[...], preferred_element_type=F32)


def matmul_residual(a, w, res, tm=1024, tn=1024):
    m, k = a.shape
    n = w.shape[1]
    tm, tn = _tile(m, tm), _tile(n, tn)
    return pl.pallas_call(
        _matmul_residual_kernel,
        grid=(m // tm, n // tn),
        in_specs=[
            pl.BlockSpec((tm, k), lambda i, j: (i, 0)),
            pl.BlockSpec((k, tn), lambda i, j: (0, j)),
            pl.BlockSpec((tm, tn), lambda i, j: (i, j)),
        ],
        out_specs=pl.BlockSpec((tm, tn), lambda i, j: (i, j)),
        out_shape=jax.ShapeDtypeStruct((m, n), F32),
        compiler_params=_params("parallel", "arbitrary"),
        name="matmul_residual",
    )(a, w, res)


def _final_norm_kernel(x_ref, g_ref, o_ref):
    o_ref[...] = _rms_scale(x_ref[...], g_ref[...])


def final_rms_norm(x, g, tm=1024):
    m, k = x.shape
    tm = _tile(m, tm)
    return pl.pallas_call(
        _final_norm_kernel,
        grid=(m // tm,),
        in_specs=[pl.BlockSpec((tm, k), lambda i: (i, 0)), pl.BlockSpec((1, k), lambda i: (0, 0))],
        out_specs=pl.BlockSpec((tm, k), lambda i: (i, 0)),
        out_shape=jax.ShapeDtypeStruct((m, k), F32),
        compiler_params=_params("parallel"),
        name="final_norm",
    )(x, g.reshape(1, k))


def _sb_attn_kernel(q_ref, k_ref, v_ref, o_ref, kmax2_ref, *, bq, bk, d, heads, scale):
    i = pl.program_id(2)
    n_diag = bq // bk
    row = lax.broadcasted_iota(jnp.int32, (bk, bk), 0)
    col = lax.broadcasted_iota(jnp.int32, (bk, bk), 1)
    suffix_ones = (row >= col).astype(BF16)
    q_idx = lax.broadcasted_iota(jnp.int32, (bq, bk), 0)
    k_idx = lax.broadcasted_iota(jnp.int32, (bq, bk), 1)

    def scores(hh, m):
        lanes = slice(hh * d, (hh + 1) * d)
        kb = k_ref[0, pl.ds(pl.multiple_of(m * bk, bk), bk), lanes]
        return lax.dot_general(q_ref[0, :, lanes], kb, _NT, preferred_element_type=F32)

    def absorb(hh, m, qk, acc, log_run, key_offset):
        vb = v_ref[0, pl.ds(pl.multiple_of(m * bk, bk), bk), hh * d:(hh + 1) * d]
        z = qk * scale
        neg_z = -z
        log_keep = jnp.minimum(neg_z, 0.0) - jnp.log2(1.0 + jnp.exp2(jnp.minimum(z, neg_z)))
        if key_offset is not None:
            strict = k_idx + key_offset < q_idx
            log_keep = jnp.where(strict, log_keep, 0.0)
        incl = jnp.dot(log_keep.astype(BF16), suffix_ones, preferred_element_type=F32)
        logw = z + incl + log_run
        if key_offset is not None:
            logw = jnp.where(strict, logw, -jnp.inf)
        p = jnp.exp2(logw)
        acc = acc + jnp.dot(p.astype(BF16), vb, preferred_element_type=F32)
        return acc, log_run + incl[:, 0:1]

    def step(m, state, key_offset):
        return tuple(absorb(hh, m, scores(hh, m), *state[hh], key_offset) for hh in range(heads))

    @pl.when(i == 0)
    def _():
        for hh in range(heads):
            kf = k_ref[0, :, hh * d:(hh + 1) * d].astype(F32)
            k_norm2 = jnp.max(jnp.sum(kf * kf, axis=-1, keepdims=True), axis=0, keepdims=True)
            kmax2_ref[hh] = jnp.broadcast_to(k_norm2, kmax2_ref.shape[1:])

    logit_bound = []
    for hh in range(heads):
        qf = q_ref[0, :, hh * d:(hh + 1) * d].astype(F32)
        q_norm2 = jnp.sum(qf * qf, axis=-1, keepdims=True)
        logit_bound.append(jnp.sqrt(q_norm2 * kmax2_ref[hh][0:1, 0:1]) * (scale * 1.01) + 1.0)

    def any_weight_left(state):
        worst = [jnp.max(state[hh][1] + logit_bound[hh]) for hh in range(heads)]
        return (functools.reduce(jnp.maximum, worst) >= EXP2_UNDERFLOW).astype(jnp.int32)

    state = tuple((jnp.zeros((bq, d), F32), jnp.zeros((bq, 1), F32)) for _ in range(heads))
    for dd in reversed(range(n_diag)):
        state = step(i * n_diag + dd, state, dd * bk)
    n_before = i * n_diag

    def more(c):
        t, go, _ = c
        return jnp.logical_and(t < n_before, go > 0)

    def walk(c):
        t, _, state = c
        state = step(n_before - 1 - t, state, None)
        return t + 1, any_weight_left(state), state

    _, _, state = lax.while_loop(more, walk, (jnp.int32(0), any_weight_left(state), state))
    for hh in range(heads):
        o_ref[0, :, hh * d:(hh + 1) * d] = state[hh][0].astype(o_ref.dtype)


def sb_attention(qkv, n_heads, bq=512, bk=256, heads_per_step=2):
    b, s, three_hd = qkv.shape
    d = three_hd // (3 * n_heads)
    assert d == LANES
    bq = min(bq, s)
    bk = min(bk, bq)
    hp = heads_per_step
    assert s % bq == 0 and bq % bk == 0 and n_heads % hp == 0
    n_grp = n_heads // hp
    kern = functools.partial(_sb_attn_kernel, bq=bq, bk=bk, d=d, heads=hp, scale=d ** -0.5 * math.log2(math.e))
    return pl.pallas_call(
        kern,
        grid=(b, n_grp, s // bq),
        in_specs=[
            pl.BlockSpec((1, bq, hp * d), lambda bi, h, i: (bi, i, h)),
            pl.BlockSpec((1, s, hp * d), lambda bi, h, i: (bi, 0, n_grp + h)),
            pl.BlockSpec((1, s, hp * d), lambda bi, h, i: (bi, 0, 2 * n_grp + h)),
        ],
        out_specs=pl.BlockSpec((1, bq, hp * d), lambda bi, h, i: (bi, i, h)),
        out_shape=jax.ShapeDtypeStruct((b, s, n_heads * d), BF16),
        scratch_shapes=[pltpu.VMEM((hp, 8, LANES), F32)],
        compiler_params=_params("parallel", "parallel", "arbitrary"),
        name="sb_attention",
    )(qkv, qkv, qkv)


def _swa_kernel(q_ref, kp_ref, kc_ref, vp_ref, vc_ref, bias_ref, sink_ref, o_ref, *, n_kv, grp, d, blk, scale):
    i = pl.program_id(1)
    rows = grp * blk
    kcol = lax.broadcasted_iota(jnp.int32, (rows, 2 * blk), 1)
    has_key = kcol >= jnp.where(i > 0, 0, blk)
    ones_k = jnp.ones((d, LANES), BF16)
    ones_v = jnp.ones((2 * blk, LANES - d), BF16)

    def operands(kh):
        qg = jnp.concatenate(
            [q_ref[0, :, (kh * grp + g) * d:(kh * grp + g + 1) * d] for g in range(grp)], axis=0)
        kk = jnp.concatenate([kp_ref[0, :, kh * d:(kh + 1) * d], kc_ref[0, :, kh * d:(kh + 1) * d]], axis=0)
        vv = jnp.concatenate([vp_ref[0, :, kh * d:(kh + 1) * d], vc_ref[0, :, kh * d:(kh + 1) * d]], axis=0)
        return qg, kk, vv

    def store(outs):
        o_ref[0] = jnp.concatenate(outs, axis=1).astype(o_ref.dtype)

    outs, smallest = [], None
    for kh in range(n_kv):
        qg, kk, vv = operands(kh)
        z = lax.dot_general(qg, kk, _NT, preferred_element_type=F32) * scale + bias_ref[kh]
        q_norm2 = jnp.dot(qg * qg, ones_k, preferred_element_type=F32)
        kf = kk.astype(F32)
        k_norm2 = jnp.max(jnp.sum(kf * kf, axis=-1, keepdims=True), axis=0, keepdims=True)
        bound = jnp.sqrt(q_norm2 * k_norm2) * (scale * NORM_MARGIN)
        p = jnp.where(has_key, jnp.exp(z - jnp.concatenate([bound, bound], axis=1)), 0.0)
        pv = jnp.dot(p.astype(BF16), jnp.concatenate([vv, ones_v], axis=1), preferred_element_type=F32)
        den = pv[:, d:2 * d] + jnp.exp(sink_ref[kh][:, :d] - bound[:, :d])
        o = pv[:, :d] / den
        low = jnp.min(den)
        smallest = low if smallest is None else jnp.minimum(smallest, low)
        outs.extend(o[g * blk:(g + 1) * blk] for g in range(grp))
    store(outs)

    @pl.when(jnp.logical_not(smallest >= DENOMINATOR_FLOOR))
    def _():
        outs = []
        for kh in range(n_kv):
            qg, kk, vv = operands(kh)
            z = lax.dot_general(qg, kk, _NT, preferred_element_type=F32) * scale + bias_ref[kh]
            z = jnp.where(has_key, z, -jnp.inf)
            sink = sink_ref[kh][:, 0:1]
            mx = jnp.maximum(jnp.max(z, axis=-1, keepdims=True), sink)
            p = jnp.exp(z - mx)
            den = jnp.sum(p, axis=-1, keepdims=True) + jnp.exp(sink - mx)
            o = jnp.dot(p.astype(BF16), vv, preferred_element_type=F32) / den
            outs.extend(o[g * blk:(g + 1) * blk] for g in range(grp))
        store(outs)


def swa_attention(qkv, sinks, n_heads, n_kv, d):
    b, s, _ = qkv.shape
    blk = SWA_WINDOW
    grp = n_heads // n_kv
    qd, kd = n_heads * d, n_kv * d
    assert qd % kd == 0 and kd % LANES == 0 and s % blk == 0 and d < LANES
    slopes = 2.0 ** (-8.0 * jnp.arange(1, n_heads + 1, dtype=F32) / n_heads)
    dist = (jnp.arange(blk)[:, None] + blk - jnp.arange(2 * blk)[None, :]).astype(F32)
    bias = jnp.where((dist >= 0) & (dist < SWA_WINDOW), -slopes[:, None, None] * dist, -jnp.inf)
    bias = bias.reshape(n_kv, grp * blk, 2 * blk)
    sink_rows = jnp.broadcast_to(
        jnp.repeat(sinks.astype(F32).reshape(n_kv, grp), blk, axis=1)[:, :, None], (n_kv, grp * blk, LANES))
    kern = functools.partial(_swa_kernel, n_kv=n_kv, grp=grp, d=d, blk=blk, scale=d ** -0.5)
    prev = lambda bi, i: (bi, jnp.maximum(i - 1, 0), qd // kd)
    cur = lambda bi, i: (bi, i, qd // kd)
    prev_v = lambda bi, i: (bi, jnp.maximum(i - 1, 0), qd // kd + 1)
    cur_v = lambda bi, i: (bi, i, qd // kd + 1)
    const = lambda bi, i: (0, 0, 0)
    return pl.pallas_call(
        kern,
        grid=(b, s // blk),
        in_specs=[
            pl.BlockSpec((1, blk, qd), lambda bi, i: (bi, i, 0)),
            pl.BlockSpec((1, blk, kd), prev),
            pl.BlockSpec((1, blk, kd), cur),
            pl.BlockSpec((1, blk, kd), prev_v),
            pl.BlockSpec((1, blk, kd), cur_v),
            pl.BlockSpec((n_kv, grp * blk, 2 * blk), const),
            pl.BlockSpec((n_kv, grp * blk, LANES), const),
        ],
        out_specs=pl.BlockSpec((1, blk, qd), lambda bi, i: (bi, i, 0)),
        out_shape=jax.ShapeDtypeStruct((b, s, qd), BF16),
        compiler_params=_params("parallel", "arbitrary"),
        name="swa_attention",
    )(qkv, qkv, qkv, qkv, qkv, bias, sink_rows)


def _rope_lanes(r, cos_ref, sin_lo_ref, sin_hi_ref):
    half = MLA_ROPE_DIM // 2
    return (r * cos_ref[...]
            + pltpu.roll(r, LANES - half, 1) * sin_lo_ref[...]
            + pltpu.roll(r, half, 1) * sin_hi_ref[...])


def _mla_q_kernel(x_ref, g_ref, w_ref, cos_ref, slo_ref, shi_ref, o_ref, *, n_heads):
    xn = _rms_scale(x_ref[...], g_ref[...]).astype(BF16)
    lane = lax.broadcasted_iota(jnp.int32, (x_ref.shape[0], LANES), 1)
    for h in range(n_heads):
        a = jnp.dot(xn, w_ref[:, 2 * LANES * h:2 * LANES * (h + 1)], preferred_element_type=F32)
        nope = a[:, :LANES]
        rope = _rope_lanes(a[:, LANES:], cos_ref, slo_ref, shi_ref)
        norm2 = jnp.sum(nope * nope + rope * rope, axis=-1, keepdims=True)
        o_ref[:, 2 * LANES * h:2 * LANES * h + LANES] = nope.astype(BF16)
        o_ref[:, 2 * LANES * h + LANES:2 * LANES * (h + 1)] = jnp.where(
            lane == MLA_ROPE_DIM, NORM_MARGIN * jnp.sqrt(norm2), rope).astype(BF16)


def _mla_kv_kernel(x_ref, g_ref, xr_ref, w_ref, cos_ref, slo_ref, shi_ref, kv_ref, kr_ref, *, chunk):
    xn = _rms_scale(x_ref[...], g_ref[...]).astype(BF16)
    for c in range(w_ref.shape[1] // chunk):
        cols = slice(c * chunk, (c + 1) * chunk)
        kv_ref[:, cols] = jnp.dot(xn, w_ref[:, cols], preferred_element_type=F32).astype(BF16)
    kr_ref[...] = _rope_lanes(xr_ref[...], cos_ref, slo_ref, shi_ref).astype(BF16)


def _rope_tables(s):
    half = MLA_ROPE_DIM // 2
    inv_freq = ROPE_THETA ** (-jnp.arange(half, dtype=F32) / half)
    ang = jnp.arange(s, dtype=F32)[:, None] * inv_freq[None, :]
    cos, sin = jnp.cos(ang), jnp.sin(ang)
    zeros = jnp.zeros((s, LANES - 2 * half), F32)
    z_half = jnp.zeros((s, half), F32)
    cos_t = jnp.concatenate([cos, cos, zeros], axis=1)
    sin_lo = jnp.concatenate([-sin, z_half, zeros], axis=1)
    sin_hi = jnp.concatenate([z_half, sin, zeros], axis=1)
    return cos_t, sin_lo, sin_hi


def mla_projections(down, g_q, g_kv, w_uq_pad, w_ukv, seq, tm=512):
    t = down.shape[0]
    tm = _tile(seq, tm)
    n_heads = w_uq_pad.shape[1] // (2 * LANES)
    tables = _rope_tables(seq)
    n_pos = seq // tm
    tab_spec = pl.BlockSpec((tm, LANES), lambda i: (i % n_pos, 0))
    q = pl.pallas_call(
        functools.partial(_mla_q_kernel, n_heads=n_heads),
        grid=(t // tm,),
        in_specs=[
            pl.BlockSpec((tm, MLA_Q_RANK), lambda i: (i, 0)),
            pl.BlockSpec((1, MLA_Q_RANK), lambda i: (0, 0)),
            pl.BlockSpec(w_uq_pad.shape, lambda i: (0, 0)),
            tab_spec, tab_spec, tab_spec,
        ],
        out_specs=pl.BlockSpec((tm, w_uq_pad.shape[1]), lambda i: (i, 0)),
        out_shape=jax.ShapeDtypeStruct((t, w_uq_pad.shape[1]), BF16),
        compiler_params=_params("parallel"),
        name="mla_q_proj",
    )(down, g_q.reshape(1, -1), w_uq_pad, *tables)
    assert MLA_Q_RANK == MLA_KV_RANK and (MLA_Q_RANK + MLA_KV_RANK) % LANES == 0
    kv, k_rope = pl.pallas_call(
        functools.partial(_mla_kv_kernel, chunk=_tile(w_ukv.shape[1], 1024)),
        grid=(t // tm,),
        in_specs=[
            pl.BlockSpec((tm, MLA_KV_RANK), lambda i: (i, 1)),
            pl.BlockSpec((1, MLA_KV_RANK), lambda i: (0, 0)),
            pl.BlockSpec((tm, LANES), lambda i: (i, (MLA_Q_RANK + MLA_KV_RANK) // LANES)),
            pl.BlockSpec(w_ukv.shape, lambda i: (0, 0)),
            tab_spec, tab_spec, tab_spec,
        ],
        out_specs=[pl.BlockSpec((tm, w_ukv.shape[1]), lambda i: (i, 0)),
                   pl.BlockSpec((tm, LANES), lambda i: (i, 0))],
        out_shape=[jax.ShapeDtypeStruct((t, w_ukv.shape[1]), BF16), jax.ShapeDtypeStruct((t, LANES), BF16)],
        compiler_params=_params("parallel"),
        name="mla_kv_proj",
    )(down, g_kv.reshape(1, -1), down, w_ukv, *tables)
    return q, kv, k_rope


def _mla_attn_kernel(q_ref, kv_ref, kr_ref, o_ref, k_ref, v_ref, *, bq, bk, heads, scale):
    i = pl.program_id(2)
    width = 2 * LANES

    @pl.when(i == 0)
    def _():
        kr = kr_ref[0]
        krf = kr.astype(F32)
        kr_norm2 = jnp.sum(krf * krf, axis=-1, keepdims=True)
        lane = lax.broadcasted_iota(jnp.int32, kr.shape, 1)
        for hh in range(heads):
            kn = kv_ref[0, :, hh * width:hh * width + LANES]
            knf = kn.astype(F32)
            k_max = jnp.sqrt(jnp.max(jnp.sum(knf * knf, axis=-1, keepdims=True) + kr_norm2, axis=0, keepdims=True))
            k_ref[hh, :, :LANES] = kn
            k_ref[hh, :, LANES:] = jnp.where(lane == MLA_ROPE_DIM, (-NORM_MARGIN * k_max).astype(BF16), kr)
            v_ref[hh, :, :LANES] = kv_ref[0, :, hh * width + LANES:(hh + 1) * width]
            v_ref[hh, :, LANES:] = jnp.ones((kr.shape[0], LANES), BF16)

    n_diag = bq // bk
    n_before = i * n_diag
    row = lax.broadcasted_iota(jnp.int32, (bk, bk), 0)
    col = lax.broadcasted_iota(jnp.int32, (bk, bk), 1)
    causal = col <= row

    def shifted_logits(hh, m, first_row):
        kb = k_ref[hh, pl.ds(pl.multiple_of(m * bk, bk), bk), :]
        q = q_ref[0, first_row:, hh * width:(hh + 1) * width]
        return lax.dot_general(q, kb, _NT, preferred_element_type=F32)

    def values(hh, m):
        return v_ref[hh, pl.ds(pl.multiple_of(m * bk, bk), bk), :]

    def quick_step(hh, m, acc, dd):
        first_row = 0 if dd is None else dd * bk
        p = jnp.exp2(shifted_logits(hh, m, first_row) * (scale * math.log2(math.e)))
        if dd is not None:
            on_diagonal = jnp.where(causal, p[:bk], 0.0)
            p = on_diagonal if p.shape[0] == bk else jnp.concatenate([on_diagonal, p[bk:]], axis=0)
        pv = jnp.dot(p.astype(BF16), values(hh, m), preferred_element_type=F32)
        return acc + pv if first_row == 0 else jnp.concatenate([acc[:first_row], acc[first_row:] + pv], axis=0)

    acc = tuple(jnp.zeros((bq, width), F32) for _ in range(heads))
    acc = lax.fori_loop(
        0, n_before, lambda m, a: tuple(quick_step(hh, m, a[hh], None) for hh in range(heads)), acc)
    for dd in range(n_diag):
        acc = tuple(quick_step(hh, n_before + dd, acc[hh], dd) for hh in range(heads))
    smallest = functools.reduce(jnp.minimum, [jnp.min(acc[hh][:, LANES:]) for hh in range(heads)])
    for hh in range(heads):
        o_ref[0, :, hh * LANES:(hh + 1) * LANES] = (acc[hh][:, :LANES] / acc[hh][:, LANES:]).astype(o_ref.dtype)

    @pl.when(jnp.logical_not(smallest >= DENOMINATOR_FLOOR))
    def _():
        def careful_step(hh, m, state, dd):
            acc, mx, den = state
            z = shifted_logits(hh, m, 0) * scale
            if dd is not None:
                q_idx = lax.broadcasted_iota(jnp.int32, (bq, bk), 0)
                k_idx = lax.broadcasted_iota(jnp.int32, (bq, bk), 1)
                z = jnp.where(k_idx + dd * bk <= q_idx, z, -jnp.inf)
            new_mx = jnp.maximum(mx, jnp.max(z, axis=-1, keepdims=True))
            corr = jnp.exp(mx - new_mx)
            pz = jnp.exp(z - new_mx)
            den = den * corr + jnp.sum(pz, axis=-1, keepdims=True)
            pv = jnp.dot(pz.astype(BF16), values(hh, m)[:, :LANES], preferred_element_type=F32)
            return acc * corr + pv, new_mx, den

        for hh in range(heads):
            state = (jnp.zeros((bq, LANES), F32), jnp.full((bq, 1), -jnp.inf, F32), jnp.zeros((bq, 1), F32))
            state = lax.fori_loop(0, n_before, lambda m, c: careful_step(hh, m, c, None), state)
            for dd in range(n_diag):
                state = careful_step(hh, n_before + dd, state, dd)
            o_ref[0, :, hh * LANES:(hh + 1) * LANES] = (state[0] / state[2]).astype(o_ref.dtype)


def mla_attention(q, kv, k_rope, n_heads, bq=1024, bk=1024, heads_per_step=2):
    b, s, _ = q.shape
    bq = min(bq, s)
    bk = min(bk, bq)
    hp = heads_per_step
    assert s % bq == 0 and bq % bk == 0 and n_heads % hp == 0 and MLA_V_DIM == LANES
    scale = (MLA_NOPE_DIM + MLA_ROPE_DIM) ** -0.5
    return pl.pallas_call(
        functools.partial(_mla_attn_kernel, bq=bq, bk=bk, heads=hp, scale=scale),
        grid=(b, n_heads // hp, s // bq),
        in_specs=[
            pl.BlockSpec((1, bq, hp * 2 * LANES), lambda bi, h, i: (bi, i, h)),
            pl.BlockSpec((1, s, hp * 2 * LANES), lambda bi, h, i: (bi, 0, h)),
            pl.BlockSpec((1, s, LANES), lambda bi, h, i: (bi, 0, 0)),
        ],
        out_specs=pl.BlockSpec((1, bq, hp * LANES), lambda bi, h, i: (bi, i, h)),
        out_shape=jax.ShapeDtypeStruct((b, s, n_heads * MLA_V_DIM), BF16),
        scratch_shapes=[pltpu.VMEM((hp, s, 2 * LANES), BF16), pltpu.VMEM((hp, s, 2 * LANES), BF16)],
        compiler_params=_params("parallel", "parallel", "arbitrary"),
        name="mla_attention",
    )(q, kv, k_rope)


def _extract_top(xs, out_refs, k):
    def drop_all(r, ys):
        out = []
        for y, out_ref in zip(ys, out_refs):
            m = jnp.max(y, axis=0, keepdims=True)
            out_ref[pl.ds(r, 1), :] = m
            out.append(jnp.where(y == m, -jnp.inf, y))
        return tuple(out)

    lax.fori_loop(0, k, drop_all, tuple(xs))
    most = [jnp.max(jnp.sum((x >= out_ref[k - 1:k, :]).astype(F32), axis=0, keepdims=True))
            for x, out_ref in zip(xs, out_refs)]

    @pl.when(functools.reduce(jnp.maximum, most) > k)
    def _():
        def drop_first(r, ys):
            out = []
            for y, out_ref in zip(ys, out_refs):
                rows = lax.broadcasted_iota(jnp.int32, y.shape, 0).astype(F32)
                m = jnp.max(y, axis=0, keepdims=True)
                out_ref[pl.ds(r, 1), :] = m
                first = jnp.min(jnp.where(y == m, rows, float(y.shape[0])), axis=0, keepdims=True)
                out.append(jnp.where(rows == first, -jnp.inf, y))
            return tuple(out)

        lax.fori_loop(0, k, drop_first, tuple(xs))


def _peer_select_kernel(qh_ref, keys_ref, st_ref, et_ref, thr_ref, v_ref, top_ref, *, heads, n_keys, key_half, topk):
    halves = []
    for hh in range(heads):
        for half in range(2):
            lanes = slice((2 * hh + half) * key_half, (2 * hh + half + 1) * key_half)
            s_t = lax.dot_general(keys_ref[half], qh_ref[:, lanes], _NT, preferred_element_type=F32)
            st_ref[hh, half * n_keys:(half + 1) * n_keys, :] = s_t
            halves.append(s_t)
    for c in range(2 * heads):
        _extract_top([halves[c]], [v_ref.at[c]], topk)
    assert topk == 16
    candidates = []
    for hh in range(heads):
        v1, v2 = v_ref[2 * hh], v_ref[2 * hh + 1]
        slabs = [v1 + v2[0:1]]
        slabs += [v1[0:8] + v2[b:b + 1] for b in range(1, 8)]
        slabs += [v1[0:1] + v2[8:16]]
        candidates.append(jnp.concatenate(slabs, axis=0))
    for hh in range(heads):
        _extract_top([candidates[hh]], [top_ref.at[hh]], topk)
    for hh in range(heads):
        top = top_ref[hh]
        thr_ref[hh] = top[topk - 1:topk]
        z_sum = jnp.sum(jnp.exp(top - top[0:1]), axis=0, keepdims=True)
        et_ref[hh, :n_keys, :] = jnp.exp(halves[2 * hh] - v_ref[2 * hh][0:1])
        et_ref[hh, n_keys:, :] = jnp.exp(halves[2 * hh + 1] - v_ref[2 * hh + 1][0:1]) / z_sum


def peer_select(qh, sub_keys, tt=512, heads_per_step=2):
    t = qh.shape[0]
    _, n_keys, key_half = sub_keys.shape
    n_heads = qh.shape[1] // (2 * key_half)
    tt = _tile(t, tt)
    hp = heads_per_step
    assert n_heads % hp == 0
    kern = functools.partial(_peer_select_kernel, heads=hp, n_keys=n_keys, key_half=key_half, topk=PEER_TOPK)
    return pl.pallas_call(
        kern,
        grid=(t // tt, n_heads // hp),
        in_specs=[
            pl.BlockSpec((tt, hp * 2 * key_half), lambda i, h: (i, h)),
            pl.BlockSpec(sub_keys.shape, lambda i, h: (0, 0, 0)),
        ],
        out_specs=[
            pl.BlockSpec((hp, 2 * n_keys, tt), lambda i, h: (h, 0, i)),
            pl.BlockSpec((hp, 2 * n_keys, tt), lambda i, h: (h, 0, i)),
            pl.BlockSpec((hp, 1, tt), lambda i, h: (h, 0, i)),
        ],
        out_shape=[
            jax.ShapeDtypeStruct((n_heads, 2 * n_keys, t), F32),
            jax.ShapeDtypeStruct((n_heads, 2 * n_keys, t), F32),
            jax.ShapeDtypeStruct((n_heads, 1, t), F32),
        ],
        scratch_shapes=[pltpu.VMEM((2 * hp, PEER_TOPK, tt), F32), pltpu.VMEM((hp, PEER_TOPK, tt), F32)],
        compiler_params=_params("parallel", "arbitrary"),
        name="peer_select",
    )(qh, sub_keys)


def _gelu_tanh(x):
    return 0.5 * x * (1.0 + jnp.tanh(math.sqrt(2.0 / math.pi) * (x + 0.044715 * (x * x * x))))


def _peer_dense_kernel(h_ref, g_ref, u_ref, v_ref, st_ref, et_ref, thr_ref, o_ref, xn_ref, *, n_heads, n_keys, te):
    e = pl.program_id(1)

    @pl.when(e == 0)
    def _():
        x = h_ref[...]
        xn_ref[...] = _rms_scale(x, g_ref[...]).astype(BF16)
        o_ref[...] = x

    hidden = lax.dot_general(xn_ref[...], u_ref[...], _NT, preferred_element_type=F32)
    rows_per_step = te // n_keys
    blocks = []
    for ii in range(rows_per_step):
        i_row = e * rows_per_step + ii
        gate_t = None
        for h in range(n_heads):
            score = st_ref[h, n_keys:2 * n_keys, :] + st_ref[h, pl.ds(i_row, 1), :]
            weight = et_ref[h, n_keys:2 * n_keys, :] * et_ref[h, pl.ds(i_row, 1), :]
            g = jnp.where(score >= thr_ref[h], weight, 0.0)
            gate_t = g if gate_t is None else gate_t + g
        blocks.append(gate_t)
    gate = jnp.concatenate(blocks, axis=0).T
    gh = (gate * _gelu_tanh(hidden)).astype(BF16)
    o_ref[...] += jnp.dot(gh, v_ref[...], preferred_element_type=F32)


def peer_dense(h, g, u, v, scores_t, factors_t, thr, tt=512, te=1024):
    t, d = h.shape
    n_exp = u.shape[0]
    n_heads, two_keys, _ = scores_t.shape
    n_keys = two_keys // 2
    tt, te = _tile(t, tt), _tile(n_exp, te)
    assert te % n_keys == 0 and n_exp == n_keys * n_keys
    kern = functools.partial(_peer_dense_kernel, n_heads=n_heads, n_keys=n_keys, te=te)
    return pl.pallas_call(
        kern,
        grid=(t // tt, n_exp // te),
        in_specs=[
            pl.BlockSpec((tt, d), lambda i, e: (i, 0)),
            pl.BlockSpec((1, d), lambda i, e: (0, 0)),
            pl.BlockSpec((te, d), lambda i, e: (e, 0)),
            pl.BlockSpec((te, d), lambda i, e: (e, 0)),
            pl.BlockSpec((n_heads, two_keys, tt), lambda i, e: (0, 0, i)),
            pl.BlockSpec((n_heads, two_keys, tt), lambda i, e: (0, 0, i)),
            pl.BlockSpec((n_heads, 1, tt), lambda i, e: (0, 0, i)),
        ],
        out_specs=pl.BlockSpec((tt, d), lambda i, e: (i, 0)),
        out_shape=jax.ShapeDtypeStruct((t, d), F32),
        scratch_shapes=[pltpu.VMEM((tt, d), BF16)],
        compiler_params=_params("parallel", "arbitrary"),
        name="peer_dense",
    )(h, g.reshape(1, d), u, v, scores_t, factors_t, thr)


def _ple_kernel(x_ref, g_ref, wg_ref, p_ref, wp_ref, r_ref, o_ref, xn_ref, pb_ref):
    @pl.when(pl.program_id(1) == 0)
    def _():
        xn_ref[...] = _rms_scale(x_ref[...], g_ref[...]).astype(BF16)
        pb_ref[...] = p_ref[...].astype(BF16)

    gate = 1.0 / (1.0 + jnp.exp(-jnp.dot(xn_ref[...], wg_ref[...], preferred_element_type=F32)))
    o_ref[...] = r_ref[...] + jnp.dot(pb_ref[...], wp_ref[...], preferred_element_type=F32) * gate


def per_layer_embedding(h, p, g, w_g, w_p, tm=1024, tn=512):
    t, d = h.shape
    pd = p.shape[1]
    tm, tn = _tile(t, tm), _tile(d, tn)
    return pl.pallas_call(
        _ple_kernel,
        grid=(t // tm, d // tn),
        in_specs=[
            pl.BlockSpec((tm, d), lambda i, j: (i, 0)),
            pl.BlockSpec((1, d), lambda i, j: (0, 0)),
            pl.BlockSpec((d, tn), lambda i, j: (0, j)),
            pl.BlockSpec((tm, pd), lambda i, j: (i, 0)),
            pl.BlockSpec((pd, tn), lambda i, j: (0, j)),
            pl.BlockSpec((tm, tn), lambda i, j: (i, j)),
        ],
        out_specs=pl.BlockSpec((tm, tn), lambda i, j: (i, j)),
        out_shape=jax.ShapeDtypeStruct((t, d), F32),
        scratch_shapes=[pltpu.VMEM((tm, d), BF16), pltpu.VMEM((tm, pd), BF16)],
        compiler_params=_params("parallel", "arbitrary"),
        name="per_layer_embedding",
    )(h, g.reshape(1, d), w_g, p, w_p, h)


def _pad_cols(w, multiple):
    extra = (-w.shape[-1]) % multiple
    return jnp.pad(w, ((0, 0), (0, extra))) if extra else w


def kernel(x, p, attn_norm, ffn_norm, ple_norm, final_norm, sb_w_qkv, sb_w_o, swa_w_qkv, swa_w_o, swa_sinks,
           mla_w_down, mla_q_norm, mla_kv_norm, mla_w_uq, mla_w_ukv, mla_w_o,
           peer_w_q, peer_sub_keys, peer_u, peer_v, ple_w_p, ple_w_g):
    b, s, d = x.shape
    t = b * s
    depth = p.shape[0]
    h = x.reshape(t, d)
    for i in range(depth):
        kind, j = i % N_MIXERS, i // N_MIXERS
        if kind == 0:
            qkv = norm_matmul(h, attn_norm[i], sb_w_qkv[j].astype(BF16), BF16)
            o = sb_attention(qkv.reshape(b, s, -1), SB_HEADS)
            w_o = sb_w_o[j]
        elif kind == 1:
            qkv = norm_matmul(h, attn_norm[i], swa_w_qkv[j].astype(BF16), BF16, tn=1280)
            head_dim = swa_w_o.shape[1] // SWA_HEADS
            o = swa_attention(qkv.reshape(b, s, -1), swa_sinks[j], SWA_HEADS, SWA_KV_HEADS, head_dim)
            w_o = swa_w_o[j]
        else:
            down = norm_matmul(h, attn_norm[i], _pad_cols(mla_w_down[j], LANES).astype(BF16), F32, tn=2048)
            qk_dim = MLA_NOPE_DIM + MLA_ROPE_DIM
            w_uq_pad = jnp.pad(mla_w_uq[j].reshape(MLA_Q_RANK, MLA_HEADS, qk_dim),
                               ((0, 0), (0, 0), (0, 2 * LANES - qk_dim))).reshape(MLA_Q_RANK, -1)
            q, kv, k_rope = mla_projections(down, mla_q_norm[j], mla_kv_norm[j], w_uq_pad.astype(BF16),
                                            mla_w_ukv[j].astype(BF16), s)
            o = mla_attention(q.reshape(b, s, -1), kv.reshape(b, s, -1), k_rope.reshape(b, s, -1), MLA_HEADS)
            w_o = mla_w_o[j]
        h = matmul_residual(o.reshape(t, -1), w_o.astype(BF16), h)
        qh = norm_matmul(h, ffn_norm[i], peer_w_q[i].astype(BF16), BF16)
        scores_t, factors_t, thr = peer_select(qh, peer_sub_keys[i].astype(BF16))
        h = peer_dense(h, ffn_norm[i], peer_u[i].astype(BF16), peer_v[i].astype(BF16), scores_t, factors_t, thr)
        h = per_layer_embedding(h, p[i].reshape(t, -1), ple_norm[i], ple_w_g[i].astype(BF16), ple_w_p[i].astype(BF16))
    return final_rms_norm(h, final_norm).reshape(b, s, d)
```

```python
import functools
import math

import jax
import jax.numpy as jnp
from jax import lax
from jax.experimental import pallas as pl
from jax.experimental.pallas import tpu as pltpu

F32 = jnp.float32
BF16 = jnp.bfloat16

RMS_EPS = 1e-6
N_MIXERS = 3
SB_HEADS = 16
SWA_HEADS = 32
SWA_KV_HEADS = 4
SWA_WINDOW = 128
MLA_HEADS = 16
MLA_Q_RANK = 512
MLA_KV_RANK = 512
MLA_NOPE_DIM = 128
MLA_ROPE_DIM = 64
MLA_V_DIM = 128
ROPE_THETA = 10000.0
PEER_N_KEYS = 64
PEER_TOPK = 16

NORM_MARGIN = 1.02
DENOMINATOR_FLOOR = 2.0 ** -80
EXP2_UNDERFLOW = -150.0
LANES = 128
V7X_VMEM_BYTES = 64 * 1024 * 1024
VMEM_LIMIT = V7X_VMEM_BYTES * 7 // 8

_NT = (((1,), (1,)), ((), ()))


def _params(*semantics):
    return pltpu.CompilerParams(dimension_semantics=semantics, vmem_limit_bytes=VMEM_LIMIT)


def _tile(n, want):
    if n <= want:
        return n
    t = (want // LANES) * LANES
    while t > LANES and n % t:
        t -= LANES
    assert n % t == 0, (n, want)
    return t


def _rms_scale(x, g):
    ms = jnp.mean(x * x, axis=-1, keepdims=True)
    return x * lax.rsqrt(ms + RMS_EPS) * g


def _norm_matmul_kernel(x_ref, g_ref, w_ref, o_ref, xn_ref):
    @pl.when(pl.program_id(1) == 0)
    def _():
        xn_ref[...] = _rms_scale(x_ref[...], g_ref[...]).astype(BF16)

    o_ref[...] = jnp.dot(xn_ref[...], w_ref[...], preferred_element_type=F32).astype(o_ref.dtype)


def norm_matmul(x, g, w, out_dtype, tm=1024, tn=1024):
    m, k = x.shape
    n = w.shape[1]
    tm, tn = _tile(m, tm), _tile(n, tn)
    return pl.pallas_call(
        _norm_matmul_kernel,
        grid=(m // tm, n // tn),
        in_specs=[
            pl.BlockSpec((tm, k), lambda i, j: (i, 0)),
            pl.BlockSpec((1, k), lambda i, j: (0, 0)),
            pl.BlockSpec((k, tn), lambda i, j: (0, j)),
        ],
        out_specs=pl.BlockSpec((tm, tn), lambda i, j: (i, j)),
        out_shape=jax.ShapeDtypeStruct((m, n), out_dtype),
        scratch_shapes=[pltpu.VMEM((tm, k), BF16)],
        compiler_params=_params("parallel", "arbitrary"),
        name="norm_matmul",
    )(x, g.reshape(1, k), w)


def _matmul_residual_kernel(a_ref, w_ref, r_ref, o_ref):
    o_ref[...] = r_ref[...] + jnp.dot(a_ref[...], w_ref[...], preferred_element_type=F32)


def matmul_residual(a, w, res, tm=1024, tn=1024):
    m, k = a.shape
    n = w.shape[1]
    tm, tn = _tile(m, tm), _tile(n, tn)
    return pl.pallas_call(
        _matmul_residual_kernel,
        grid=(m // tm, n // tn),
        in_specs=[
            pl.BlockSpec((tm, k), lambda i, j: (i, 0)),
            pl.BlockSpec((k, tn), lambda i, j: (0, j)),
            pl.BlockSpec((tm, tn), lambda i, j: (i, j)),
        ],
        out_specs=pl.BlockSpec((tm, tn), lambda i, j: (i, j)),
        out_shape=jax.ShapeDtypeStruct((m, n), F32),
        compiler_params=_params("parallel", "arbitrary"),
        name="matmul_residual",
    )(a, w, res)


def _final_norm_kernel(x_ref, g_ref, o_ref):
    o_ref[...] = _rms_scale(x_ref[...], g_ref[...])


def final_rms_norm(x, g, tm=1024):
    m, k = x.shape
    tm = _tile(m, tm)
    return pl.pallas_call(
        _final_norm_kernel,
        grid=(m // tm,),
        in_specs=[pl.BlockSpec((tm, k), lambda i: (i, 0)), pl.BlockSpec((1, k), lambda i: (0, 0))],
        out_specs=pl.BlockSpec((tm, k), lambda i: (i, 0)),
        out_shape=jax.ShapeDtypeStruct((m, k), F32),
        compiler_params=_params("parallel"),
        name="final_norm",
    )(x, g.reshape(1, k))


def _sb_attn_kernel(q_ref, k_ref, v_ref, o_ref, kmax2_ref, *, bq, bk, d, heads, scale):
    i = pl.program_id(2)
    n_diag = bq // bk
    row = lax.broadcasted_iota(jnp.int32, (bk, bk), 0)
    col = lax.broadcasted_iota(jnp.int32, (bk, bk), 1)
    suffix_ones = (row >= col).astype(BF16)
    q_idx = lax.broadcasted_iota(jnp.int32, (bq, bk), 0)
    k_idx = lax.broadcasted_iota(jnp.int32, (bq, bk), 1)

    def scores(hh, m):
        lanes = slice(hh * d, (hh + 1) * d)
        kb = k_ref[0, pl.ds(pl.multiple_of(m * bk, bk), bk), lanes]
        return lax.dot_general(q_ref[0, :, lanes], kb, _NT, preferred_element_type=F32)

    def absorb(hh, m, qk, acc, log_run, key_offset):
        vb = v_ref[0, pl.ds(pl.multiple_of(m * bk, bk), bk), hh * d:(hh + 1) * d]
        z = qk * scale
        neg_z = -z
        log_keep = jnp.minimum(neg_z, 0.0) - jnp.log2(1.0 + jnp.exp2(jnp.minimum(z, neg_z)))
        if key_offset is not None:
            strict = k_idx + key_offset < q_idx
            log_keep = jnp.where(strict, log_keep, 0.0)
        incl = jnp.dot(log_keep.astype(BF16), suffix_ones, preferred_element_type=F32)
        logw = z + incl + log_run
        if key_offset is not None:
            logw = jnp.where(strict, logw, -jnp.inf)
        p = jnp.exp2(logw)
        acc = acc + jnp.dot(p.astype(BF16), vb, preferred_element_type=F32)
        return acc, log_run + incl[:, 0:1]

    def step(m, state, key_offset):
        return tuple(absorb(hh, m, scores(hh, m), *state[hh], key_offset) for hh in range(heads))

    @pl.when(i == 0)
    def _():
        for hh in range(heads):
            kf = k_ref[0, :, hh * d:(hh + 1) * d].astype(F32)
            k_norm2 = jnp.max(jnp.sum(kf * kf, axis=-1, keepdims=True), axis=0, keepdims=True)
            kmax2_ref[hh] = jnp.broadcast_to(k_norm2, kmax2_ref.shape[1:])

    logit_bound = []
    for hh in range(heads):
        qf = q_ref[0, :, hh * d:(hh + 1) * d].astype(F32)
        q_norm2 = jnp.sum(qf * qf, axis=-1, keepdims=True)
        logit_bound.append(jnp.sqrt(q_norm2 * kmax2_ref[hh][0:1, 0:1]) * (scale * 1.01) + 1.0)

    def any_weight_left(state):
        worst = [jnp.max(state[hh][1] + logit_bound[hh]) for hh in range(heads)]
        return (functools.reduce(jnp.maximum, worst) >= EXP2_UNDERFLOW).astype(jnp.int32)

    state = tuple((jnp.zeros((bq, d), F32), jnp.zeros((bq, 1), F32)) for _ in range(heads))
    for dd in reversed(range(n_diag)):
        state = step(i * n_diag + dd, state, dd * bk)
    n_before = i * n_diag

    def more(c):
        t, go, _ = c
        return jnp.logical_and(t < n_before, go > 0)

    def walk(c):
        t, _, state = c
        state = step(n_before - 1 - t, state, None)
        return t + 1, any_weight_left(state), state

    _, _, state = lax.while_loop(more, walk, (jnp.int32(0), any_weight_left(state), state))
    for hh in range(heads):
        o_ref[0, :, hh * d:(hh + 1) * d] = state[hh][0].astype(o_ref.dtype)


def sb_attention(qkv, n_heads, bq=512, bk=256, heads_per_step=4):
    b, s, three_hd = qkv.shape
    d = three_hd // (3 * n_heads)
    assert d == LANES
    bq = min(bq, s)
    bk = min(bk, bq)
    hp = heads_per_step
    assert s % bq == 0 and bq % bk == 0 and n_heads % hp == 0
    n_grp = n_heads // hp
    kern = functools.partial(_sb_attn_kernel, bq=bq, bk=bk, d=d, heads=hp, scale=d ** -0.5 * math.log2(math.e))
    return pl.pallas_call(
        kern,
        grid=(b, n_grp, s // bq),
        in_specs=[
            pl.BlockSpec((1, bq, hp * d), lambda bi, h, i: (bi, i, h)),
            pl.BlockSpec((1, s, hp * d), lambda bi, h, i: (bi, 0, n_grp + h)),
            pl.BlockSpec((1, s, hp * d), lambda bi, h, i: (bi, 0, 2 * n_grp + h)),
        ],
        out_specs=pl.BlockSpec((1, bq, hp * d), lambda bi, h, i: (bi, i, h)),
        out_shape=jax.ShapeDtypeStruct((b, s, n_heads * d), BF16),
        scratch_shapes=[pltpu.VMEM((hp, 8, LANES), F32)],
        compiler_params=_params("parallel", "parallel", "arbitrary"),
        name="sb_attention",
    )(qkv, qkv, qkv)


def _swa_kernel(q_ref, kp_ref, kc_ref, vp_ref, vc_ref, bias_ref, sink_ref, o_ref, *, n_kv, grp, d, blk, scale):
    i = pl.program_id(1)
    rows = grp * blk
    kcol = lax.broadcasted_iota(jnp.int32, (rows, 2 * blk), 1)
    has_key = kcol >= jnp.where(i > 0, 0, blk)
    ones_k = jnp.ones((d, LANES), BF16)
    ones_v = jnp.ones((2 * blk, LANES - d), BF16)

    def operands(kh):
        qg = jnp.concatenate(
            [q_ref[0, :, (kh * grp + g) * d:(kh * grp + g + 1) * d] for g in range(grp)], axis=0)
        kk = jnp.concatenate([kp_ref[0, :, kh * d:(kh + 1) * d], kc_ref[0, :, kh * d:(kh + 1) * d]], axis=0)
        vv = jnp.concatenate([vp_ref[0, :, kh * d:(kh + 1) * d], vc_ref[0, :, kh * d:(kh + 1) * d]], axis=0)
        return qg, kk, vv

    def store(outs):
        o_ref[0] = jnp.concatenate(outs, axis=1).astype(o_ref.dtype)

    outs, smallest = [], None
    for kh in range(n_kv):
        qg, kk, vv = operands(kh)
        z = lax.dot_general(qg, kk, _NT, preferred_element_type=F32) * scale + bias_ref[kh]
        q_norm2 = jnp.dot(qg * qg, ones_k, preferred_element_type=F32)
        kf = kk.astype(F32)
        k_norm2 = jnp.max(jnp.sum(kf * kf, axis=-1, keepdims=True), axis=0, keepdims=True)
        bound = jnp.sqrt(q_norm2 * k_norm2) * (scale * NORM_MARGIN)
        p = jnp.where(has_key, jnp.exp(z - jnp.concatenate([bound, bound], axis=1)), 0.0)
        pv = jnp.dot(p.astype(BF16), jnp.concatenate([vv, ones_v], axis=1), preferred_element_type=F32)
        den = pv[:, d:2 * d] + jnp.exp(sink_ref[kh][:, :d] - bound[:, :d])
        o = pv[:, :d] / den
        low = jnp.min(den)
        smallest = low if smallest is None else jnp.minimum(smallest, low)
        outs.extend(o[g * blk:(g + 1) * blk] for g in range(grp))
    store(outs)

    @pl.when(jnp.logical_not(smallest >= DENOMINATOR_FLOOR))
    def _():
        outs = []
        for kh in range(n_kv):
            qg, kk, vv = operands(kh)
            z = lax.dot_general(qg, kk, _NT, preferred_element_type=F32) * scale + bias_ref[kh]
            z = jnp.where(has_key, z, -jnp.inf)
            sink = sink_ref[kh][:, 0:1]
            mx = jnp.maximum(jnp.max(z, axis=-1, keepdims=True), sink)
            p = jnp.exp(z - mx)
            den = jnp.sum(p, axis=-1, keepdims=True) + jnp.exp(sink - mx)
            o = jnp.dot(p.astype(BF16), vv, preferred_element_type=F32) / den
            outs.extend(o[g * blk:(g + 1) * blk] for g in range(grp))
        store(outs)


def swa_attention(qkv, sinks, n_heads, n_kv, d):
    b, s, _ = qkv.shape
    blk = SWA_WINDOW
    grp = n_heads // n_kv
    qd, kd = n_heads * d, n_kv * d
    assert qd % kd == 0 and kd % LANES == 0 and s % blk == 0 and d < LANES
    slopes = 2.0 ** (-8.0 * jnp.arange(1, n_heads + 1, dtype=F32) / n_heads)
    dist = (jnp.arange(blk)[:, None] + blk - jnp.arange(2 * blk)[None, :]).astype(F32)
    bias = jnp.where((dist >= 0) & (dist < SWA_WINDOW), -slopes[:, None, None] * dist, -jnp.inf)
    bias = bias.reshape(n_kv, grp * blk, 2 * blk)
    sink_rows = jnp.broadcast_to(
        jnp.repeat(sinks.astype(F32).reshape(n_kv, grp), blk, axis=1)[:, :, None], (n_kv, grp * blk, LANES))
    kern = functools.partial(_swa_kernel, n_kv=n_kv, grp=grp, d=d, blk=blk, scale=d ** -0.5)
    prev = lambda bi, i: (bi, jnp.maximum(i - 1, 0), qd // kd)
    cur = lambda bi, i: (bi, i, qd // kd)
    prev_v = lambda bi, i: (bi, jnp.maximum(i - 1, 0), qd // kd + 1)
    cur_v = lambda bi, i: (bi, i, qd // kd + 1)
    const = lambda bi, i: (0, 0, 0)
    return pl.pallas_call(
        kern,
        grid=(b, s // blk),
        in_specs=[
            pl.BlockSpec((1, blk, qd), lambda bi, i: (bi, i, 0)),
            pl.BlockSpec((1, blk, kd), prev),
            pl.BlockSpec((1, blk, kd), cur),
            pl.BlockSpec((1, blk, kd), prev_v),
            pl.BlockSpec((1, blk, kd), cur_v),
            pl.BlockSpec((n_kv, grp * blk, 2 * blk), const),
            pl.BlockSpec((n_kv, grp * blk, LANES), const),
        ],
        out_specs=pl.BlockSpec((1, blk, qd), lambda bi, i: (bi, i, 0)),
        out_shape=jax.ShapeDtypeStruct((b, s, qd), BF16),
        compiler_params=_params("parallel", "arbitrary"),
        name="swa_attention",
    )(qkv, qkv, qkv, qkv, qkv, bias, sink_rows)


def _rope_lanes(r, cos_ref, sin_lo_ref, sin_hi_ref):
    half = MLA_ROPE_DIM // 2
    return (r * cos_ref[...]
            + pltpu.roll(r, LANES - half, 1) * sin_lo_ref[...]
            + pltpu.roll(r, half, 1) * sin_hi_ref[...])


def _mla_q_kernel(x_ref, g_ref, w_ref, cos_ref, slo_ref, shi_ref, o_ref, *, n_heads):
    xn = _rms_scale(x_ref[...], g_ref[...]).astype(BF16)
    lane = lax.broadcasted_iota(jnp.int32, (x_ref.shape[0], LANES), 1)
    for h in range(n_heads):
        a = jnp.dot(xn, w_ref[:, 2 * LANES * h:2 * LANES * (h + 1)], preferred_element_type=F32)
        nope = a[:, :LANES]
        rope = _rope_lanes(a[:, LANES:], cos_ref, slo_ref, shi_ref)
        norm2 = jnp.sum(nope * nope + rope * rope, axis=-1, keepdims=True)
        o_ref[:, 2 * LANES * h:2 * LANES * h + LANES] = nope.astype(BF16)
        o_ref[:, 2 * LANES * h + LANES:2 * LANES * (h + 1)] = jnp.where(
            lane == MLA_ROPE_DIM, NORM_MARGIN * jnp.sqrt(norm2), rope).astype(BF16)


def _mla_kv_kernel(x_ref, g_ref, xr_ref, w_ref, cos_ref, slo_ref, shi_ref, kv_ref, kr_ref, *, chunk):
    xn = _rms_scale(x_ref[...], g_ref[...]).astype(BF16)
    for c in range(w_ref.shape[1] // chunk):
        cols = slice(c * chunk, (c + 1) * chunk)
        kv_ref[:, cols] = jnp.dot(xn, w_ref[:, cols], preferred_element_type=F32).astype(BF16)
    kr_ref[...] = _rope_lanes(xr_ref[...], cos_ref, slo_ref, shi_ref).astype(BF16)


def _rope_tables(s):
    half = MLA_ROPE_DIM // 2
    inv_freq = ROPE_THETA ** (-jnp.arange(half, dtype=F32) / half)
    ang = jnp.arange(s, dtype=F32)[:, None] * inv_freq[None, :]
    cos, sin = jnp.cos(ang), jnp.sin(ang)
    zeros = jnp.zeros((s, LANES - 2 * half), F32)
    z_half = jnp.zeros((s, half), F32)
    cos_t = jnp.concatenate([cos, cos, zeros], axis=1)
    sin_lo = jnp.concatenate([-sin, z_half, zeros], axis=1)
    sin_hi = jnp.concatenate([z_half, sin, zeros], axis=1)
    return cos_t, sin_lo, sin_hi


def mla_projections(down, g_q, g_kv, w_uq_pad, w_ukv, seq, tm=512):
    t = down.shape[0]
    tm = _tile(seq, tm)
    n_heads = w_uq_pad.shape[1] // (2 * LANES)
    tables = _rope_tables(seq)
    n_pos = seq // tm
    tab_spec = pl.BlockSpec((tm, LANES), lambda i: (i % n_pos, 0))
    q = pl.pallas_call(
        functools.partial(_mla_q_kernel, n_heads=n_heads),
        grid=(t // tm,),
        in_specs=[
            pl.BlockSpec((tm, MLA_Q_RANK), lambda i: (i, 0)),
            pl.BlockSpec((1, MLA_Q_RANK), lambda i: (0, 0)),
            pl.BlockSpec(w_uq_pad.shape, lambda i: (0, 0)),
            tab_spec, tab_spec, tab_spec,
        ],
        out_specs=pl.BlockSpec((tm, w_uq_pad.shape[1]), lambda i: (i, 0)),
        out_shape=jax.ShapeDtypeStruct((t, w_uq_pad.shape[1]), BF16),
        compiler_params=_params("parallel"),
        name="mla_q_proj",
    )(down, g_q.reshape(1, -1), w_uq_pad, *tables)
    assert MLA_Q_RANK == MLA_KV_RANK and (MLA_Q_RANK + MLA_KV_RANK) % LANES == 0
    kv, k_rope = pl.pallas_call(
        functools.partial(_mla_kv_kernel, chunk=_tile(w_ukv.shape[1], 1024)),
        grid=(t // tm,),
        in_specs=[
            pl.BlockSpec((tm, MLA_KV_RANK), lambda i: (i, 1)),
            pl.BlockSpec((1, MLA_KV_RANK), lambda i: (0, 0)),
            pl.BlockSpec((tm, LANES), lambda i: (i, (MLA_Q_RANK + MLA_KV_RANK) // LANES)),
            pl.BlockSpec(w_ukv.shape, lambda i: (0, 0)),
            tab_spec, tab_spec, tab_spec,
        ],
        out_specs=[pl.BlockSpec((tm, w_ukv.shape[1]), lambda i: (i, 0)),
                   pl.BlockSpec((tm, LANES), lambda i: (i, 0))],
        out_shape=[jax.ShapeDtypeStruct((t, w_ukv.shape[1]), BF16), jax.ShapeDtypeStruct((t, LANES), BF16)],
        compiler_params=_params("parallel"),
        name="mla_kv_proj",
    )(down, g_kv.reshape(1, -1), down, w_ukv, *tables)
    return q, kv, k_rope


def _mla_attn_kernel(q_ref, kv_ref, kr_ref, o_ref, k_ref, v_ref, *, bq, bk, heads, scale):
    i = pl.program_id(2)
    width = 2 * LANES

    @pl.when(i == 0)
    def _():
        kr = kr_ref[0]
        krf = kr.astype(F32)
        kr_norm2 = jnp.sum(krf * krf, axis=-1, keepdims=True)
        lane = lax.broadcasted_iota(jnp.int32, kr.shape, 1)
        for hh in range(heads):
            kn = kv_ref[0, :, hh * width:hh * width + LANES]
            knf = kn.astype(F32)
            k_max = jnp.sqrt(jnp.max(jnp.sum(knf * knf, axis=-1, keepdims=True) + kr_norm2, axis=0, keepdims=True))
            k_ref[hh, :, :LANES] = kn
            k_ref[hh, :, LANES:] = jnp.where(lane == MLA_ROPE_DIM, (-NORM_MARGIN * k_max).astype(BF16), kr)
            v_ref[hh, :, :LANES] = kv_ref[0, :, hh * width + LANES:(hh + 1) * width]
            v_ref[hh, :, LANES:] = jnp.ones((kr.shape[0], LANES), BF16)

    n_diag = bq // bk
    n_before = i * n_diag
    row = lax.broadcasted_iota(jnp.int32, (bk, bk), 0)
    col = lax.broadcasted_iota(jnp.int32, (bk, bk), 1)
    causal = col <= row

    def shifted_logits(hh, m, first_row):
        kb = k_ref[hh, pl.ds(pl.multiple_of(m * bk, bk), bk), :]
        q = q_ref[0, first_row:, hh * width:(hh + 1) * width]
        return lax.dot_general(q, kb, _NT, preferred_element_type=F32)

    def values(hh, m):
        return v_ref[hh, pl.ds(pl.multiple_of(m * bk, bk), bk), :]

    def quick_step(hh, m, acc, dd):
        first_row = 0 if dd is None else dd * bk
        p = jnp.exp2(shifted_logits(hh, m, first_row) * (scale * math.log2(math.e)))
        if dd is not None:
            on_diagonal = jnp.where(causal, p[:bk], 0.0)
            p = on_diagonal if p.shape[0] == bk else jnp.concatenate([on_diagonal, p[bk:]], axis=0)
        pv = jnp.dot(p.astype(BF16), values(hh, m), preferred_element_type=F32)
        return acc + pv if first_row == 0 else jnp.concatenate([acc[:first_row], acc[first_row:] + pv], axis=0)

    acc = tuple(jnp.zeros((bq, width), F32) for _ in range(heads))
    acc = lax.fori_loop(
        0, n_before, lambda m, a: tuple(quick_step(hh, m, a[hh], None) for hh in range(heads)), acc)
    for dd in range(n_diag):
        acc = tuple(quick_step(hh, n_before + dd, acc[hh], dd) for hh in range(heads))
    smallest = functools.reduce(jnp.minimum, [jnp.min(acc[hh][:, LANES:]) for hh in range(heads)])
    for hh in range(heads):
        o_ref[0, :, hh * LANES:(hh + 1) * LANES] = (acc[hh][:, :LANES] / acc[hh][:, LANES:]).astype(o_ref.dtype)

    @pl.when(jnp.logical_not(smallest >= DENOMINATOR_FLOOR))
    def _():
        def careful_step(hh, m, state, dd):
            acc, mx, den = state
            z = shifted_logits(hh, m, 0) * scale
            if dd is not None:
                q_idx = lax.broadcasted_iota(jnp.int32, (bq, bk), 0)
                k_idx = lax.broadcasted_iota(jnp.int32, (bq, bk), 1)
                z = jnp.where(k_idx + dd * bk <= q_idx, z, -jnp.inf)
            new_mx = jnp.maximum(mx, jnp.max(z, axis=-1, keepdims=True))
            corr = jnp.exp(mx - new_mx)
            pz = jnp.exp(z - new_mx)
            den = den * corr + jnp.sum(pz, axis=-1, keepdims=True)
            pv = jnp.dot(pz.astype(BF16), values(hh, m)[:, :LANES], preferred_element_type=F32)
            return acc * corr + pv, new_mx, den

        for hh in range(heads):
            state = (jnp.zeros((bq, LANES), F32), jnp.full((bq, 1), -jnp.inf, F32), jnp.zeros((bq, 1), F32))
            state = lax.fori_loop(0, n_before, lambda m, c: careful_step(hh, m, c, None), state)
            for dd in range(n_diag):
                state = careful_step(hh, n_before + dd, state, dd)
            o_ref[0, :, hh * LANES:(hh + 1) * LANES] = (state[0] / state[2]).astype(o_ref.dtype)


def mla_attention(q, kv, k_rope, n_heads, bq=1024, bk=1024, heads_per_step=2):
    b, s, _ = q.shape
    bq = min(bq, s)
    bk = min(bk, bq)
    hp = heads_per_step
    assert s % bq == 0 and bq % bk == 0 and n_heads % hp == 0 and MLA_V_DIM == LANES
    scale = (MLA_NOPE_DIM + MLA_ROPE_DIM) ** -0.5
    return pl.pallas_call(
        functools.partial(_mla_attn_kernel, bq=bq, bk=bk, heads=hp, scale=scale),
        grid=(b, n_heads // hp, s // bq),
        in_specs=[
            pl.BlockSpec((1, bq, hp * 2 * LANES), lambda bi, h, i: (bi, i, h)),
            pl.BlockSpec((1, s, hp * 2 * LANES), lambda bi, h, i: (bi, 0, h)),
            pl.BlockSpec((1, s, LANES), lambda bi, h, i: (bi, 0, 0)),
        ],
        out_specs=pl.BlockSpec((1, bq, hp * LANES), lambda bi, h, i: (bi, i, h)),
        out_shape=jax.ShapeDtypeStruct((b, s, n_heads * MLA_V_DIM), BF16),
        scratch_shapes=[pltpu.VMEM((hp, s, 2 * LANES), BF16), pltpu.VMEM((hp, s, 2 * LANES), BF16)],
        compiler_params=_params("parallel", "parallel", "arbitrary"),
        name="mla_attention",
    )(q, kv, k_rope)


def _extract_top(xs, out_refs, k):
    def drop_all(r, ys):
        out = []
        for y, out_ref in zip(ys, out_refs):
            m = jnp.max(y, axis=0, keepdims=True)
            out_ref[pl.ds(r, 1), :] = m
            out.append(jnp.where(y == m, -jnp.inf, y))
        return tuple(out)

    lax.fori_loop(0, k, drop_all, tuple(xs))
    most = [jnp.max(jnp.sum((x >= out_ref[k - 1:k, :]).astype(F32), axis=0, keepdims=True))
            for x, out_ref in zip(xs, out_refs)]

    @pl.when(functools.reduce(jnp.maximum, most) > k)
    def _():
        def drop_first(r, ys):
            out = []
            for y, out_ref in zip(ys, out_refs):
                rows = lax.broadcasted_iota(jnp.int32, y.shape, 0).astype(F32)
                m = jnp.max(y, axis=0, keepdims=True)
                out_ref[pl.ds(r, 1), :] = m
                first = jnp.min(jnp.where(y == m, rows, float(y.shape[0])), axis=0, keepdims=True)
                out.append(jnp.where(rows == first, -jnp.inf, y))
            return tuple(out)

        lax.fori_loop(0, k, drop_first, tuple(xs))


def _peer_select_kernel(qh_ref, keys_ref, st_ref, et_ref, thr_ref, v_ref, top_ref, *, heads, n_keys, key_half, topk):
    halves = []
    for hh in range(heads):
        for half in range(2):
            lanes = slice((2 * hh + half) * key_half, (2 * hh + half + 1) * key_half)
            s_t = lax.dot_general(keys_ref[half], qh_ref[:, lanes], _NT, preferred_element_type=F32)
            st_ref[hh, half * n_keys:(half + 1) * n_keys, :] = s_t
            halves.append(s_t)
    for c in range(2 * heads):
        _extract_top([halves[c]], [v_ref.at[c]], topk)
    assert topk == 16
    candidates = []
    for hh in range(heads):
        v1, v2 = v_ref[2 * hh], v_ref[2 * hh + 1]
        slabs = [v1 + v2[0:1]]
        slabs += [v1[0:8] + v2[b:b + 1] for b in range(1, 8)]
        slabs += [v1[0:1] + v2[8:16]]
        candidates.append(jnp.concatenate(slabs, axis=0))
    for hh in range(heads):
        _extract_top([candidates[hh]], [top_ref.at[hh]], topk)
    for hh in range(heads):
        top = top_ref[hh]
        thr_ref[hh] = top[topk - 1:topk]
        z_sum = jnp.sum(jnp.exp(top - top[0:1]), axis=0, keepdims=True)
        et_ref[hh, :n_keys, :] = jnp.exp(halves[2 * hh] - v_ref[2 * hh][0:1])
        et_ref[hh, n_keys:, :] = jnp.exp(halves[2 * hh + 1] - v_ref[2 * hh + 1][0:1]) / z_sum


def peer_select(qh, sub_keys, tt=512, heads_per_step=2):
    t = qh.shape[0]
    _, n_keys, key_half = sub_keys.shape
    n_heads = qh.shape[1] // (2 * key_half)
    tt = _tile(t, tt)
    hp = heads_per_step
    assert n_heads % hp == 0
    kern = functools.partial(_peer_select_kernel, heads=hp, n_keys=n_keys, key_half=key_half, topk=PEER_TOPK)
    return pl.pallas_call(
        kern,
        grid=(t // tt, n_heads // hp),
        in_specs=[
            pl.BlockSpec((tt, hp * 2 * key_half), lambda i, h: (i, h)),
            pl.BlockSpec(sub_keys.shape, lambda i, h: (0, 0, 0)),
        ],
        out_specs=[
            pl.BlockSpec((hp, 2 * n_keys, tt), lambda i, h: (h, 0, i)),
            pl.BlockSpec((hp, 2 * n_keys, tt), lambda i, h: (h, 0, i)),
            pl.BlockSpec((hp, 1, tt), lambda i, h: (h, 0, i)),
        ],
        out_shape=[
            jax.ShapeDtypeStruct((n_heads, 2 * n_keys, t), F32),
            jax.ShapeDtypeStruct((n_heads, 2 * n_keys, t), F32),
            jax.ShapeDtypeStruct((n_heads, 1, t), F32),
        ],
        scratch_shapes=[pltpu.VMEM((2 * hp, PEER_TOPK, tt), F32), pltpu.VMEM((hp, PEER_TOPK, tt), F32)],
        compiler_params=_params("parallel", "arbitrary"),
        name="peer_select",
    )(qh, sub_keys)


def _gelu_tanh(x):
    return 0.5 * x * (1.0 + jnp.tanh(math.sqrt(2.0 / math.pi) * (x + 0.044715 * (x * x * x))))


def _peer_dense_kernel(h_ref, g_ref, u_ref, v_ref, st_ref, et_ref, thr_ref, o_ref, xn_ref, *, n_heads, n_keys, te):
    e = pl.program_id(1)

    @pl.when(e == 0)
    def _():
        x = h_ref[...]
        xn_ref[...] = _rms_scale(x, g_ref[...]).astype(BF16)
        o_ref[...] = x

    hidden = lax.dot_general(xn_ref[...], u_ref[...], _NT, preferred_element_type=F32)
    rows_per_step = te // n_keys
    blocks = []
    for ii in range(rows_per_step):
        i_row = e * rows_per_step + ii
        gate_t = None
        for h in range(n_heads):
            score = st_ref[h, n_keys:2 * n_keys, :] + st_ref[h, pl.ds(i_row, 1), :]
            weight = et_ref[h, n_keys:2 * n_keys, :] * et_ref[h, pl.ds(i_row, 1), :]
            g = jnp.where(score >= thr_ref[h], weight, 0.0)
            gate_t = g if gate_t is None else gate_t + g
        blocks.append(gate_t)
    gate = jnp.concatenate(blocks, axis=0).T
    gh = (gate * _gelu_tanh(hidden)).astype(BF16)
    o_ref[...] += jnp.dot(gh, v_ref[...], preferred_element_type=F32)


def peer_dense(h, g, u, v, scores_t, factors_t, thr, tt=512, te=1024):
    t, d = h.shape
    n_exp = u.shape[0]
    n_heads, two_keys, _ = scores_t.shape
    n_keys = two_keys // 2
    tt, te = _tile(t, tt), _tile(n_exp, te)
    assert te % n_keys == 0 and n_exp == n_keys * n_keys
    kern = functools.partial(_peer_dense_kernel, n_heads=n_heads, n_keys=n_keys, te=te)
    return pl.pallas_call(
        kern,
        grid=(t // tt, n_exp // te),
        in_specs=[
            pl.BlockSpec((tt, d), lambda i, e: (i, 0)),
            pl.BlockSpec((1, d), lambda i, e: (0, 0)),
            pl.BlockSpec((te, d), lambda i, e: (e, 0)),
            pl.BlockSpec((te, d), lambda i, e: (e, 0)),
            pl.BlockSpec((n_heads, two_keys, tt), lambda i, e: (0, 0, i)),
            pl.BlockSpec((n_heads, two_keys, tt), lambda i, e: (0, 0, i)),
            pl.BlockSpec((n_heads, 1, tt), lambda i, e: (0, 0, i)),
        ],
        out_specs=pl.BlockSpec((tt, d), lambda i, e: (i, 0)),
        out_shape=jax.ShapeDtypeStruct((t, d), F32),
        scratch_shapes=[pltpu.VMEM((tt, d), BF16)],
        compiler_params=_params("parallel", "arbitrary"),
        name="peer_dense",
    )(h, g.reshape(1, d), u, v, scores_t, factors_t, thr)


def _ple_kernel(x_ref, g_ref, wg_ref, p_ref, wp_ref, r_ref, o_ref, xn_ref, pb_ref):
    @pl.when(pl.program_id(1) == 0)
    def _():
        xn_ref[...] = _rms_scale(x_ref[...], g_ref[...]).astype(BF16)
        pb_ref[...] = p_ref[...].astype(BF16)

    gate = 1.0 / (1.0 + jnp.exp(-jnp.dot(xn_ref[...], wg_ref[...], preferred_element_type=F32)))
    o_ref[...] = r_ref[...] + jnp.dot(pb_ref[...], wp_ref[...], preferred_element_type=F32) * gate


def per_layer_embedding(h, p, g, w_g, w_p, tm=1024, tn=512):
    t, d = h.shape
    pd = p.shape[1]
    tm, tn = _tile(t, tm), _tile(d, tn)
    return pl.pallas_call(
        _ple_kernel,
        grid=(t // tm, d // tn),
        in_specs=[
            pl.BlockSpec((tm, d), lambda i, j: (i, 0)),
            pl.BlockSpec((1, d), lambda i, j: (0, 0)),
            pl.BlockSpec((d, tn), lambda i, j: (0, j)),
            pl.BlockSpec((tm, pd), lambda i, j: (i, 0)),
            pl.BlockSpec((pd, tn), lambda i, j: (0, j)),
            pl.BlockSpec((tm, tn), lambda i, j: (i, j)),
        ],
        out_specs=pl.BlockSpec((tm, tn), lambda i, j: (i, j)),
        out_shape=jax.ShapeDtypeStruct((t, d), F32),
        scratch_shapes=[pltpu.VMEM((tm, d), BF16), pltpu.VMEM((tm, pd), BF16)],
        compiler_params=_params("parallel", "arbitrary"),
        name="per_layer_embedding",
    )(h, g.reshape(1, d), w_g, p, w_p, h)


def _pad_cols(w, multiple):
    extra = (-w.shape[-1]) % multiple
    return jnp.pad(w, ((0, 0), (0, extra))) if extra else w


def kernel(x, p, attn_norm, ffn_norm, ple_norm, final_norm, sb_w_qkv, sb_w_o, swa_w_qkv, swa_w_o, swa_sinks,
           mla_w_down, mla_q_norm, mla_kv_norm, mla_w_uq, mla_w_ukv, mla_w_o,
           peer_w_q, peer_sub_keys, peer_u, peer_v, ple_w_p, ple_w_g):
    b, s, d = x.shape
    t = b * s
    depth = p.shape[0]
    h = x.reshape(t, d)
    for i in range(depth):
        kind, j = i % N_MIXERS, i // N_MIXERS
        if kind == 0:
            qkv = norm_matmul(h, attn_norm[i], sb_w_qkv[j].astype(BF16), BF16)
            o = sb_attention(qkv.reshape(b, s, -1), SB_HEADS)
            w_o = sb_w_o[j]
        elif kind == 1:
            qkv = norm_matmul(h, attn_norm[i], swa_w_qkv[j].astype(BF16), BF16, tn=1280)
            head_dim = swa_w_o.shape[1] // SWA_HEADS
            o = swa_attention(qkv.reshape(b, s, -1), swa_sinks[j], SWA_HEADS, SWA_KV_HEADS, head_dim)
            w_o = swa_w_o[j]
        else:
            down = norm_matmul(h, attn_norm[i], _pad_cols(mla_w_down[j], LANES).astype(BF16), F32, tn=2048)
            qk_dim = MLA_NOPE_DIM + MLA_ROPE_DIM
            w_uq_pad = jnp.pad(mla_w_uq[j].reshape(MLA_Q_RANK, MLA_HEADS, qk_dim),
                               ((0, 0), (0, 0), (0, 2 * LANES - qk_dim))).reshape(MLA_Q_RANK, -1)
            q, kv, k_rope = mla_projections(down, mla_q_norm[j], mla_kv_norm[j], w_uq_pad.astype(BF16),
                                            mla_w_ukv[j].astype(BF16), s)
            o = mla_attention(q.reshape(b, s, -1), kv.reshape(b, s, -1), k_rope.reshape(b, s, -1), MLA_HEADS)
            w_o = mla_w_o[j]
        h = matmul_residual(o.reshape(t, -1), w_o.astype(BF16), h)
        qh = norm_matmul(h, ffn_norm[i], peer_w_q[i].astype(BF16), BF16)
        scores_t, factors_t, thr = peer_select(qh, peer_sub_keys[i].astype(BF16))
        h = peer_dense(h, ffn_norm[i], peer_u[i].astype(BF16), peer_v[i].astype(BF16), scores_t, factors_t, thr)
        h = per_layer_embedding(h, p[i].reshape(t, -1), ple_norm[i], ple_w_g[i].astype(BF16), ple_w_p[i].astype(BF16))
    return final_rms_norm(h, final_norm).reshape(b, s, d)
```

```python
import functools
import math

import jax
import jax.numpy as jnp
from jax import lax
from jax.experimental import pallas as pl
from jax.experimental.pallas import tpu as pltpu

F32 = jnp.float32
BF16 = jnp.bfloat16

RMS_EPS = 1e-6
N_MIXERS = 3
SB_HEADS = 16
SWA_HEADS = 32
SWA_KV_HEADS = 4
SWA_WINDOW = 128
MLA_HEADS = 16
MLA_Q_RANK = 512
MLA_KV_RANK = 512
MLA_NOPE_DIM = 128
MLA_ROPE_DIM = 64
MLA_V_DIM = 128
ROPE_THETA = 10000.0
PEER_TOPK = 16

NORM_MARGIN = 1.02
DENOMINATOR_FLOOR = 2.0 ** -80
EXP2_UNDERFLOW = -150.0
LANES = 128
V7X_VMEM_BYTES = 64 * 1024 * 1024
VMEM_LIMIT = V7X_VMEM_BYTES * 7 // 8

_NT = (((1,), (1,)), ((), ()))


def _params(*semantics):
    return pltpu.CompilerParams(dimension_semantics=semantics, vmem_limit_bytes=VMEM_LIMIT)


def _tile(n, want):
    if n <= want:
        return n
    t = (want // LANES) * LANES
    while t > LANES and n % t:
        t -= LANES
    assert n % t == 0, (n, want)
    return t


def _rms_scale(x, g):
    ms = jnp.mean(x * x, axis=-1, keepdims=True)
    return x * lax.rsqrt(ms + RMS_EPS) * g


def _norm_matmul_kernel(x_ref, g_ref, w_ref, o_ref, xn_ref):
    @pl.when(pl.program_id(1) == 0)
    def _():
        xn_ref[...] = _rms_scale(x_ref[...], g_ref[...]).astype(BF16)

    o_ref[...] = jnp.dot(xn_ref[...], w_ref[...], preferred_element_type=F32).astype(o_ref.dtype)


def norm_matmul(x, g, w, out_dtype, tm=1024, tn=1024):
    m, k = x.shape
    n = w.shape[1]
    tm, tn = _tile(m, tm), _tile(n, tn)
    return pl.pallas_call(
        _norm_matmul_kernel,
        grid=(m // tm, n // tn),
        in_specs=[
            pl.BlockSpec((tm, k), lambda i, j: (i, 0)),
            pl.BlockSpec((1, k), lambda i, j: (0, 0)),
            pl.BlockSpec((k, tn), lambda i, j: (0, j)),
        ],
        out_specs=pl.BlockSpec((tm, tn), lambda i, j: (i, j)),
        out_shape=jax.ShapeDtypeStruct((m, n), out_dtype),
        scratch_shapes=[pltpu.VMEM((tm, k), BF16)],
        compiler_params=_params("parallel", "arbitrary"),
        name="norm_matmul",
    )(x, g.reshape(1, k), w)


def _matmul_residual_kernel(a_ref, w_ref, r_ref, o_ref):
    o_ref[...] = r_ref[...] + jnp.dot(a_ref[...], w_ref[...], preferred_element_type=F32)


def matmul_residual(a, w, res, tm=1024, tn=1024):
    m, k = a.shape
    n = w.shape[1]
    tm, tn = _tile(m, tm), _tile(n, tn)
    return pl.pallas_call(
        _matmul_residual_kernel,
        grid=(m // tm, n // tn),
        in_specs=[
            pl.BlockSpec((tm, k), lambda i, j: (i, 0)),
            pl.BlockSpec((k, tn), lambda i, j: (0, j)),
            pl.BlockSpec((tm, tn), lambda i, j: (i, j)),
        ],
        out_specs=pl.BlockSpec((tm, tn), lambda i, j: (i, j)),
        out_shape=jax.ShapeDtypeStruct((m, n), F32),
        compiler_params=_params("parallel", "arbitrary"),
        name="matmul_residual",
    )(a, w, res)


def _final_norm_kernel(x_ref, g_ref, o_ref):
    o_ref[...] = _rms_scale(x_ref[...], g_ref[...])


def final_rms_norm(x, g, tm=1024):
    m, k = x.shape
    tm = _tile(m, tm)
    return pl.pallas_call(
        _final_norm_kernel,
        grid=(m // tm,),
        in_specs=[pl.BlockSpec((tm, k), lambda i: (i, 0)), pl.BlockSpec((1, k), lambda i: (0, 0))],
        out_specs=pl.BlockSpec((tm, k), lambda i: (i, 0)),
        out_shape=jax.ShapeDtypeStruct((m, k), F32),
        compiler_params=_params("parallel"),
        name="final_norm",
    )(x, g.reshape(1, k))


def _sb_attn_kernel(q_ref, k_ref, v_ref, o_ref, kmax2_ref, *, bq, bk, d, heads, scale):
    i = pl.program_id(2)
    n_diag = bq // bk
    row = lax.broadcasted_iota(jnp.int32, (bk, bk), 0)
    col = lax.broadcasted_iota(jnp.int32, (bk, bk), 1)
    suffix_ones = (row >= col).astype(BF16)
    q_idx = lax.broadcasted_iota(jnp.int32, (bq, bk), 0)
    k_idx = lax.broadcasted_iota(jnp.int32, (bq, bk), 1)

    def scores(hh, m):
        lanes = slice(hh * d, (hh + 1) * d)
        kb = k_ref[0, pl.ds(pl.multiple_of(m * bk, bk), bk), lanes]
        return lax.dot_general(q_ref[0, :, lanes], kb, _NT, preferred_element_type=F32)

    def absorb(hh, m, qk, acc, log_run, key_offset):
        vb = v_ref[0, pl.ds(pl.multiple_of(m * bk, bk), bk), hh * d:(hh + 1) * d]
        z = qk * scale
        neg_z = -z
        log_keep = jnp.minimum(neg_z, 0.0) - jnp.log2(1.0 + jnp.exp2(jnp.minimum(z, neg_z)))
        if key_offset is not None:
            strict = k_idx + key_offset < q_idx
            log_keep = jnp.where(strict, log_keep, 0.0)
        incl = jnp.dot(log_keep.astype(BF16), suffix_ones, preferred_element_type=F32)
        logw = z + incl + log_run
        if key_offset is not None:
            logw = jnp.where(strict, logw, -jnp.inf)
        p = jnp.exp2(logw)
        acc = acc + jnp.dot(p.astype(BF16), vb, preferred_element_type=F32)
        return acc, log_run + incl[:, 0:1]

    def step(m, state, key_offset):
        return tuple(absorb(hh, m, scores(hh, m), *state[hh], key_offset) for hh in range(heads))

    @pl.when(i == 0)
    def _():
        for hh in range(heads):
            kf = k_ref[0, :, hh * d:(hh + 1) * d].astype(F32)
            k_norm2 = jnp.max(jnp.sum(kf * kf, axis=-1, keepdims=True), axis=0, keepdims=True)
            kmax2_ref[hh] = jnp.broadcast_to(k_norm2, kmax2_ref.shape[1:])

    logit_bound = []
    for hh in range(heads):
        qf = q_ref[0, :, hh * d:(hh + 1) * d].astype(F32)
        q_norm2 = jnp.sum(qf * qf, axis=-1, keepdims=True)
        logit_bound.append(jnp.sqrt(q_norm2 * kmax2_ref[hh][0:1, 0:1]) * (scale * NORM_MARGIN))

    def any_weight_left(state):
        worst = [jnp.max(state[hh][1] + logit_bound[hh]) for hh in range(heads)]
        return (functools.reduce(jnp.maximum, worst) >= EXP2_UNDERFLOW).astype(jnp.int32)

    state = tuple((jnp.zeros((bq, d), F32), jnp.zeros((bq, 1), F32)) for _ in range(heads))
    for dd in reversed(range(n_diag)):
        state = step(i * n_diag + dd, state, dd * bk)
    n_before = i * n_diag

    def more(c):
        t, go, _ = c
        return jnp.logical_and(t < n_before, go > 0)

    def walk(c):
        t, _, state = c
        state = step(n_before - 1 - t, state, None)
        return t + 1, any_weight_left(state), state

    _, _, state = lax.while_loop(more, walk, (jnp.int32(0), any_weight_left(state), state))
    for hh in range(heads):
        o_ref[0, :, hh * d:(hh + 1) * d] = state[hh][0].astype(o_ref.dtype)


def sb_attention(qkv, n_heads, bq=512, bk=256, heads_per_step=4):
    b, s, three_hd = qkv.shape
    d = three_hd // (3 * n_heads)
    assert d == LANES
    bq = min(bq, s)
    bk = min(bk, bq)
    hp = math.gcd(heads_per_step, n_heads)
    assert s % bq == 0 and bq % bk == 0
    n_grp = n_heads // hp
    kern = functools.partial(_sb_attn_kernel, bq=bq, bk=bk, d=d, heads=hp, scale=d ** -0.5 * math.log2(math.e))
    return pl.pallas_call(
        kern,
        grid=(b, n_grp, s // bq),
        in_specs=[
            pl.BlockSpec((1, bq, hp * d), lambda bi, h, i: (bi, i, h)),
            pl.BlockSpec((1, s, hp * d), lambda bi, h, i: (bi, 0, n_grp + h)),
            pl.BlockSpec((1, s, hp * d), lambda bi, h, i: (bi, 0, 2 * n_grp + h)),
        ],
        out_specs=pl.BlockSpec((1, bq, hp * d), lambda bi, h, i: (bi, i, h)),
        out_shape=jax.ShapeDtypeStruct((b, s, n_heads * d), BF16),
        scratch_shapes=[pltpu.VMEM((hp, 8, LANES), F32)],
        compiler_params=_params("parallel", "parallel", "arbitrary"),
        name="sb_attention",
    )(qkv, qkv, qkv)


def _swa_kernel(q_ref, kp_ref, kc_ref, vp_ref, vc_ref, bias_ref, sink_ref, o_ref, *, n_kv, grp, d, blk, scale):
    i = pl.program_id(1)
    rows = grp * blk
    kcol = lax.broadcasted_iota(jnp.int32, (rows, 2 * blk), 1)
    has_key = kcol >= jnp.where(i > 0, 0, blk)
    ones_k = jnp.ones((d, LANES), BF16)
    ones_v = jnp.ones((2 * blk, LANES - d), BF16)

    def operands(kh):
        qg = jnp.concatenate(
            [q_ref[0, :, (kh * grp + g) * d:(kh * grp + g + 1) * d] for g in range(grp)], axis=0)
        kk = jnp.concatenate([kp_ref[0, :, kh * d:(kh + 1) * d], kc_ref[0, :, kh * d:(kh + 1) * d]], axis=0)
        vv = jnp.concatenate([vp_ref[0, :, kh * d:(kh + 1) * d], vc_ref[0, :, kh * d:(kh + 1) * d]], axis=0)
        return qg, kk, vv

    def store(outs):
        o_ref[0] = jnp.concatenate(outs, axis=1).astype(o_ref.dtype)

    outs, smallest = [], None
    for kh in range(n_kv):
        qg, kk, vv = operands(kh)
        z = lax.dot_general(qg, kk, _NT, preferred_element_type=F32) * scale + bias_ref[kh]
        q_norm2 = jnp.dot(qg * qg, ones_k, preferred_element_type=F32)
        kf = kk.astype(F32)
        k_norm2 = jnp.max(jnp.sum(kf * kf, axis=-1, keepdims=True), axis=0, keepdims=True)
        bound = jnp.sqrt(q_norm2 * k_norm2) * (scale * NORM_MARGIN)
        p = jnp.where(has_key, jnp.exp(z - jnp.concatenate([bound, bound], axis=1)), 0.0)
        pv = jnp.dot(p.astype(BF16), jnp.concatenate([vv, ones_v], axis=1), preferred_element_type=F32)
        den = pv[:, d:2 * d] + jnp.exp(sink_ref[kh][:, :d] - bound[:, :d])
        o = pv[:, :d] / den
        low = jnp.min(den)
        smallest = low if smallest is None else jnp.minimum(smallest, low)
        outs.extend(o[g * blk:(g + 1) * blk] for g in range(grp))
    store(outs)

    @pl.when(jnp.logical_not(smallest >= DENOMINATOR_FLOOR))
    def _():
        outs = []
        for kh in range(n_kv):
            qg, kk, vv = operands(kh)
            z = lax.dot_general(qg, kk, _NT, preferred_element_type=F32) * scale + bias_ref[kh]
            z = jnp.where(has_key, z, -jnp.inf)
            sink = sink_ref[kh][:, 0:1]
            mx = jnp.maximum(jnp.max(z, axis=-1, keepdims=True), sink)
            p = jnp.exp(z - mx)
            den = jnp.sum(p, axis=-1, keepdims=True) + jnp.exp(sink - mx)
            o = jnp.dot(p.astype(BF16), vv, preferred_element_type=F32) / den
            outs.extend(o[g * blk:(g + 1) * blk] for g in range(grp))
        store(outs)


def swa_attention(qkv, sinks, n_heads, n_kv, d):
    b, s, _ = qkv.shape
    blk = SWA_WINDOW
    grp = n_heads // n_kv
    qd, kd = n_heads * d, n_kv * d
    assert qd % kd == 0 and kd % LANES == 0 and s % blk == 0 and d < LANES
    slopes = 2.0 ** (-8.0 * jnp.arange(1, n_heads + 1, dtype=F32) / n_heads)
    dist = (jnp.arange(blk)[:, None] + blk - jnp.arange(2 * blk)[None, :]).astype(F32)
    bias = jnp.where((dist >= 0) & (dist < SWA_WINDOW), -slopes[:, None, None] * dist, -jnp.inf)
    bias = bias.reshape(n_kv, grp * blk, 2 * blk)
    sink_rows = jnp.broadcast_to(
        jnp.repeat(sinks.astype(F32).reshape(n_kv, grp), blk, axis=1)[:, :, None], (n_kv, grp * blk, LANES))
    kern = functools.partial(_swa_kernel, n_kv=n_kv, grp=grp, d=d, blk=blk, scale=d ** -0.5)
    prev = lambda bi, i: (bi, jnp.maximum(i - 1, 0), qd // kd)
    cur = lambda bi, i: (bi, i, qd // kd)
    prev_v = lambda bi, i: (bi, jnp.maximum(i - 1, 0), qd // kd + 1)
    cur_v = lambda bi, i: (bi, i, qd // kd + 1)
    const = lambda bi, i: (0, 0, 0)
    return pl.pallas_call(
        kern,
        grid=(b, s // blk),
        in_specs=[
            pl.BlockSpec((1, blk, qd), lambda bi, i: (bi, i, 0)),
            pl.BlockSpec((1, blk, kd), prev),
            pl.BlockSpec((1, blk, kd), cur),
            pl.BlockSpec((1, blk, kd), prev_v),
            pl.BlockSpec((1, blk, kd), cur_v),
            pl.BlockSpec((n_kv, grp * blk, 2 * blk), const),
            pl.BlockSpec((n_kv, grp * blk, LANES), const),
        ],
        out_specs=pl.BlockSpec((1, blk, qd), lambda bi, i: (bi, i, 0)),
        out_shape=jax.ShapeDtypeStruct((b, s, qd), BF16),
        compiler_params=_params("parallel", "arbitrary"),
        name="swa_attention",
    )(qkv, qkv, qkv, qkv, qkv, bias, sink_rows)


def _rope_lanes(r, cos_ref, sin_lo_ref, sin_hi_ref):
    half = MLA_ROPE_DIM // 2
    return (r * cos_ref[...]
            + pltpu.roll(r, LANES - half, 1) * sin_lo_ref[...]
            + pltpu.roll(r, half, 1) * sin_hi_ref[...])


def _mla_q_kernel(x_ref, g_ref, w_ref, cos_ref, slo_ref, shi_ref, o_ref, *, n_heads):
    xn = _rms_scale(x_ref[...], g_ref[...]).astype(BF16)
    lane = lax.broadcasted_iota(jnp.int32, (x_ref.shape[0], LANES), 1)
    for h in range(n_heads):
        a = jnp.dot(xn, w_ref[:, 2 * LANES * h:2 * LANES * (h + 1)], preferred_element_type=F32)
        nope = a[:, :LANES]
        rope = _rope_lanes(a[:, LANES:], cos_ref, slo_ref, shi_ref)
        norm2 = jnp.sum(nope * nope + rope * rope, axis=-1, keepdims=True)
        o_ref[:, 2 * LANES * h:2 * LANES * h + LANES] = nope.astype(BF16)
        o_ref[:, 2 * LANES * h + LANES:2 * LANES * (h + 1)] = jnp.where(
            lane == MLA_ROPE_DIM, NORM_MARGIN * jnp.sqrt(norm2), rope).astype(BF16)


def _mla_kv_kernel(x_ref, g_ref, xr_ref, w_ref, cos_ref, slo_ref, shi_ref, kv_ref, kr_ref, *, chunk):
    xn = _rms_scale(x_ref[...], g_ref[...]).astype(BF16)
    for c in range(w_ref.shape[1] // chunk):
        cols = slice(c * chunk, (c + 1) * chunk)
        kv_ref[:, cols] = jnp.dot(xn, w_ref[:, cols], preferred_element_type=F32).astype(BF16)
    kr_ref[...] = _rope_lanes(xr_ref[...], cos_ref, slo_ref, shi_ref).astype(BF16)


def _rope_tables(s):
    half = MLA_ROPE_DIM // 2
    inv_freq = ROPE_THETA ** (-jnp.arange(half, dtype=F32) / half)
    ang = jnp.arange(s, dtype=F32)[:, None] * inv_freq[None, :]
    cos, sin = jnp.cos(ang), jnp.sin(ang)
    zeros = jnp.zeros((s, LANES - 2 * half), F32)
    z_half = jnp.zeros((s, half), F32)
    cos_t = jnp.concatenate([cos, cos, zeros], axis=1)
    sin_lo = jnp.concatenate([-sin, z_half, zeros], axis=1)
    sin_hi = jnp.concatenate([z_half, sin, zeros], axis=1)
    return cos_t, sin_lo, sin_hi


def mla_projections(down, g_q, g_kv, w_uq_pad, w_ukv, seq, tm=512):
    t = down.shape[0]
    tm = _tile(seq, tm)
    n_heads = w_uq_pad.shape[1] // (2 * LANES)
    tables = _rope_tables(seq)
    n_pos = seq // tm
    tab_spec = pl.BlockSpec((tm, LANES), lambda i: (i % n_pos, 0))
    q = pl.pallas_call(
        functools.partial(_mla_q_kernel, n_heads=n_heads),
        grid=(t // tm,),
        in_specs=[
            pl.BlockSpec((tm, MLA_Q_RANK), lambda i: (i, 0)),
            pl.BlockSpec((1, MLA_Q_RANK), lambda i: (0, 0)),
            pl.BlockSpec(w_uq_pad.shape, lambda i: (0, 0)),
            tab_spec, tab_spec, tab_spec,
        ],
        out_specs=pl.BlockSpec((tm, w_uq_pad.shape[1]), lambda i: (i, 0)),
        out_shape=jax.ShapeDtypeStruct((t, w_uq_pad.shape[1]), BF16),
        compiler_params=_params("parallel"),
        name="mla_q_proj",
    )(down, g_q.reshape(1, -1), w_uq_pad, *tables)
    assert MLA_Q_RANK == MLA_KV_RANK and (MLA_Q_RANK + MLA_KV_RANK) % LANES == 0
    kv, k_rope = pl.pallas_call(
        functools.partial(_mla_kv_kernel, chunk=_tile(w_ukv.shape[1], 1024)),
        grid=(t // tm,),
        in_specs=[
            pl.BlockSpec((tm, MLA_KV_RANK), lambda i: (i, 1)),
            pl.BlockSpec((1, MLA_KV_RANK), lambda i: (0, 0)),
            pl.BlockSpec((tm, LANES), lambda i: (i, (MLA_Q_RANK + MLA_KV_RANK) // LANES)),
            pl.BlockSpec(w_ukv.shape, lambda i: (0, 0)),
            tab_spec, tab_spec, tab_spec,
        ],
        out_specs=[pl.BlockSpec((tm, w_ukv.shape[1]), lambda i: (i, 0)),
                   pl.BlockSpec((tm, LANES), lambda i: (i, 0))],
        out_shape=[jax.ShapeDtypeStruct((t, w_ukv.shape[1]), BF16), jax.ShapeDtypeStruct((t, LANES), BF16)],
        compiler_params=_params("parallel"),
        name="mla_kv_proj",
    )(down, g_kv.reshape(1, -1), down, w_ukv, *tables)
    return q, kv, k_rope


def _mla_attn_kernel(q_ref, kv_ref, kr_ref, o_ref, k_ref, v_ref, *, bq, bk, bd, heads, scale):
    i = pl.program_id(2)
    width = 2 * LANES

    @pl.when(i == 0)
    def _():
        kr = kr_ref[0]
        krf = kr.astype(F32)
        kr_norm2 = jnp.sum(krf * krf, axis=-1, keepdims=True)
        lane = lax.broadcasted_iota(jnp.int32, kr.shape, 1)
        for hh in range(heads):
            kn = kv_ref[0, :, hh * width:hh * width + LANES]
            knf = kn.astype(F32)
            k_max = jnp.sqrt(jnp.max(jnp.sum(knf * knf, axis=-1, keepdims=True) + kr_norm2, axis=0, keepdims=True))
            k_ref[hh, :, :LANES] = kn
            k_ref[hh, :, LANES:] = jnp.where(lane == MLA_ROPE_DIM, (-NORM_MARGIN * k_max).astype(BF16), kr)
            v_ref[hh, :, :LANES] = kv_ref[0, :, hh * width + LANES:(hh + 1) * width]
            v_ref[hh, :, LANES:] = jnp.ones((kr.shape[0], LANES), BF16)

    n_diag = bq // bk
    n_before = i * n_diag
    row = lax.broadcasted_iota(jnp.int32, (bd, bd), 0)
    col = lax.broadcasted_iota(jnp.int32, (bd, bd), 1)
    causal = col <= row

    def shifted_logits(hh, start, size, first_row):
        kb = k_ref[hh, pl.ds(start, size), :]
        q = q_ref[0, first_row:, hh * width:(hh + 1) * width]
        return lax.dot_general(q, kb, _NT, preferred_element_type=F32)

    def values(hh, start, size):
        return v_ref[hh, pl.ds(start, size), :]

    def quick_step(hh, start, size, acc, first_row):
        p = jnp.exp2(shifted_logits(hh, start, size, first_row or 0) * (scale * math.log2(math.e)))
        if first_row is not None:
            on_diagonal = jnp.where(causal, p[:size], 0.0)
            p = on_diagonal if p.shape[0] == size else jnp.concatenate([on_diagonal, p[size:]], axis=0)
        pv = jnp.dot(p.astype(BF16), values(hh, start, size), preferred_element_type=F32)
        return acc + pv if not first_row else jnp.concatenate([acc[:first_row], acc[first_row:] + pv], axis=0)

    acc = tuple(jnp.zeros((bq, width), F32) for _ in range(heads))
    acc = lax.fori_loop(
        0, n_before,
        lambda m, a: tuple(quick_step(hh, pl.multiple_of(m * bk, bk), bk, a[hh], None) for hh in range(heads)), acc)
    for dd in range(bq // bd):
        start = pl.multiple_of(i * bq + dd * bd, bd)
        acc = tuple(quick_step(hh, start, bd, acc[hh], dd * bd) for hh in range(heads))
    smallest = functools.reduce(jnp.minimum, [jnp.min(acc[hh][:, LANES:]) for hh in range(heads)])
    for hh in range(heads):
        o_ref[0, :, hh * LANES:(hh + 1) * LANES] = (acc[hh][:, :LANES] / acc[hh][:, LANES:]).astype(o_ref.dtype)

    @pl.when(jnp.logical_not(smallest >= DENOMINATOR_FLOOR))
    def _():
        def careful_step(hh, m, state, dd):
            acc, mx, den = state
            start = pl.multiple_of(m * bk, bk)
            z = shifted_logits(hh, start, bk, 0) * scale
            if dd is not None:
                q_idx = lax.broadcasted_iota(jnp.int32, (bq, bk), 0)
                k_idx = lax.broadcasted_iota(jnp.int32, (bq, bk), 1)
                z = jnp.where(k_idx + dd * bk <= q_idx, z, -jnp.inf)
            new_mx = jnp.maximum(mx, jnp.max(z, axis=-1, keepdims=True))
            corr = jnp.exp(mx - new_mx)
            pz = jnp.exp(z - new_mx)
            den = den * corr + jnp.sum(pz, axis=-1, keepdims=True)
            pv = jnp.dot(pz.astype(BF16), values(hh, start, bk)[:, :LANES], preferred_element_type=F32)
            return acc * corr + pv, new_mx, den

        for hh in range(heads):
            state = (jnp.zeros((bq, LANES), F32), jnp.full((bq, 1), -jnp.inf, F32), jnp.zeros((bq, 1), F32))
            state = lax.fori_loop(0, n_before, lambda m, c: careful_step(hh, m, c, None), state)
            for dd in range(n_diag):
                state = careful_step(hh, n_before + dd, state, dd)
            o_ref[0, :, hh * LANES:(hh + 1) * LANES] = (state[0] / state[2]).astype(o_ref.dtype)


def mla_attention(q, kv, k_rope, n_heads, bq=1024, bk=1024, bd=512, heads_per_step=2):
    b, s, _ = q.shape
    bq = min(bq, s)
    bk = min(bk, bq)
    hp = heads_per_step
    bd = min(bd, bq)
    assert s % bq == 0 and bq % bk == 0 and bq % bd == 0 and n_heads % hp == 0 and MLA_V_DIM == LANES
    scale = (MLA_NOPE_DIM + MLA_ROPE_DIM) ** -0.5
    return pl.pallas_call(
        functools.partial(_mla_attn_kernel, bq=bq, bk=bk, bd=bd, heads=hp, scale=scale),
        grid=(b, n_heads // hp, s // bq),
        in_specs=[
            pl.BlockSpec((1, bq, hp * 2 * LANES), lambda bi, h, i: (bi, i, h)),
            pl.BlockSpec((1, s, hp * 2 * LANES), lambda bi, h, i: (bi, 0, h)),
            pl.BlockSpec((1, s, LANES), lambda bi, h, i: (bi, 0, 0)),
        ],
        out_specs=pl.BlockSpec((1, bq, hp * LANES), lambda bi, h, i: (bi, i, h)),
        out_shape=jax.ShapeDtypeStruct((b, s, n_heads * MLA_V_DIM), BF16),
        scratch_shapes=[pltpu.VMEM((hp, s, 2 * LANES), BF16), pltpu.VMEM((hp, s, 2 * LANES), BF16)],
        compiler_params=_params("parallel", "parallel", "arbitrary"),
        name="mla_attention",
    )(q, kv, k_rope)


def _extract_top(xs, out_refs, k):
    def drop_all(r, ys):
        out = []
        for y, out_ref in zip(ys, out_refs):
            m = jnp.max(y, axis=0, keepdims=True)
            out_ref[pl.ds(r, 1), :] = m
            out.append(jnp.where(y == m, -jnp.inf, y))
        return tuple(out)

    lax.fori_loop(0, k, drop_all, tuple(xs))
    most = [jnp.max(jnp.sum((x >= out_ref[k - 1:k, :]).astype(F32), axis=0, keepdims=True))
            for x, out_ref in zip(xs, out_refs)]

    @pl.when(functools.reduce(jnp.maximum, most) > k)
    def _():
        def drop_first(r, ys):
            out = []
            for y, out_ref in zip(ys, out_refs):
                rows = lax.broadcasted_iota(jnp.int32, y.shape, 0).astype(F32)
                m = jnp.max(y, axis=0, keepdims=True)
                out_ref[pl.ds(r, 1), :] = m
                first = jnp.min(jnp.where(y == m, rows, float(y.shape[0])), axis=0, keepdims=True)
                out.append(jnp.where(rows == first, -jnp.inf, y))
            return tuple(out)

        lax.fori_loop(0, k, drop_first, tuple(xs))


def _peer_select_kernel(qh_ref, keys_ref, st_ref, et_ref, thr_ref, v_ref, top_ref, *, heads, n_keys, key_half, topk):
    halves = []
    for hh in range(heads):
        for half in range(2):
            lanes = slice((2 * hh + half) * key_half, (2 * hh + half + 1) * key_half)
            s_t = lax.dot_general(keys_ref[half], qh_ref[:, lanes], _NT, preferred_element_type=F32)
            st_ref[hh, half * n_keys:(half + 1) * n_keys, :] = s_t
            halves.append(s_t)
    for c in range(2 * heads):
        _extract_top([halves[c]], [v_ref.at[c]], topk)
    assert topk == 16
    candidates = []
    for hh in range(heads):
        v1, v2 = v_ref[2 * hh], v_ref[2 * hh + 1]
        slabs = [v1 + v2[0:1]]
        slabs += [v1[0:8] + v2[b:b + 1] for b in range(1, 8)]
        slabs += [v1[0:1] + v2[8:16]]
        candidates.append(jnp.concatenate(slabs, axis=0))
    for hh in range(heads):
        _extract_top([candidates[hh]], [top_ref.at[hh]], topk)
    for hh in range(heads):
        top = top_ref[hh]
        thr_ref[hh] = top[topk - 1:topk]
        z_sum = jnp.sum(jnp.exp(top - top[0:1]), axis=0, keepdims=True)
        et_ref[hh, :n_keys, :] = jnp.exp(halves[2 * hh] - v_ref[2 * hh][0:1])
        et_ref[hh, n_keys:, :] = jnp.exp(halves[2 * hh + 1] - v_ref[2 * hh + 1][0:1]) / z_sum


def peer_select(qh, sub_keys, tt=512, heads_per_step=2):
    t = qh.shape[0]
    _, n_keys, key_half = sub_keys.shape
    n_heads = qh.shape[1] // (2 * key_half)
    tt = _tile(t, tt)
    hp = heads_per_step
    assert n_heads % hp == 0
    kern = functools.partial(_peer_select_kernel, heads=hp, n_keys=n_keys, key_half=key_half, topk=PEER_TOPK)
    return pl.pallas_call(
        kern,
        grid=(t // tt, n_heads // hp),
        in_specs=[
            pl.BlockSpec((tt, hp * 2 * key_half), lambda i, h: (i, h)),
            pl.BlockSpec(sub_keys.shape, lambda i, h: (0, 0, 0)),
        ],
        out_specs=[
            pl.BlockSpec((hp, 2 * n_keys, tt), lambda i, h: (h, 0, i)),
            pl.BlockSpec((hp, 2 * n_keys, tt), lambda i, h: (h, 0, i)),
            pl.BlockSpec((hp, 1, tt), lambda i, h: (h, 0, i)),
        ],
        out_shape=[
            jax.ShapeDtypeStruct((n_heads, 2 * n_keys, t), F32),
            jax.ShapeDtypeStruct((n_heads, 2 * n_keys, t), F32),
            jax.ShapeDtypeStruct((n_heads, 1, t), F32),
        ],
        scratch_shapes=[pltpu.VMEM((2 * hp, PEER_TOPK, tt), F32), pltpu.VMEM((hp, PEER_TOPK, tt), F32)],
        compiler_params=_params("parallel", "arbitrary"),
        name="peer_select",
    )(qh, sub_keys)


def _gelu_tanh(x):
    return 0.5 * x * (1.0 + jnp.tanh(math.sqrt(2.0 / math.pi) * (x + 0.044715 * (x * x * x))))


def _peer_dense_kernel(h_ref, g_ref, u_ref, v_ref, st_ref, et_ref, thr_ref, o_ref, xn_ref, *, n_heads, n_keys, te):
    e = pl.program_id(1)

    @pl.when(e == 0)
    def _():
        x = h_ref[...]
        xn_ref[...] = _rms_scale(x, g_ref[...]).astype(BF16)
        o_ref[...] = x

    hidden = lax.dot_general(xn_ref[...], u_ref[...], _NT, preferred_element_type=F32)
    rows_per_step = te // n_keys
    blocks = []
    for ii in range(rows_per_step):
        i_row = e * rows_per_step + ii
        gate_t = None
        for h in range(n_heads):
            score = st_ref[h, n_keys:2 * n_keys, :] + st_ref[h, pl.ds(i_row, 1), :]
            weight = et_ref[h, n_keys:2 * n_keys, :] * et_ref[h, pl.ds(i_row, 1), :]
            g = jnp.where(score >= thr_ref[h], weight, 0.0)
            gate_t = g if gate_t is None else gate_t + g
        blocks.append(gate_t)
    gate = jnp.concatenate(blocks, axis=0).T
    gh = (gate * _gelu_tanh(hidden)).astype(BF16)
    o_ref[...] += jnp.dot(gh, v_ref[...], preferred_element_type=F32)


def peer_dense(h, g, u, v, scores_t, factors_t, thr, tt=512, te=1024):
    t, d = h.shape
    n_exp = u.shape[0]
    n_heads, two_keys, _ = scores_t.shape
    n_keys = two_keys // 2
    tt, te = _tile(t, tt), _tile(n_exp, te)
    assert te % n_keys == 0 and n_exp == n_keys * n_keys
    kern = functools.partial(_peer_dense_kernel, n_heads=n_heads, n_keys=n_keys, te=te)
    return pl.pallas_call(
        kern,
        grid=(t // tt, n_exp // te),
        in_specs=[
            pl.BlockSpec((tt, d), lambda i, e: (i, 0)),
            pl.BlockSpec((1, d), lambda i, e: (0, 0)),
            pl.BlockSpec((te, d), lambda i, e: (e, 0)),
            pl.BlockSpec((te, d), lambda i, e: (e, 0)),
            pl.BlockSpec((n_heads, two_keys, tt), lambda i, e: (0, 0, i)),
            pl.BlockSpec((n_heads, two_keys, tt), lambda i, e: (0, 0, i)),
            pl.BlockSpec((n_heads, 1, tt), lambda i, e: (0, 0, i)),
        ],
        out_specs=pl.BlockSpec((tt, d), lambda i, e: (i, 0)),
        out_shape=jax.ShapeDtypeStruct((t, d), F32),
        scratch_shapes=[pltpu.VMEM((tt, d), BF16)],
        compiler_params=_params("parallel", "arbitrary"),
        name="peer_dense",
    )(h, g.reshape(1, d), u, v, scores_t, factors_t, thr)


def _ple_kernel(x_ref, g_ref, wg_ref, p_ref, wp_ref, r_ref, o_ref, xn_ref, pb_ref):
    @pl.when(pl.program_id(1) == 0)
    def _():
        xn_ref[...] = _rms_scale(x_ref[...], g_ref[...]).astype(BF16)
        pb_ref[...] = p_ref[...].astype(BF16)

    gate = 1.0 / (1.0 + jnp.exp(-jnp.dot(xn_ref[...], wg_ref[...], preferred_element_type=F32)))
    o_ref[...] = r_ref[...] + jnp.dot(pb_ref[...], wp_ref[...], preferred_element_type=F32) * gate


def per_layer_embedding(h, p, g, w_g, w_p, tm=1024, tn=512):
    t, d = h.shape
    pd = p.shape[1]
    tm, tn = _tile(t, tm), _tile(d, tn)
    return pl.pallas_call(
        _ple_kernel,
        grid=(t // tm, d // tn),
        in_specs=[
            pl.BlockSpec((tm, d), lambda i, j: (i, 0)),
            pl.BlockSpec((1, d), lambda i, j: (0, 0)),
            pl.BlockSpec((d, tn), lambda i, j: (0, j)),
            pl.BlockSpec((tm, pd), lambda i, j: (i, 0)),
            pl.BlockSpec((pd, tn), lambda i, j: (0, j)),
            pl.BlockSpec((tm, tn), lambda i, j: (i, j)),
        ],
        out_specs=pl.BlockSpec((tm, tn), lambda i, j: (i, j)),
        out_shape=jax.ShapeDtypeStruct((t, d), F32),
        scratch_shapes=[pltpu.VMEM((tm, d), BF16), pltpu.VMEM((tm, pd), BF16)],
        compiler_params=_params("parallel", "arbitrary"),
        name="per_layer_embedding",
    )(h, g.reshape(1, d), w_g, p, w_p, h)


def _pad_cols(w, multiple):
    extra = (-w.shape[-1]) % multiple
    return jnp.pad(w, ((0, 0), (0, extra))) if extra else w


def kernel(x, p, attn_norm, ffn_norm, ple_norm, final_norm, sb_w_qkv, sb_w_o, swa_w_qkv, swa_w_o, swa_sinks,
           mla_w_down, mla_q_norm, mla_kv_norm, mla_w_uq, mla_w_ukv, mla_w_o,
           peer_w_q, peer_sub_keys, peer_u, peer_v, ple_w_p, ple_w_g):
    b, s, d = x.shape
    t = b * s
    depth = p.shape[0]
    h = x.reshape(t, d)
    for i in range(depth):
        kind, j = i % N_MIXERS, i // N_MIXERS
        if kind == 0:
            qkv = norm_matmul(h, attn_norm[i], sb_w_qkv[j].astype(BF16), BF16)
            o = sb_attention(qkv.reshape(b, s, -1), SB_HEADS)
            w_o = sb_w_o[j]
        elif kind == 1:
            qkv = norm_matmul(h, attn_norm[i], swa_w_qkv[j].astype(BF16), BF16, tn=1280)
            head_dim = swa_w_o.shape[1] // SWA_HEADS
            o = swa_attention(qkv.reshape(b, s, -1), swa_sinks[j], SWA_HEADS, SWA_KV_HEADS, head_dim)
            w_o = swa_w_o[j]
        else:
            down = norm_matmul(h, attn_norm[i], _pad_cols(mla_w_down[j], LANES).astype(BF16), F32, tn=2048)
            qk_dim = MLA_NOPE_DIM + MLA_ROPE_DIM
            w_uq_pad = jnp.pad(mla_w_uq[j].reshape(MLA_Q_RANK, MLA_HEADS, qk_dim),
                               ((0, 0), (0, 0), (0, 2 * LANES - qk_dim))).reshape(MLA_Q_RANK, -1)
            q, kv, k_rope = mla_projections(down, mla_q_norm[j], mla_kv_norm[j], w_uq_pad.astype(BF16),
                                            mla_w_ukv[j].astype(BF16), s)
            o = mla_attention(q.reshape(b, s, -1), kv.reshape(b, s, -1), k_rope.reshape(b, s, -1), MLA_HEADS)
            w_o = mla_w_o[j]
        h = matmul_residual(o.reshape(t, -1), w_o.astype(BF16), h)
        qh = norm_matmul(h, ffn_norm[i], peer_w_q[i].astype(BF16), BF16)
        scores_t, factors_t, thr = peer_select(qh, peer_sub_keys[i].astype(BF16))
        h = peer_dense(h, ffn_norm[i], peer_u[i].astype(BF16), peer_v[i].astype(BF16), scores_t, factors_t, thr)
        h = per_layer_embedding(h, p[i].reshape(t, -1), ple_norm[i], ple_w_g[i].astype(BF16), ple_w_p[i].astype(BF16))
    return final_rms_norm(h, final_norm).reshape(b, s, d)
```
